```python
import functools
import jax, jax.numpy as jnp
from jax import lax
import numpy as np

D_MODEL = 1024
BATCH = 32
SEQ = 256
DEPTH = 2
DEC_BATCH = 8
DEC_SEQ = 4096
PAST_LEN = 512

GRID_W = 64
CONV_DIM = 256
CONV_WIDTH = 31
CONV_PAD = CONV_WIDTH // 2
N_HEADS = 8
N_KV_HEADS = 2
HEAD_DIM = 64
Q_PER_KV = N_HEADS // N_KV_HEADS
Q_DIM = N_HEADS * HEAD_DIM
KV_DIM = N_KV_HEADS * HEAD_DIM
WINDOW = 128
ATTN_BLOCK = 128
SPAN = ATTN_BLOCK + 2 * WINDOW
ATTN_SCALE = HEAD_DIM ** -0.5
ROPE_AXIS_DIM = HEAD_DIM // 2
ROPE_BASE = 10000.0
NEG_INF = -1e30
FOURIER_GROUPS = 4
FOURIER_GROUP_DIM = 64
FOURIER_DIM = FOURIER_GROUPS * FOURIER_GROUP_DIM
IN_SPLITS = (CONV_DIM, 2 * CONV_DIM, 2 * CONV_DIM + Q_DIM, 2 * CONV_DIM + Q_DIM + KV_DIM,
             2 * CONV_DIM + Q_DIM + 2 * KV_DIM, 2 * CONV_DIM + Q_DIM + 2 * KV_DIM + FOURIER_DIM)
IN_COLS = IN_SPLITS[-1] + 3 * D_MODEL
N_EXPERTS = 64
TOP_K = 6
EXPERT_DIM = 256
SHARED_DIM = 256
ROUTED_SCALE = 2.5
EXPERT_BLOCK = 128
LN_EPS = 1e-6
DEEPNORM_ALPHA = (2 * DEPTH) ** 0.25
DEEPNORM_BETA = (8 * DEPTH) ** -0.25

kernel_name = "hybrid_diffusion_conv_swa_fnet_moe_step"


def _layer_norm(x, g=None, b=None):
    xf = x.astype(jnp.float32)
    mu = jnp.mean(xf, axis=-1, keepdims=True)
    var = jnp.mean(jnp.square(xf - mu), axis=-1, keepdims=True)
    y = (xf - mu) * lax.rsqrt(var + LN_EPS)
    if g is not None:
        y = y * g.astype(jnp.float32) + b.astype(jnp.float32)
    return y.astype(x.dtype)


def _rope_1d(x, pos):
    half = ROPE_AXIS_DIM // 2
    inv = ROPE_BASE ** (-jnp.arange(half, dtype=jnp.float32) / half)
    ang = pos.astype(jnp.float32)[:, None] * inv[None, :]
    cos = jnp.cos(ang)[None, :, None, :]
    sin = jnp.sin(ang)[None, :, None, :]
    xf = x.astype(jnp.float32)
    x1, x2 = xf[..., :half], xf[..., half:]
    return jnp.concatenate([x1 * cos - x2 * sin, x1 * sin + x2 * cos], axis=-1).astype(x.dtype)


def _axial_rope(x, row_pos, col_pos):
    return jnp.concatenate([_rope_1d(x[..., :ROPE_AXIS_DIM], row_pos),
                            _rope_1d(x[..., ROPE_AXIS_DIM:], col_pos)], axis=-1)


def _sink_attend(qb, keys, vals, sink, mask):
    s = jnp.einsum('bqhgd,bkhd->bhgqk', qb, keys).astype(jnp.float32) * ATTN_SCALE
    if mask is not None:
        s = jnp.where(mask, s, NEG_INF)
    sink_col = jnp.broadcast_to(sink.astype(jnp.float32).reshape(1, N_KV_HEADS, Q_PER_KV, 1, 1),
                                s.shape[:-1] + (1,))
    p = jax.nn.softmax(jnp.concatenate([sink_col, s], axis=-1), axis=-1)[..., 1:]
    return jnp.einsum('bhgqk,bkhd->bqhgd', p.astype(vals.dtype), vals)


def _context_attention(q, k, v, sink):
    b, n_ctx = q.shape[:2]
    nb = n_ctx // ATTN_BLOCK
    qb = q.reshape(b, nb, ATTN_BLOCK, N_KV_HEADS, Q_PER_KV, HEAD_DIM).swapaxes(0, 1)
    out = lax.map(lambda qi: _sink_attend(qi, k, v, sink, None), qb)
    return out.swapaxes(0, 1).reshape(b, n_ctx, Q_DIM)


def _latent_attention(q, k, v, sink, k_ctx, v_ctx, row_pos, col_pos):
    q = _axial_rope(q, row_pos, col_pos)
    k = _axial_rope(k, row_pos, col_pos)
    b, n_lat = q.shape[:2]
    nb = n_lat // ATTN_BLOCK
    qb = q.reshape(b, nb, ATTN_BLOCK, N_KV_HEADS, Q_PER_KV, HEAD_DIM).swapaxes(0, 1)
    pad = ((0, 0), (WINDOW, WINDOW), (0, 0), (0, 0))
    kp = jnp.pad(k, pad)
    vp = jnp.pad(v, pad)
    ctx_valid = jnp.ones((ATTN_BLOCK, k_ctx.shape[1]), dtype=bool)

    def block(args):
        i, qi = args
        start = i * ATTN_BLOCK
        kw = lax.dynamic_slice_in_dim(kp, start, SPAN, axis=1)
        vw = lax.dynamic_slice_in_dim(vp, start, SPAN, axis=1)
        qpos = start + jnp.arange(ATTN_BLOCK)
        kpos = start - WINDOW + jnp.arange(SPAN)
        win = ((kpos >= 0) & (kpos < n_lat))[None, :] & (jnp.abs(qpos[:, None] - kpos[None, :]) <= WINDOW)
        mask = jnp.concatenate([win, ctx_valid], axis=1)
        return _sink_attend(qi, jnp.concatenate([kw, k_ctx], axis=1),
                            jnp.concatenate([vw, v_ctx], axis=1), sink, mask)

    out = lax.map(block, (jnp.arange(nb), qb))
    return out.swapaxes(0, 1).reshape(b, n_lat, Q_DIM)


def _conformer_conv(u_val, u_gate, w_dw, b_dw, ln_g, ln_b):
    z = u_val * jax.nn.sigmoid(u_gate)
    z = lax.conv_general_dilated(z, w_dw[:, None, :], window_strides=(1,),
                                 padding=[(CONV_PAD, CONV_PAD)],
                                 dimension_numbers=('NWC', 'WIO', 'NWC'),
                                 feature_group_count=CONV_DIM) + b_dw
    return jax.nn.silu(_layer_norm(z, ln_g, ln_b))


def _fourier_mix(u):
    b, n, _ = u.shape
    ug = u.reshape(b, n, FOURIER_GROUPS, FOURIER_GROUP_DIM).astype(jnp.float32)
    f = jnp.fft.fft2(ug, axes=(1, 3), norm='ortho').real
    return f.reshape(b, n, FOURIER_DIM).astype(u.dtype)


def _token_mixing(h, lw, attend):
    b, n, _ = h.shape
    proj = h @ lw['w_in']
    u_val, u_gate, q, k, v, u_four, gates = jnp.split(proj, IN_SPLITS, axis=-1)
    q = q.reshape(b, n, N_HEADS, HEAD_DIM)
    k = k.reshape(b, n, N_KV_HEADS, HEAD_DIM)
    v = v.reshape(b, n, N_KV_HEADS, HEAD_DIM)
    branch_a = _conformer_conv(u_val, u_gate, lw['conv_w'], lw['conv_b'],
                               lw['conv_ln_g'], lw['conv_ln_b']) @ lw['w_conv_out']
    branch_b = attend(q, k, v) @ lw['w_attn_out']
    branch_c = _fourier_mix(u_four) @ lw['w_fourier_out']
    g = jax.nn.sigmoid(gates.astype(jnp.float32)).astype(h.dtype)
    g_a, g_b, g_c = jnp.split(g, 3, axis=-1)
    merged = g_a * branch_a + g_b * branch_b + g_c * branch_c
    return merged @ lw['w_out'], (k, v)


def _swiglu(x, w_gate, w_up, w_down):
    return (jax.nn.silu(x @ w_gate) * (x @ w_up)) @ w_down


def _routed_experts(xt, top_idx, top_w, w_gate, w_up, w_down):
    n = xt.shape[0]
    nk = n * TOP_K
    n_blocks = -(-nk // EXPERT_BLOCK) + N_EXPERTS
    n_rows = n_blocks * EXPERT_BLOCK
    flat_e = top_idx.reshape(nk)
    order = jnp.argsort(flat_e)
    sorted_e = flat_e[order]
    counts = jnp.bincount(flat_e, length=N_EXPERTS)
    padded = (counts + EXPERT_BLOCK - 1) // EXPERT_BLOCK * EXPERT_BLOCK
    group_start = jnp.cumsum(counts) - counts
    padded_end = jnp.cumsum(padded)
    padded_start = padded_end - padded
    dest = padded_start[sorted_e] + (jnp.arange(nk) - group_start[sorted_e])
    src_tok = jnp.full((n_rows,), n, dtype=jnp.int32).at[dest].set((order // TOP_K).astype(jnp.int32))
    row_w = jnp.zeros((n_rows,), xt.dtype).at[dest].set(top_w.reshape(nk)[order])
    x_pad = jnp.concatenate([xt, jnp.zeros((1, xt.shape[1]), xt.dtype)], axis=0)
    xs = x_pad[src_tok].reshape(n_blocks, EXPERT_BLOCK, xt.shape[1])
    block_e = jnp.minimum(jnp.searchsorted(padded_end, jnp.arange(n_blocks) * EXPERT_BLOCK, side='right'),
                          N_EXPERTS - 1)

    def expert_block(args):
        xb, e, wb = args
        return _swiglu(xb, w_gate[e], w_up[e], w_down[e]) * wb[:, None]

    ys = lax.map(expert_block, (xs, block_e, row_w.reshape(n_blocks, EXPERT_BLOCK)))
    return jax.ops.segment_sum(ys.reshape(n_rows, -1), src_tok, num_segments=n + 1)[:n]


def _moe(h, lw):
    b, n, d = h.shape
    xt = h.reshape(b * n, d)
    scores = jax.nn.sigmoid((xt @ lw['router_w']).astype(jnp.float32))
    _, top_idx = lax.top_k(scores + lw['router_bias'].astype(jnp.float32), TOP_K)
    top_s = jnp.take_along_axis(scores, top_idx, axis=-1)
    top_w = (ROUTED_SCALE * top_s / jnp.sum(top_s, axis=-1, keepdims=True)).astype(h.dtype)
    routed = _routed_experts(xt, top_idx, top_w, lw['exp_w_gate'], lw['exp_w_up'], lw['exp_w_down'])
    shared = _swiglu(xt, lw['shared_w_gate'], lw['shared_w_up'], lw['shared_w_down'])
    return (routed + shared).reshape(b, n, d)


def _layer(x, mod, lw, attend):
    shift1, scale1, gate1, shift2, scale2, gate2 = jnp.split(mod[:, None, :], 6, axis=-1)
    h = _layer_norm(x) * (1 + scale1) + shift1
    mix, kv = _token_mixing(h, lw, attend)
    x = _layer_norm(DEEPNORM_ALPHA * x + gate1 * mix, lw['ln1_g'], lw['ln1_b'])
    h = _layer_norm(x) * (1 + scale2) + shift2
    x = _layer_norm(DEEPNORM_ALPHA * x + gate2 * _moe(h, lw), lw['ln2_g'], lw['ln2_b'])
    return x, kv


def setup_inputs(seed: int = 0) -> dict:
    key = jax.random.key(seed)
    ks = jax.random.split(key, 32)

    def nrm(k, shape, scale):
        return jax.random.normal(k, shape, jnp.float32) * scale

    d = D_MODEL
    return {
        "x_prompt": nrm(ks[0], (BATCH, SEQ, d), 1.0),
        "x_sample": nrm(ks[1], (DEC_BATCH, DEC_SEQ, d), 1.0),
        "cache_k": nrm(ks[2], (DEC_BATCH, DEPTH, PAST_LEN, N_KV_HEADS, HEAD_DIM), 1.0),
        "cache_v": nrm(ks[3], (DEC_BATCH, DEPTH, PAST_LEN, N_KV_HEADS, HEAD_DIM), 1.0),
        "c": nrm(ks[4], (DEC_BATCH, d), 1.0),
        "c_ctx": nrm(ks[5], (d,), 1.0),
        "w_ada": nrm(ks[6], (DEPTH, d, 6 * d), d ** -0.5),
        "b_ada": nrm(ks[7], (DEPTH, 6 * d), 0.01),
        "w_in": nrm(ks[8], (DEPTH, d, IN_COLS), d ** -0.5),
        "conv_w": nrm(ks[9], (DEPTH, CONV_WIDTH, CONV_DIM), CONV_WIDTH ** -0.5),
        "conv_b": nrm(ks[10], (DEPTH, CONV_DIM), 0.01),
        "conv_ln_g": 1.0 + nrm(ks[11], (DEPTH, CONV_DIM), 0.01),
        "conv_ln_b": nrm(ks[12], (DEPTH, CONV_DIM), 0.01),
        "w_conv_out": nrm(ks[13], (DEPTH, CONV_DIM, d), CONV_DIM ** -0.5),
        "attn_sink": nrm(ks[14], (DEPTH, N_HEADS), 0.5),
        "w_attn_out": nrm(ks[15], (DEPTH, Q_DIM, d), Q_DIM ** -0.5),
        "w_fourier_out": nrm(ks[16], (DEPTH, FOURIER_DIM, d), FOURIER_DIM ** -0.5),
        "w_out": nrm(ks[17], (DEPTH, d, d), DEEPNORM_BETA * d ** -0.5),
        "post_ln_g": 1.0 + nrm(ks[18], (DEPTH, 2, d), 0.01),
        "post_ln_b": nrm(ks[19], (DEPTH, 2, d), 0.01),
        "router_w": nrm(ks[20], (DEPTH, d, N_EXPERTS), d ** -0.5),
        "router_bias": nrm(ks[21], (DEPTH, N_EXPERTS), 0.01),
        "exp_w_gate": nrm(ks[22], (DEPTH, N_EXPERTS, d, EXPERT_DIM), d ** -0.5),
        "exp_w_up": nrm(ks[23], (DEPTH, N_EXPERTS, d, EXPERT_DIM), d ** -0.5),
        "exp_w_down": nrm(ks[24], (DEPTH, N_EXPERTS, EXPERT_DIM, d), DEEPNORM_BETA * EXPERT_DIM ** -0.5),
        "shared_w_gate": nrm(ks[25], (DEPTH, d, SHARED_DIM), d ** -0.5),
        "shared_w_up": nrm(ks[26], (DEPTH, d, SHARED_DIM), d ** -0.5),
        "shared_w_down": nrm(ks[27], (DEPTH, SHARED_DIM, d), DEEPNORM_BETA * SHARED_DIM ** -0.5),
    }


def reference(x_prompt, x_sample, cache_k, cache_v, c, c_ctx, w_ada, b_ada, w_in, conv_w, conv_b,
              conv_ln_g, conv_ln_b, w_conv_out, attn_sink, w_attn_out, w_fourier_out, w_out,
              post_ln_g, post_ln_b, router_w, router_bias, exp_w_gate, exp_w_up, exp_w_down,
              shared_w_gate, shared_w_up, shared_w_down):
    n_lat = x_sample.shape[1]
    rows = n_lat // GRID_W
    row_pos = jnp.repeat(jnp.arange(rows), GRID_W)
    col_pos = jnp.tile(jnp.arange(GRID_W), rows)

    y_ctx = x_prompt
    y_lat = x_sample
    new_k, new_v = [], []
    for l in range(DEPTH):
        lw = dict(w_in=w_in[l], conv_w=conv_w[l], conv_b=conv_b[l], conv_ln_g=conv_ln_g[l],
                  conv_ln_b=conv_ln_b[l], w_conv_out=w_conv_out[l], w_attn_out=w_attn_out[l],
                  w_fourier_out=w_fourier_out[l], w_out=w_out[l],
                  ln1_g=post_ln_g[l, 0], ln1_b=post_ln_b[l, 0], ln2_g=post_ln_g[l, 1], ln2_b=post_ln_b[l, 1],
                  router_w=router_w[l], router_bias=router_bias[l], exp_w_gate=exp_w_gate[l],
                  exp_w_up=exp_w_up[l], exp_w_down=exp_w_down[l], shared_w_gate=shared_w_gate[l],
                  shared_w_up=shared_w_up[l], shared_w_down=shared_w_down[l])
        mod_ctx = jax.nn.silu(c_ctx)[None, :] @ w_ada[l] + b_ada[l]
        ctx_attend = functools.partial(_context_attention, sink=attn_sink[l])
        y_ctx, (k_l, v_l) = _layer(y_ctx, mod_ctx, lw, ctx_attend)
        new_k.append(k_l)
        new_v.append(v_l)
        mod_lat = jax.nn.silu(c) @ w_ada[l] + b_ada[l]
        lat_attend = functools.partial(_latent_attention, sink=attn_sink[l], k_ctx=cache_k[:, l],
                                       v_ctx=cache_v[:, l], row_pos=row_pos, col_pos=col_pos)
        y_lat, _ = _layer(y_lat, mod_lat, lw, lat_attend)

    return (y_ctx, y_lat, jnp.stack(new_k, axis=1), jnp.stack(new_v, axis=1))
```

```python
import functools

import jax
import jax.numpy as jnp
from jax import lax
from jax.experimental import pallas as pl
from jax.experimental.pallas import tpu as pltpu

F32 = jnp.float32
BF16 = jnp.bfloat16
I32 = jnp.int32

D_MODEL = 1024
DEPTH = 2
GRID_W = 64
CONV_DIM = 256
CONV_WIDTH = 31
CONV_PAD = CONV_WIDTH // 2
CONV_HALO = 16
N_HEADS = 8
N_KV_HEADS = 2
HEAD_DIM = 64
Q_PER_KV = N_HEADS // N_KV_HEADS
Q_DIM = N_HEADS * HEAD_DIM
KV_DIM = N_KV_HEADS * HEAD_DIM
WINDOW = 128
ATTN_BLOCK = 128
ATTN_SCALE = HEAD_DIM ** -0.5
ROPE_AXIS_DIM = HEAD_DIM // 2
ROPE_HALF = ROPE_AXIS_DIM // 2
ROPE_BASE = 10000.0
NEG_INF = -1e30
FOURIER_GROUPS = 4
FOURIER_GROUP_DIM = 64
FOURIER_DIM = FOURIER_GROUPS * FOURIER_GROUP_DIM
PRE_COLS = 2 * CONV_DIM + Q_DIM + 2 * KV_DIM + FOURIER_DIM
N_EXPERTS = 64
TOP_K = 6
EXPERT_DIM = 256
ROUTED_SCALE = 2.5
LN_EPS = 1e-6
DEEPNORM_ALPHA = (2 * DEPTH) ** 0.25

LANES = 128
MOD_ROWS = 16
TOKEN_TILE = 256
EXPERT_ROWS = 256
SLOTS = 8
VMEM_LIMIT = 56 * 1024 * 1024


def _layer_norm(x):
    mu = jnp.mean(x, axis=-1, keepdims=True)
    xc = x - mu
    var = jnp.mean(xc * xc, axis=-1, keepdims=True)
    return xc * lax.rsqrt(var + LN_EPS)


def _silu(x):
    return x * jax.nn.sigmoid(x)


def _dot(a, b):
    return jnp.dot(a, b, preferred_element_type=F32)


def _full(shape):
    return pl.BlockSpec(shape, lambda *_: (0,) * len(shape))


def _mod_kernel(c_ref, w_ref, b_ref, o_ref):
    o_ref[0] = _dot(_silu(c_ref[...]), w_ref[0]) + b_ref[0]


def _modulation(cond, w_ada, b_ada):
    depth, d, cols = w_ada.shape
    cb = cols // 4
    return pl.pallas_call(
        _mod_kernel,
        out_shape=jax.ShapeDtypeStruct((depth, MOD_ROWS, cols), F32),
        grid=(depth, cols // cb),
        in_specs=[pl.BlockSpec((MOD_ROWS, d), lambda l, j: (0, 0)),
                  pl.BlockSpec((1, d, cb), lambda l, j: (l, 0, j)),
                  pl.BlockSpec((1, 1, cb), lambda l, j: (l, 0, j))],
        out_specs=pl.BlockSpec((1, MOD_ROWS, cb), lambda l, j: (l, 0, j)),
        compiler_params=pltpu.CompilerParams(vmem_limit_bytes=VMEM_LIMIT),
        name="modulation",
    )(cond, w_ada, b_ada.reshape(depth, 1, cols))


def _premix_kernel(*refs, rope):
    if rope:
        (x_ref, mod_ref, w_ref, cc_ref, sc_ref, cos_ref, sin_ref,
         z_ref, q_ref, k_ref, v_ref, uc_ref, us_ref) = refs
    else:
        (x_ref, mod_ref, w_ref, cc_ref, sc_ref,
         z_ref, q_ref, k_ref, v_ref, uc_ref, us_ref) = refs
    d = D_MODEL
    mod = mod_ref[0]
    h = _layer_norm(x_ref[...]) * (1.0 + mod[:, d:2 * d]) + mod[:, 0:d]
    p = _dot(h.astype(BF16), w_ref[...])
    c0 = CONV_DIM
    z_ref[...] = (p[:, 0:c0] * jax.nn.sigmoid(p[:, c0:2 * c0])).astype(z_ref.dtype)
    q0 = 2 * c0
    k0 = q0 + Q_DIM
    v0 = k0 + KV_DIM
    f0 = v0 + KV_DIM
    q = p[:, q0:k0] * ATTN_SCALE
    k = p[:, k0:v0]
    if rope:
        cos = cos_ref[...]
        sin = sin_ref[...]
        first = (lax.broadcasted_iota(I32, cos.shape, 1) % ROPE_AXIS_DIM) < ROPE_HALF

        def rot(t):
            sw = jnp.where(first, pltpu.roll(t, LANES - ROPE_HALF, 1), pltpu.roll(t, ROPE_HALF, 1))
            return t * cos + sw * sin

        q = jnp.concatenate([rot(q[:, j * LANES:(j + 1) * LANES]) for j in range(Q_DIM // LANES)], axis=1)
        k = rot(k)
    q_ref[...] = q.astype(q_ref.dtype)
    k_ref[...] = k.astype(k_ref.dtype)
    v_ref[...] = p[:, v0:f0].astype(v_ref.dtype)
    uf = p[:, f0:f0 + FOURIER_DIM].astype(BF16)
    uc_ref[...] = _dot(uf, cc_ref[...]).astype(uc_ref.dtype)
    us_ref[...] = _dot(uf, sc_ref[...]).astype(us_ref.dtype)


def _premix(x2, mod_l, w_pre, cc, sc, rope_tabs, *, seq, mod_row, kv_dtype):
    n, d = x2.shape
    tm = min(TOKEN_TILE, seq)
    tpb = seq // tm
    nb = n // seq
    rope = rope_tabs is not None
    if mod_row is None:
        mod_map = lambda i: (i // tpb, 0, 0)
    else:
        mod_map = lambda i: (mod_row, 0, 0)
    tok = lambda w: pl.BlockSpec((tm, w), lambda i: (i, 0))
    in_specs = [tok(d), pl.BlockSpec((1, 1, 6 * d), mod_map), _full(w_pre.shape), _full(cc.shape), _full(sc.shape)]
    args = [x2, mod_l, w_pre, cc, sc]
    if rope:
        in_specs += [pl.BlockSpec((tm, LANES), lambda i: (i % tpb, 0))] * 2
        args += list(rope_tabs)
    four_spec = pl.BlockSpec((tm, FOURIER_DIM), lambda i: (i % tpb, i // tpb))
    return pl.pallas_call(
        functools.partial(_premix_kernel, rope=rope),
        out_shape=[jax.ShapeDtypeStruct((n, CONV_DIM), BF16),
                   jax.ShapeDtypeStruct((n, Q_DIM), BF16),
                   jax.ShapeDtypeStruct((n, KV_DIM), kv_dtype),
                   jax.ShapeDtypeStruct((n, KV_DIM), kv_dtype),
                   jax.ShapeDtypeStruct((seq, nb * FOURIER_DIM), BF16),
                   jax.ShapeDtypeStruct((seq, nb * FOURIER_DIM), BF16)],
        grid=(n // tm,),
        in_specs=in_specs,
        out_specs=[tok(CONV_DIM), tok(Q_DIM), tok(KV_DIM), tok(KV_DIM), four_spec, four_spec],
        compiler_params=pltpu.CompilerParams(vmem_limit_bytes=VMEM_LIMIT),
        name="premix_lat" if rope else "premix_ctx",
    )(*args)


def _sink_softmax_pv(s, sink_col, v):
    m = jnp.maximum(jnp.max(s, axis=1, keepdims=True), sink_col)
    p = jnp.exp(s - m)
    denom = jnp.sum(p, axis=1, keepdims=True) + jnp.exp(sink_col - m)
    return _dot(p.astype(BF16), v) / denom


def _sink_column(sink_ref, kv_head, rows):
    return jnp.concatenate(
        [jnp.full((rows, 1), sink_ref[kv_head * Q_PER_KV + g], F32) for g in range(Q_PER_KV)], axis=0)


def _ctx_attn_kernel(sink_ref, q_ref, k_ref, v_ref, o_ref):
    rows = q_ref.shape[2]
    q = q_ref[0].reshape(Q_PER_KV * rows, HEAD_DIM)
    s = lax.dot_general(q, k_ref[0, 0], (((1,), (1,)), ((), ())), preferred_element_type=F32)
    o = _sink_softmax_pv(s, _sink_column(sink_ref, pl.program_id(1), rows), v_ref[0, 0])
    o_ref[0] = o.reshape(Q_PER_KV, rows, HEAD_DIM).astype(o_ref.dtype)


def _ctx_attention(sink, q4, k4, v4):
    b, _, s, dh = q4.shape
    return pl.pallas_call(
        _ctx_attn_kernel,
        out_shape=jax.ShapeDtypeStruct(q4.shape, BF16),
        grid=(b, N_KV_HEADS),
        in_specs=[pl.BlockSpec(memory_space=pltpu.SMEM),
                  pl.BlockSpec((1, Q_PER_KV, s, dh), lambda i, h: (i, h, 0, 0)),
                  pl.BlockSpec((1, 1, s, dh), lambda i, h: (i, h, 0, 0)),
                  pl.BlockSpec((1, 1, s, dh), lambda i, h: (i, h, 0, 0))],
        out_specs=pl.BlockSpec((1, Q_PER_KV, s, dh), lambda i, h: (i, h, 0, 0)),
        compiler_params=pltpu.CompilerParams(vmem_limit_bytes=VMEM_LIMIT),
        name="attn_ctx",
    )(sink, q4, k4, v4)


def _lat_attn_kernel(sink_ref, q_ref, k_ref, v_ref, kc_ref, vc_ref, bias_ref, o_ref):
    i = pl.program_id(2)
    nblk = pl.num_programs(2)
    blk = ATTN_BLOCK
    starts = [pl.multiple_of(jnp.maximum(i - 1, 0) * blk, blk),
              pl.multiple_of(i * blk, blk),
              pl.multiple_of(jnp.minimum(i + 1, nblk - 1) * blk, blk)]
    keys = jnp.concatenate([k_ref[0, 0, pl.ds(st, blk), :] for st in starts] + [kc_ref[0, 0]], axis=0)
    vals = jnp.concatenate([v_ref[0, 0, pl.ds(st, blk), :] for st in starts] + [vc_ref[0, 0]], axis=0)
    q = q_ref[0].reshape(Q_PER_KV * blk, HEAD_DIM)
    s = lax.dot_general(q, keys, (((1,), (1,)), ((), ())), preferred_element_type=F32) + bias_ref[0]
    o = _sink_softmax_pv(s, _sink_column(sink_ref, pl.program_id(1), blk), vals)
    o_ref[0] = o.reshape(Q_PER_KV, blk, HEAD_DIM).astype(o_ref.dtype)


def _window_bias(n_ctx_keys):
    blk = ATTN_BLOCK
    r = jnp.arange(blk)[:, None]
    j = jnp.arange(3 * blk)[None, :]
    band = (j >= r) & (j <= r + 2 * WINDOW)
    has_prev = j >= blk
    has_next = j < 2 * blk
    variants = [band & has_prev, band, band & has_next]
    out = []
    for m in variants:
        full = jnp.concatenate([m, jnp.ones((blk, n_ctx_keys), bool)], axis=1)
        out.append(jnp.tile(jnp.where(full, 0.0, NEG_INF).astype(F32), (Q_PER_KV, 1)))
    return jnp.stack(out)


def _lat_attention(sink, q4, k4, v4, kc4, vc4):
    b, _, l, dh = q4.shape
    nblk = l // ATTN_BLOCK
    assert nblk >= 2 and WINDOW == ATTN_BLOCK
    pc = kc4.shape[2]
    bias = _window_bias(pc)
    nkeys = 3 * ATTN_BLOCK + pc
    bias_map = lambda i, h, j: (jnp.where(j == 0, 0, jnp.where(j == nblk - 1, 2, 1)), 0, 0)
    return pl.pallas_call(
        _lat_attn_kernel,
        out_shape=jax.ShapeDtypeStruct(q4.shape, BF16),
        grid=(b, N_KV_HEADS, nblk),
        in_specs=[pl.BlockSpec(memory_space=pltpu.SMEM),
                  pl.BlockSpec((1, Q_PER_KV, ATTN_BLOCK, dh), lambda i, h, j: (i, h, j, 0)),
                  pl.BlockSpec((1, 1, l, dh), lambda i, h, j: (i, h, 0, 0)),
                  pl.BlockSpec((1, 1, l, dh), lambda i, h, j: (i, h, 0, 0)),
                  pl.BlockSpec((1, 1, pc, dh), lambda i, h, j: (i, h, 0, 0)),
                  pl.BlockSpec((1, 1, pc, dh), lambda i, h, j: (i, h, 0, 0)),
                  pl.BlockSpec((1, Q_PER_KV * ATTN_BLOCK, nkeys), bias_map)],
        out_specs=pl.BlockSpec((1, Q_PER_KV, ATTN_BLOCK, dh), lambda i, h, j: (i, h, j, 0)),
        compiler_params=pltpu.CompilerParams(vmem_limit_bytes=VMEM_LIMIT),
        name="attn_lat",
    )(sink, q4, k4, v4, kc4, vc4, bias)


def _fourier_kernel(cn_ref, sn_ref, uc_ref, us_ref, o_ref):
    o_ref[...] = (_dot(cn_ref[...], uc_ref[...]) - _dot(sn_ref[...], us_ref[...])).astype(o_ref.dtype)


def _fourier_positions(cn, sn, uc, us):
    seq = cn.shape[0]
    tf = min(512, seq)
    nb = uc.shape[1] // FOURIER_DIM
    return pl.pallas_call(
        _fourier_kernel,
        out_shape=jax.ShapeDtypeStruct(uc.shape, BF16),
        grid=(seq // tf, nb),
        in_specs=[pl.BlockSpec((tf, seq), lambda j, b: (j, 0)),
                  pl.BlockSpec((tf, seq), lambda j, b: (j, 0)),
                  pl.BlockSpec((seq, FOURIER_DIM), lambda j, b: (0, b)),
                  pl.BlockSpec((seq, FOURIER_DIM), lambda j, b: (0, b))],
        out_specs=pl.BlockSpec((tf, FOURIER_DIM), lambda j, b: (j, b)),
        compiler_params=pltpu.CompilerParams(vmem_limit_bytes=VMEM_LIMIT),
        name="fourier",
    )(cn, sn, uc, us)


def _postmix_kernel(*refs, halo, tpb):
    if halo:
        (x_ref, mod_ref, z_ref, zp_ref, zn_ref, attn_ref, four_ref, wg_ref, cw_ref, cb_ref, cg_ref, cbeta_ref,
         wco_ref, wao_ref, wfo_ref, wo_ref, g1_ref, b1_ref, rwh_ref, rwl_ref, rb_ref, sg_ref, su_ref, sd_ref,
         r0_ref, h2_ref, idx_ref, rank_ref, wts_ref, cnt_ref, zs_ref, carry_ref) = refs
    else:
        (x_ref, mod_ref, z_ref, attn_ref, four_ref, wg_ref, cw_ref, cb_ref, cg_ref, cbeta_ref,
         wco_ref, wao_ref, wfo_ref, wo_ref, g1_ref, b1_ref, rwh_ref, rwl_ref, rb_ref, sg_ref, su_ref, sd_ref,
         r0_ref, h2_ref, idx_ref, rank_ref, wts_ref, cnt_ref, zs_ref, carry_ref) = refs
    d = D_MODEL
    i = pl.program_id(0)
    tm = x_ref.shape[0]
    x = x_ref[...]
    mod = mod_ref[0]
    shift1, scale1, gate1, shift2, scale2, gate2 = [mod[:, j * d:(j + 1) * d] for j in range(6)]
    h = (_layer_norm(x) * (1.0 + scale1) + shift1).astype(BF16)
    gates = jax.nn.sigmoid(_dot(h, wg_ref[...]))

    if halo:
        tib = i % tpb
        zs_ref[0:CONV_HALO, :] = zp_ref[...].astype(F32) * jnp.where(tib != 0, 1.0, 0.0)
        zs_ref[CONV_HALO + tm:, :] = zn_ref[...].astype(F32) * jnp.where(tib != tpb - 1, 1.0, 0.0)
    else:
        zs_ref[0:CONV_HALO, :] = jnp.zeros((CONV_HALO, CONV_DIM), F32)
        zs_ref[CONV_HALO + tm:, :] = jnp.zeros((CONV_HALO, CONV_DIM), F32)
    zs_ref[CONV_HALO:CONV_HALO + tm, :] = z_ref[...].astype(F32)
    acc = jnp.broadcast_to(cb_ref[...], (tm, CONV_DIM))
    for j in range(CONV_WIDTH):
        off = CONV_HALO - CONV_PAD + j
        acc = acc + cw_ref[j:j + 1, :] * zs_ref[off:off + tm, :]
    conv = _silu(_layer_norm(acc) * cg_ref[...] + cbeta_ref[...]).astype(BF16)

    merged = (gates[:, 0:d] * _dot(conv, wco_ref[...])
              + gates[:, d:2 * d] * _dot(attn_ref[...], wao_ref[...])
              + gates[:, 2 * d:3 * d] * _dot(four_ref[...], wfo_ref[...]))
    mix = _dot(merged.astype(BF16), wo_ref[...])
    x1 = _layer_norm(DEEPNORM_ALPHA * x + gate1 * mix) * g1_ref[...] + b1_ref[...]
    h2 = _layer_norm(x1) * (1.0 + scale2) + shift2
    h2_ref[...] = h2

    h2_hi = h2.astype(BF16)
    h2_lo = (h2 - h2_hi.astype(F32)).astype(BF16)
    logits = _dot(h2_hi, rwh_ref[...]) + _dot(h2_lo, rwh_ref[...]) + _dot(h2_hi, rwl_ref[...])
    scores = jax.nn.sigmoid(logits)
    sel = scores + rb_ref[...]
    lane_e = lax.broadcasted_iota(I32, (tm, N_EXPERTS), 1)
    onehots, top_s = [], []
    for _ in range(TOP_K):
        mx = jnp.max(sel, axis=1, keepdims=True)
        idx = jnp.min(jnp.where(sel == mx, lane_e, N_EXPERTS), axis=1, keepdims=True)
        oh = lane_e == idx
        onehots.append((oh, idx))
        top_s.append(jnp.sum(jnp.where(oh, scores, 0.0), axis=1, keepdims=True))
        sel = jnp.where(oh, -jnp.inf, sel)
    total = top_s[0]
    for s_k in top_s[1:]:
        total = total + s_k

    @pl.when(i == 0)
    def _():
        carry_ref[...] = jnp.zeros_like(carry_ref)

    chosen = jnp.zeros((tm, N_EXPERTS), F32)
    for oh, _ in onehots:
        chosen = jnp.where(oh, 1.0, chosen)
    row = lax.broadcasted_iota(I32, (tm, tm), 0)
    col = lax.broadcasted_iota(I32, (tm, tm), 1)
    before = jnp.where(col < row, 1.0, 0.0).astype(BF16)
    prefix = _dot(before, chosen.astype(BF16)) + carry_ref[...]
    carry = carry_ref[...] + jnp.sum(chosen, axis=0, keepdims=True)
    carry_ref[...] = carry
    cnt_ref[...] = carry

    lane = lax.broadcasted_iota(I32, (tm, LANES), 1)
    idx_out = jnp.zeros((tm, LANES), I32)
    rank_out = jnp.zeros((tm, LANES), I32)
    wts_out = jnp.zeros((tm, LANES), F32)
    for k, (oh, idx) in enumerate(onehots):
        rank = jnp.sum(jnp.where(oh, prefix, 0.0), axis=1, keepdims=True).astype(I32)
        idx_out = jnp.where(lane == k, idx, idx_out)
        rank_out = jnp.where(lane == k, rank, rank_out)
        wts_out = jnp.where(lane == k, ROUTED_SCALE * top_s[k] / total, wts_out)
    idx_ref[...] = idx_out
    rank_ref[...] = rank_out
    wts_ref[...] = wts_out

    h2b = h2.astype(BF16)
    shared = _dot((_silu(_dot(h2b, sg_ref[...])) * _dot(h2b, su_ref[...])).astype(BF16), sd_ref[...])
    r0_ref[...] = DEEPNORM_ALPHA * x1 + gate2 * shared


def _postmix(x2, mod_l, z, attn, four, lw, *, seq, mod_row):
    n, d = x2.shape
    tm = min(TOKEN_TILE, seq)
    tpb = seq // tm
    halo = tpb > 1
    hb = tm // CONV_HALO
    last_halo = n // CONV_HALO - 1
    if mod_row is None:
        mod_map = lambda i: (i // tpb, 0, 0)
    else:
        mod_map = lambda i: (mod_row, 0, 0)
    tok = lambda w: pl.BlockSpec((tm, w), lambda i: (i, 0))
    in_specs = [tok(d), pl.BlockSpec((1, 1, 6 * d), mod_map), tok(CONV_DIM)]
    args = [x2, mod_l, z]
    if halo:
        in_specs += [pl.BlockSpec((CONV_HALO, CONV_DIM), lambda i: (jnp.maximum(i * hb - 1, 0), 0)),
                     pl.BlockSpec((CONV_HALO, CONV_DIM), lambda i: (jnp.minimum((i + 1) * hb, last_halo), 0))]
        args += [z, z]
    in_specs += [tok(Q_DIM), pl.BlockSpec((tm, FOURIER_DIM), lambda i: (i % tpb, i // tpb))]
    args += [attn, four]
    weights = [lw['w_gates'], lw['conv_w'], lw['conv_b'], lw['conv_ln_g'], lw['conv_ln_b'], lw['w_conv_out'],
               lw['w_attn_out'], lw['w_fourier_out'], lw['w_out'], lw['ln1_g'], lw['ln1_b'],
               lw['router_hi'], lw['router_lo'], lw['router_bias'],
               lw['shared_w_gate'], lw['shared_w_up'], lw['shared_w_down']]
    in_specs += [_full(w.shape) for w in weights]
    args += weights
    return pl.pallas_call(
        functools.partial(_postmix_kernel, halo=halo, tpb=tpb),
        out_shape=[jax.ShapeDtypeStruct((n, d), F32),
                   jax.ShapeDtypeStruct((n, d), F32),
                   jax.ShapeDtypeStruct((n, LANES), I32),
                   jax.ShapeDtypeStruct((n, LANES), I32),
                   jax.ShapeDtypeStruct((n, LANES), F32),
                   jax.ShapeDtypeStruct((1, N_EXPERTS), F32)],
        grid=(n // tm,),
        in_specs=in_specs,
        out_specs=[tok(d), tok(d), tok(LANES), tok(LANES), tok(LANES), _full((1, N_EXPERTS))],
        scratch_shapes=[pltpu.VMEM((tm + 2 * CONV_HALO, CONV_DIM), F32),
                        pltpu.VMEM((1, N_EXPERTS), F32)],
        compiler_params=pltpu.CompilerParams(dimension_semantics=("arbitrary",), vmem_limit_bytes=VMEM_LIMIT),
        name="postmix_lat" if halo else "postmix_ctx",
    )(*args)


def _tile_copy_wait(src_tile, dst_hbm, sem):
    pltpu.make_async_copy(src_tile, dst_hbm.at[pl.ds(0, src_tile.shape[0])], sem).wait()


def _dispatch_kernel(pstart_ref, idx_ref, rank_ref, h_ref, dest_ref, xs_ref, sem):
    tm = h_ref.shape[0]

    def body(t, carry):
        for k in range(TOP_K):
            a = t * SLOTS + k
            dst = pstart_ref[idx_ref[a]] + rank_ref[a]
            dest_ref[a] = dst
            pltpu.make_async_copy(h_ref.at[pl.ds(t, 1)], xs_ref.at[pl.ds(dst, 1)], sem).start()
        for k in range(TOP_K, SLOTS):
            dest_ref[t * SLOTS + k] = 0
        return carry

    lax.fori_loop(0, tm, body, 0)
    for _ in range(TOP_K):
        _tile_copy_wait(h_ref, xs_ref, sem)


def _dispatch(pstart, idx_flat, rank_flat, h2, n_rows):
    n, d = h2.shape
    tm = TOKEN_TILE
    flat = pl.BlockSpec((tm * SLOTS,), lambda i, *_: (i,), memory_space=pltpu.SMEM)
    return pl.pallas_call(
        _dispatch_kernel,
        out_shape=[jax.ShapeDtypeStruct((n * SLOTS,), I32),
                   jax.ShapeDtypeStruct((n_rows, d), F32)],
        grid_spec=pltpu.PrefetchScalarGridSpec(
            num_scalar_prefetch=1,
            grid=(n // tm,),
            in_specs=[flat, flat, pl.BlockSpec((tm, d), lambda i, *_: (i, 0))],
            out_specs=[flat, pl.BlockSpec(memory_space=pl.ANY)],
            scratch_shapes=[pltpu.SemaphoreType.DMA]),
        compiler_params=pltpu.CompilerParams(vmem_limit_bytes=VMEM_LIMIT),
        name="moe_dispatch",
    )(pstart, idx_flat, rank_flat, h2)


def _experts_kernel(be_ref, bi_ref, nu_ref, xs_ref, wg_ref, wu_ref, wd_ref, ys_ref):
    @pl.when(pl.program_id(0) < nu_ref[0])
    def _():
        x = xs_ref[...].astype(BF16)
        a = (_silu(_dot(x, wg_ref[0])) * _dot(x, wu_ref[0])).astype(BF16)
        ys_ref[...] = _dot(a, wd_ref[0])


def _experts(block_e, block_i, n_used, xs, w_gate, w_up, w_down):
    n_rows, d = xs.shape
    r = EXPERT_ROWS
    return pl.pallas_call(
        _experts_kernel,
        out_shape=jax.ShapeDtypeStruct((n_rows, d), F32),
        grid_spec=pltpu.PrefetchScalarGridSpec(
            num_scalar_prefetch=3,
            grid=(n_rows // r,),
            in_specs=[pl.BlockSpec((r, d), lambda j, be, bi, nu: (bi[j], 0)),
                      pl.BlockSpec((1, d, EXPERT_DIM), lambda j, be, bi, nu: (be[j], 0, 0)),
                      pl.BlockSpec((1, d, EXPERT_DIM), lambda j, be, bi, nu: (be[j], 0, 0)),
                      pl.BlockSpec((1, EXPERT_DIM, d), lambda j, be, bi, nu: (be[j], 0, 0))],
            out_specs=pl.BlockSpec((r, d), lambda j, be, bi, nu: (bi[j], 0))),
        compiler_params=pltpu.CompilerParams(dimension_semantics=("arbitrary",), vmem_limit_bytes=VMEM_LIMIT),
        name="moe_experts",
    )(block_e, block_i, n_used, xs, w_gate, w_up, w_down)


def _combine_kernel(dest_ref, r0_ref, wts_ref, mod_ref, g2_ref, b2_ref, ys_ref, o_ref, buf_ref, sem):
    tm, d = r0_ref.shape

    def body(t, carry):
        for k in range(TOP_K):
            src = dest_ref[t * SLOTS + k]
            pltpu.make_async_copy(ys_ref.at[pl.ds(src, 1)], buf_ref.at[k, pl.ds(t, 1)], sem).start()
        return carry

    lax.fori_loop(0, tm, body, 0)
    for k in range(TOP_K):
        pltpu.make_async_copy(ys_ref.at[pl.ds(0, tm)], buf_ref.at[k], sem).wait()
    wts = wts_ref[...]
    routed = wts[:, 0:1] * buf_ref[0]
    for k in range(1, TOP_K):
        routed = routed + wts[:, k:k + 1] * buf_ref[k]
    gate2 = mod_ref[0][:, 5 * d:6 * d]
    o_ref[...] = _layer_norm(r0_ref[...] + gate2 * routed) * g2_ref[...] + b2_ref[...]


def _combine(dest_flat, r0, wts, mod_l, ln2_g, ln2_b, ys, *, seq, mod_row):
    n, d = r0.shape
    tm = TOKEN_TILE
    tpb = seq // tm
    if mod_row is None:
        mod_map = lambda i: (i // tpb, 0, 0)
    else:
        mod_map = lambda i: (mod_row, 0, 0)
    return pl.pallas_call(
        _combine_kernel,
        out_shape=jax.ShapeDtypeStruct((n, d), F32),
        grid=(n // tm,),
        in_specs=[pl.BlockSpec((tm * SLOTS,), lambda i: (i,), memory_space=pltpu.SMEM),
                  pl.BlockSpec((tm, d), lambda i: (i, 0)),
                  pl.BlockSpec((tm, LANES), lambda i: (i, 0)),
                  pl.BlockSpec((1, 1, 6 * d), mod_map),
                  _full((1, d)), _full((1, d)),
                  pl.BlockSpec(memory_space=pl.ANY)],
        out_specs=pl.BlockSpec((tm, d), lambda i: (i, 0)),
        scratch_shapes=[pltpu.VMEM((TOP_K, tm, d), F32), pltpu.SemaphoreType.DMA],
        compiler_params=pltpu.CompilerParams(vmem_limit_bytes=VMEM_LIMIT),
        name="moe_combine",
    )(dest_flat, r0, wts, mod_l, ln2_g, ln2_b, ys)


def _moe_routed(h2, idx, rank, counts, lw):
    n = h2.shape[0]
    r = EXPERT_ROWS
    n_blocks = -(-(n * TOP_K) // r) + N_EXPERTS
    cnt = counts.reshape(N_EXPERTS).astype(I32)
    padded = (cnt + r - 1) // r * r
    pend = jnp.cumsum(padded)
    pstart = (pend - padded).astype(I32)
    n_used = (pend[-1] // r).astype(I32)
    blocks = jnp.arange(n_blocks, dtype=I32)
    block_i = jnp.minimum(blocks, n_used - 1)
    block_e = jnp.minimum(jnp.searchsorted(pend, block_i * r, side='right'), N_EXPERTS - 1).astype(I32)
    idx_flat = idx[:, :SLOTS].reshape(-1)
    rank_flat = rank[:, :SLOTS].reshape(-1)
    dest, xs = _dispatch(pstart, idx_flat, rank_flat, h2, n_blocks * r)
    ys = _experts(block_e, block_i, n_used.reshape(1), xs, lw['exp_w_gate'], lw['exp_w_up'], lw['exp_w_down'])
    return dest, ys


def _channel_dft():
    j = jnp.arange(FOURIER_DIM)
    same = (j[:, None] // FOURIER_GROUP_DIM) == (j[None, :] // FOURIER_GROUP_DIM)
    prod = ((j[:, None] % FOURIER_GROUP_DIM) * (j[None, :] % FOURIER_GROUP_DIM)) % FOURIER_GROUP_DIM
    ang = (2.0 * jnp.pi / FOURIER_GROUP_DIM) * prod.astype(F32)
    scale = FOURIER_GROUP_DIM ** -0.5
    return (jnp.where(same, jnp.cos(ang) * scale, 0.0).astype(BF16),
            jnp.where(same, jnp.sin(ang) * scale, 0.0).astype(BF16))


def _position_dft(n):
    j = jnp.arange(n, dtype=I32)
    ang = (2.0 * jnp.pi / n) * ((j[:, None] * j[None, :]) % n).astype(F32)
    scale = n ** -0.5
    return (jnp.cos(ang) * scale).astype(BF16), (jnp.sin(ang) * scale).astype(BF16)


def _rope_tables(n_lat):
    t = jnp.arange(n_lat)
    inv = ROPE_BASE ** (-jnp.arange(ROPE_HALF, dtype=F32) / ROPE_HALF)
    row_ang = (t // GRID_W).astype(F32)[:, None] * inv[None, :]
    col_ang = (t % GRID_W).astype(F32)[:, None] * inv[None, :]
    cos = jnp.concatenate([jnp.cos(row_ang)] * 2 + [jnp.cos(col_ang)] * 2, axis=1)
    sin = jnp.concatenate([-jnp.sin(row_ang), jnp.sin(row_ang), -jnp.sin(col_ang), jnp.sin(col_ang)], axis=1)
    reps = LANES // HEAD_DIM
    return jnp.tile(cos, (1, reps)), jnp.tile(sin, (1, reps))


def _heads_first(a2, batch, seq, heads):
    return a2.reshape(batch, seq, heads, HEAD_DIM).transpose(0, 2, 1, 3)


def _layer_group(x2, mod_l, lw, tabs, *, batch, seq, mod_row, attend):
    z, q, k, v, uc, us = _premix(x2, mod_l, lw['w_pre'], tabs['cc'], tabs['sc'], tabs['rope'],
                                 seq=seq, mod_row=mod_row, kv_dtype=F32 if tabs['rope'] is None else BF16)
    q4 = _heads_first(q, batch, seq, N_HEADS)
    k4 = _heads_first(k.astype(BF16), batch, seq, N_KV_HEADS)
    v4 = _heads_first(v.astype(BF16), batch, seq, N_KV_HEADS)
    o4 = attend(q4, k4, v4)
    attn = o4.transpose(0, 2, 1, 3).reshape(batch * seq, Q_DIM)
    four = _fourier_positions(tabs['cn'], tabs['sn'], uc, us)
    r0, h2, idx, rank, wts, counts = _postmix(x2, mod_l, z, attn, four, lw, seq=seq, mod_row=mod_row)
    dest, ys = _moe_routed(h2, idx, rank, counts, lw)
    out = _combine(dest, r0, wts, mod_l, lw['ln2_g'], lw['ln2_b'], ys, seq=seq, mod_row=mod_row)
    return out, k, v


def kernel(x_prompt, x_sample, cache_k, cache_v, c, c_ctx, w_ada, b_ada, w_in, conv_w, conv_b, conv_ln_g, conv_ln_b, w_conv_out, attn_sink, w_attn_out, w_fourier_out, w_out, post_ln_g, post_ln_b, router_w, router_bias, exp_w_gate, exp_w_up, exp_w_down, shared_w_gate, shared_w_up, shared_w_down):
    b_ctx, s_ctx, d = x_prompt.shape
    b_lat, s_lat, _ = x_sample.shape
    depth = w_in.shape[0]
    ctx_row = b_lat
    assert b_lat < MOD_ROWS and d == D_MODEL

    cond = jnp.concatenate([c, c_ctx[None, :], jnp.zeros((MOD_ROWS - b_lat - 1, d), F32)], axis=0)
    mod = _modulation(cond, w_ada, b_ada)

    cc, sc = _channel_dft()
    cn_ctx, sn_ctx = _position_dft(s_ctx)
    cn_lat, sn_lat = _position_dft(s_lat)
    tabs_ctx = dict(cc=cc, sc=sc, cn=cn_ctx, sn=sn_ctx, rope=None)
    tabs_lat = dict(cc=cc, sc=sc, cn=cn_lat, sn=sn_lat, rope=_rope_tables(s_lat))

    y_ctx = x_prompt.reshape(b_ctx * s_ctx, d)
    y_lat = x_sample.reshape(b_lat * s_lat, d)
    new_k, new_v = [], []
    for l in range(depth):
        rw = router_w[l]
        rw_hi = rw.astype(BF16)
        row = lambda a: a.reshape(1, -1)
        lw = dict(
            w_pre=w_in[l][:, :PRE_COLS].astype(BF16), w_gates=w_in[l][:, PRE_COLS:].astype(BF16),
            conv_w=conv_w[l], conv_b=row(conv_b[l]), conv_ln_g=row(conv_ln_g[l]), conv_ln_b=row(conv_ln_b[l]),
            w_conv_out=w_conv_out[l].astype(BF16), w_attn_out=w_attn_out[l].astype(BF16),
            w_fourier_out=w_fourier_out[l].astype(BF16), w_out=w_out[l].astype(BF16),
            ln1_g=row(post_ln_g[l, 0]), ln1_b=row(post_ln_b[l, 0]),
            ln2_g=row(post_ln_g[l, 1]), ln2_b=row(post_ln_b[l, 1]),
            router_hi=rw_hi, router_lo=(rw - rw_hi.astype(F32)).astype(BF16), router_bias=row(router_bias[l]),
            exp_w_gate=exp_w_gate[l].astype(BF16), exp_w_up=exp_w_up[l].astype(BF16),
            exp_w_down=exp_w_down[l].astype(BF16),
            shared_w_gate=shared_w_gate[l].astype(BF16), shared_w_up=shared_w_up[l].astype(BF16),
            shared_w_down=shared_w_down[l].astype(BF16))
        mod_l = mod[l].reshape(MOD_ROWS, 1, 6 * d)
        sink = attn_sink[l]

        y_ctx, k_l, v_l = _layer_group(y_ctx, mod_l, lw, tabs_ctx, batch=b_ctx, seq=s_ctx, mod_row=ctx_row,
                                       attend=functools.partial(_ctx_attention, sink))
        new_k.append(k_l.reshape(b_ctx, s_ctx, N_KV_HEADS, HEAD_DIM))
        new_v.append(v_l.reshape(b_ctx, s_ctx, N_KV_HEADS, HEAD_DIM))

        kc4 = cache_k[:, l].astype(BF16).transpose(0, 2, 1, 3)
        vc4 = cache_v[:, l].astype(BF16).transpose(0, 2, 1, 3)
        y_lat, _, _ = _layer_group(y_lat, mod_l, lw, tabs_lat, batch=b_lat, seq=s_lat, mod_row=None,
                                   attend=lambda q4, k4, v4: _lat_attention(sink, q4, k4, v4, kc4, vc4))

    return (y_ctx.reshape(b_ctx, s_ctx, d), y_lat.reshape(b_lat, s_lat, d),
            jnp.stack(new_k, axis=1), jnp.stack(new_v, axis=1))
```

```python
import functools

import jax
import jax.numpy as jnp
from jax import lax
from jax.experimental import pallas as pl
from jax.experimental.pallas import tpu as pltpu

F32 = jnp.float32
BF16 = jnp.bfloat16
I32 = jnp.int32
U32 = jnp.uint32

D_MODEL = 1024
DEPTH = 2
GRID_W = 64
CONV_DIM = 256
CONV_WIDTH = 31
CONV_PAD = CONV_WIDTH // 2
CONV_HALO = 16
N_HEADS = 8
N_KV_HEADS = 2
HEAD_DIM = 64
Q_PER_KV = N_HEADS // N_KV_HEADS
Q_DIM = N_HEADS * HEAD_DIM
KV_DIM = N_KV_HEADS * HEAD_DIM
WINDOW = 128
ATTN_BLOCK = 128
ATTN_SCALE = HEAD_DIM ** -0.5
ROPE_AXIS_DIM = HEAD_DIM // 2
ROPE_HALF = ROPE_AXIS_DIM // 2
ROPE_BASE = 10000.0
NEG_INF = -1e30
FOURIER_GROUPS = 4
FOURIER_GROUP_DIM = 64
FOURIER_DIM = FOURIER_GROUPS * FOURIER_GROUP_DIM
PRE_COLS = 2 * CONV_DIM + Q_DIM + 2 * KV_DIM + FOURIER_DIM
N_EXPERTS = 64
TOP_K = 6
EXPERT_DIM = 256
ROUTED_SCALE = 2.5
LN_EPS = 1e-6
DEEPNORM_ALPHA = (2 * DEPTH) ** 0.25

LANES = 128
SUBLANES = 8
MOD_ROWS = 16
TOKEN_TILE = 256
EXPERT_ROWS = 256
PIECE = SUBLANES
SORT_ROWS = -(-(TOKEN_TILE * TOP_K + N_EXPERTS * (PIECE - 1)) // 256) * 256
PACKED = D_MODEL // 2
VMEM_LIMIT = 56 * 1024 * 1024


def _layer_norm(x):
    mu = jnp.mean(x, axis=-1, keepdims=True)
    xc = x - mu
    var = jnp.mean(xc * xc, axis=-1, keepdims=True)
    return xc * lax.rsqrt(var + LN_EPS)


def _silu(x):
    return x * jax.nn.sigmoid(x)


def _dot(a, b):
    return jnp.dot(a, b, preferred_element_type=F32)


def _full(shape):
    return pl.BlockSpec(shape, lambda *_: (0,) * len(shape), pipeline_mode=pl.Buffered(1))


def _pack_rows(y, rounded=False):
    bits = lax.bitcast_convert_type(y if rounded else y.astype(BF16).astype(F32), U32)
    return (bits[:, :PACKED] >> 16) | bits[:, PACKED:]


def _unpack_rows(u):
    lo = lax.bitcast_convert_type(u << 16, F32)
    hi = lax.bitcast_convert_type(u & jnp.uint32(0xFFFF0000), F32)
    return jnp.concatenate([lo, hi], axis=1).astype(BF16)


class _Stream:
    def __init__(self, b_ctx, s_ctx, b_lat, s_lat):
        self.tm = TOKEN_TILE
        assert s_ctx % self.tm == 0 and s_lat % self.tm == 0
        self.b_ctx, self.s_ctx, self.b_lat, self.s_lat = b_ctx, s_ctx, b_lat, s_lat
        self.n_ctx = b_ctx * s_ctx
        self.n = self.n_ctx + b_lat * s_lat
        self.tpb_c = s_ctx // self.tm
        self.tpb_l = s_lat // self.tm
        self.nct = self.n_ctx // self.tm
        self.tiles = self.n // self.tm
        self.ctx_row = b_lat

    def mod_row(self, i):
        return jnp.where(i >= self.nct, (i - self.nct) // self.tpb_l, self.ctx_row)

    def tile_in_seq(self, i):
        return jnp.where(i >= self.nct, (i - self.nct) % self.tpb_l, i % self.tpb_c)

    def tiles_per_seq(self, i):
        return jnp.where(i >= self.nct, self.tpb_l, self.tpb_c)

    def rope_block(self, i):
        return jnp.where(i >= self.nct, self.tpb_c + (i - self.nct) % self.tpb_l, i % self.tpb_c)


def _mod_kernel(c_ref, w_ref, b_ref, o_ref):
    o_ref[0] = _dot(_silu(c_ref[...]), w_ref[0]) + b_ref[0]


def _modulation(cond, w_ada, b_ada):
    depth, d, cols = w_ada.shape
    cb = cols // 4
    return pl.pallas_call(
        _mod_kernel,
        out_shape=jax.ShapeDtypeStruct((depth, MOD_ROWS, cols), F32),
        grid=(depth, cols // cb),
        in_specs=[pl.BlockSpec((MOD_ROWS, d), lambda l, j: (0, 0)),
                  pl.BlockSpec((1, d, cb), lambda l, j: (l, 0, j)),
                  pl.BlockSpec((1, 1, cb), lambda l, j: (l, 0, j))],
        out_specs=pl.BlockSpec((1, MOD_ROWS, cb), lambda l, j: (l, 0, j)),
        compiler_params=pltpu.CompilerParams(vmem_limit_bytes=VMEM_LIMIT),
        name="modulation",
    )(cond, w_ada, b_ada.reshape(depth, 1, cols))


def _premix_kernel(x_ref, mod_ref, w_ref, cc_ref, sc_ref, cos_ref, sin_ref,
                   z_ref, q_ref, k_ref, v_ref, uc_ref, us_ref):
    d = D_MODEL
    mod = mod_ref[0]
    h = _layer_norm(x_ref[...]) * (1.0 + mod[:, d:2 * d]) + mod[:, 0:d]
    p = _dot(h.astype(BF16), w_ref[...])
    c0 = CONV_DIM
    z_ref[...] = (p[:, 0:c0] * jax.nn.sigmoid(p[:, c0:2 * c0])).astype(z_ref.dtype)
    q0 = 2 * c0
    k0 = q0 + Q_DIM
    v0 = k0 + KV_DIM
    f0 = v0 + KV_DIM
    cos = cos_ref[...]
    sin = sin_ref[...]
    first = (lax.broadcasted_iota(I32, cos.shape, 1) % ROPE_AXIS_DIM) < ROPE_HALF

    def rot(t):
        sw = jnp.where(first, pltpu.roll(t, LANES - ROPE_HALF, 1), pltpu.roll(t, ROPE_HALF, 1))
        return t * cos + sw * sin

    q = p[:, q0:k0] * ATTN_SCALE
    q = jnp.concatenate([rot(q[:, j * LANES:(j + 1) * LANES]) for j in range(Q_DIM // LANES)], axis=1)
    q_ref[...] = q.astype(q_ref.dtype)
    k_ref[...] = rot(p[:, k0:v0])
    v_ref[...] = p[:, v0:f0]
    uf = p[:, f0:f0 + FOURIER_DIM].astype(BF16)
    uc_ref[...] = _dot(uf, cc_ref[...]).astype(uc_ref.dtype)
    us_ref[...] = _dot(uf, sc_ref[...]).astype(us_ref.dtype)


def _premix(st, x2, mod_l, w_pre, cc, sc, rope_cos, rope_sin):
    d = x2.shape[1]
    tm = st.tm
    tok = lambda w: pl.BlockSpec((tm, w), lambda i: (i, 0))
    rope = pl.BlockSpec((tm, LANES), lambda i: (st.rope_block(i), 0))
    return pl.pallas_call(
        _premix_kernel,
        out_shape=[jax.ShapeDtypeStruct((st.n, CONV_DIM), BF16),
                   jax.ShapeDtypeStruct((st.n, Q_DIM), BF16),
                   jax.ShapeDtypeStruct((st.n, KV_DIM), F32),
                   jax.ShapeDtypeStruct((st.n, KV_DIM), F32),
                   jax.ShapeDtypeStruct((st.n, FOURIER_DIM), BF16),
                   jax.ShapeDtypeStruct((st.n, FOURIER_DIM), BF16)],
        grid=(st.tiles,),
        in_specs=[tok(d), pl.BlockSpec((1, 1, 6 * d), lambda i: (st.mod_row(i), 0, 0)),
                  _full(w_pre.shape), _full(cc.shape), _full(sc.shape), rope, rope],
        out_specs=[tok(CONV_DIM), tok(Q_DIM), tok(KV_DIM), tok(KV_DIM), tok(FOURIER_DIM), tok(FOURIER_DIM)],
        compiler_params=pltpu.CompilerParams(vmem_limit_bytes=VMEM_LIMIT),
        name="premix",
    )(x2, mod_l, w_pre, cc, sc, rope_cos, rope_sin)


def _sink_softmax_pv(s, sink_col, v):
    m = jnp.maximum(jnp.max(s, axis=1, keepdims=True), sink_col)
    p = jnp.exp(s - m)
    denom = jnp.sum(p, axis=1, keepdims=True) + jnp.exp(sink_col - m)
    return _dot(p.astype(BF16), v) / denom


def _sink_column(sink_ref, kv_head, rows):
    return jnp.concatenate(
        [jnp.full((rows, 1), sink_ref[kv_head * Q_PER_KV + g], F32) for g in range(Q_PER_KV)], axis=0)


def _ctx_attn_kernel(sink_ref, q_ref, k_ref, v_ref, o_ref):
    rows = q_ref.shape[2]
    q = q_ref[0].reshape(Q_PER_KV * rows, HEAD_DIM)
    s = lax.dot_general(q, k_ref[0, 0], (((1,), (1,)), ((), ())), preferred_element_type=F32)
    o = _sink_softmax_pv(s, _sink_column(sink_ref, pl.program_id(1), rows), v_ref[0, 0])
    o_ref[0] = o.reshape(Q_PER_KV, rows, HEAD_DIM).astype(o_ref.dtype)


def _ctx_attention(sink, q4, k4, v4):
    b, _, s, dh = q4.shape
    return pl.pallas_call(
        _ctx_attn_kernel,
        out_shape=jax.ShapeDtypeStruct(q4.shape, BF16),
        grid=(b, N_KV_HEADS),
        in_specs=[pl.BlockSpec(memory_space=pltpu.SMEM),
                  pl.BlockSpec((1, Q_PER_KV, s, dh), lambda i, h: (i, h, 0, 0)),
                  pl.BlockSpec((1, 1, s, dh), lambda i, h: (i, h, 0, 0)),
                  pl.BlockSpec((1, 1, s, dh), lambda i, h: (i, h, 0, 0))],
        out_specs=pl.BlockSpec((1, Q_PER_KV, s, dh), lambda i, h: (i, h, 0, 0)),
        compiler_params=pltpu.CompilerParams(vmem_limit_bytes=VMEM_LIMIT),
        name="attn_ctx",
    )(sink, q4, k4, v4)


def _lat_attn_kernel(sink_ref, q_ref, k_ref, v_ref, kc_ref, vc_ref, bias_ref, o_ref):
    i = pl.program_id(2)
    nblk = pl.num_programs(2)
    blk = ATTN_BLOCK
    starts = [pl.multiple_of(jnp.maximum(i - 1, 0) * blk, blk),
              pl.multiple_of(i * blk, blk),
              pl.multiple_of(jnp.minimum(i + 1, nblk - 1) * blk, blk)]
    keys = jnp.concatenate([k_ref[0, 0, pl.ds(st, blk), :] for st in starts] + [kc_ref[0, 0]], axis=0)
    vals = jnp.concatenate([v_ref[0, 0, pl.ds(st, blk), :] for st in starts] + [vc_ref[0, 0]], axis=0)
    q = q_ref[0].reshape(Q_PER_KV * blk, HEAD_DIM)
    s = lax.dot_general(q, keys, (((1,), (1,)), ((), ())), preferred_element_type=F32) + bias_ref[0]
    o = _sink_softmax_pv(s, _sink_column(sink_ref, pl.program_id(1), blk), vals)
    o_ref[0] = o.reshape(Q_PER_KV, blk, HEAD_DIM).astype(o_ref.dtype)


def _window_bias(n_ctx_keys):
    blk = ATTN_BLOCK
    r = jnp.arange(blk)[:, None]
    j = jnp.arange(3 * blk)[None, :]
    band = (j >= r) & (j <= r + 2 * WINDOW)
    has_prev = j >= blk
    has_next = j < 2 * blk
    variants = [band & has_prev, band, band & has_next]
    out = []
    for m in variants:
        full = jnp.concatenate([m, jnp.ones((blk, n_ctx_keys), bool)], axis=1)
        out.append(jnp.tile(jnp.where(full, 0.0, NEG_INF).astype(F32), (Q_PER_KV, 1)))
    return jnp.stack(out)


def _lat_attention(sink, q4, k4, v4, kc4, vc4):
    b, _, l, dh = q4.shape
    nblk = l // ATTN_BLOCK
    assert nblk >= 2 and WINDOW == ATTN_BLOCK
    pc = kc4.shape[2]
    bias = _window_bias(pc)
    nkeys = 3 * ATTN_BLOCK + pc
    bias_map = lambda i, h, j: (jnp.where(j == 0, 0, jnp.where(j == nblk - 1, 2, 1)), 0, 0)
    return pl.pallas_call(
        _lat_attn_kernel,
        out_shape=jax.ShapeDtypeStruct(q4.shape, BF16),
        grid=(b, N_KV_HEADS, nblk),
        in_specs=[pl.BlockSpec(memory_space=pltpu.SMEM),
                  pl.BlockSpec((1, Q_PER_KV, ATTN_BLOCK, dh), lambda i, h, j: (i, h, j, 0)),
                  pl.BlockSpec((1, 1, l, dh), lambda i, h, j: (i, h, 0, 0)),
                  pl.BlockSpec((1, 1, l, dh), lambda i, h, j: (i, h, 0, 0)),
                  pl.BlockSpec((1, 1, pc, dh), lambda i, h, j: (i, h, 0, 0)),
                  pl.BlockSpec((1, 1, pc, dh), lambda i, h, j: (i, h, 0, 0)),
                  pl.BlockSpec((1, Q_PER_KV * ATTN_BLOCK, nkeys), bias_map)],
        out_specs=pl.BlockSpec((1, Q_PER_KV, ATTN_BLOCK, dh), lambda i, h, j: (i, h, j, 0)),
        compiler_params=pltpu.CompilerParams(vmem_limit_bytes=VMEM_LIMIT),
        name="attn_lat",
    )(sink, q4, k4, v4, kc4, vc4, bias)


def _fourier_kernel(cn_ref, sn_ref, uc_ref, us_ref, o_ref):
    o_ref[...] = (_dot(cn_ref[...], uc_ref[...]) - _dot(sn_ref[...], us_ref[...])).astype(o_ref.dtype)


def _fourier_positions(cn, sn, uc, us):
    seq = cn.shape[0]
    tf = min(512, seq)
    nb = uc.shape[0] // seq
    jt = seq // tf
    return pl.pallas_call(
        _fourier_kernel,
        out_shape=jax.ShapeDtypeStruct(uc.shape, BF16),
        grid=(jt, nb),
        in_specs=[pl.BlockSpec((tf, seq), lambda j, b: (j, 0)),
                  pl.BlockSpec((tf, seq), lambda j, b: (j, 0)),
                  pl.BlockSpec((seq, FOURIER_DIM), lambda j, b: (b, 0)),
                  pl.BlockSpec((seq, FOURIER_DIM), lambda j, b: (b, 0))],
        out_specs=pl.BlockSpec((tf, FOURIER_DIM), lambda j, b: (b * jt + j, 0)),
        compiler_params=pltpu.CompilerParams(vmem_limit_bytes=VMEM_LIMIT),
        name="fourier",
    )(cn, sn, uc, us)


def _postmix_kernel(x_ref, mod_ref, z_ref, zp_ref, zn_ref, attn_ref, four_ref, wg_ref, cw_ref, cb_ref, cg_ref,
                    cbeta_ref, wco_ref, wao_ref, wfo_ref, wo_ref, g1_ref, b1_ref, rwh_ref, rwl_ref, rb_ref,
                    sg_ref, su_ref, sd_ref,
                    r0_ref, xs_ref, pos_ref, wts_ref, tab_ref, cnt_ref, zs_ref, carry_ref, *, st):
    d = D_MODEL
    i = pl.program_id(0)
    tm = x_ref.shape[0]
    x = x_ref[...]
    mod = mod_ref[0]
    shift1, scale1, gate1, shift2, scale2, gate2 = [mod[:, j * d:(j + 1) * d] for j in range(6)]
    h = (_layer_norm(x) * (1.0 + scale1) + shift1).astype(BF16)
    gates = jax.nn.sigmoid(_dot(h, wg_ref[...]))

    tib = st.tile_in_seq(i)
    zs_ref[0:CONV_HALO, :] = zp_ref[...].astype(F32) * jnp.where(tib != 0, 1.0, 0.0)
    zs_ref[CONV_HALO + tm:, :] = zn_ref[...].astype(F32) * jnp.where(tib != st.tiles_per_seq(i) - 1, 1.0, 0.0)
    zs_ref[CONV_HALO:CONV_HALO + tm, :] = z_ref[...].astype(F32)
    acc = jnp.broadcast_to(cb_ref[...], (tm, CONV_DIM))
    for j in range(CONV_WIDTH):
        off = CONV_HALO - CONV_PAD + j
        acc = acc + cw_ref[j:j + 1, :] * zs_ref[off:off + tm, :]
    conv = _silu(_layer_norm(acc) * cg_ref[...] + cbeta_ref[...]).astype(BF16)

    merged = (gates[:, 0:d] * _dot(conv, wco_ref[...])
              + gates[:, d:2 * d] * _dot(attn_ref[...], wao_ref[...])
              + gates[:, 2 * d:3 * d] * _dot(four_ref[...], wfo_ref[...]))
    mix = _dot(merged.astype(BF16), wo_ref[...])
    x1 = _layer_norm(DEEPNORM_ALPHA * x + gate1 * mix) * g1_ref[...] + b1_ref[...]
    h2 = _layer_norm(x1) * (1.0 + scale2) + shift2

    h2_hi = h2.astype(BF16)
    h2_lo = (h2 - h2_hi.astype(F32)).astype(BF16)
    logits = _dot(h2_hi, rwh_ref[...]) + _dot(h2_lo, rwh_ref[...]) + _dot(h2_hi, rwl_ref[...])
    scores = jax.nn.sigmoid(logits)
    sel = scores + rb_ref[...]
    lane_e = lax.broadcasted_iota(I32, (tm, LANES), 1)
    onehots, top_s = [], []
    for _ in range(TOP_K):
        mx = jnp.max(sel, axis=1, keepdims=True)
        idx = jnp.min(jnp.where(sel == mx, lane_e, LANES), axis=1, keepdims=True)
        oh = lane_e == idx
        onehots.append(oh)
        top_s.append(jnp.sum(jnp.where(oh, scores, 0.0), axis=1, keepdims=True))
        sel = jnp.where(oh, -jnp.inf, sel)
    total = top_s[0]
    for s_k in top_s[1:]:
        total = total + s_k

    @pl.when(i == 0)
    def _():
        carry_ref[...] = jnp.zeros_like(carry_ref)

    chosen = jnp.zeros((tm, LANES), F32)
    for oh in onehots:
        chosen = jnp.where(oh, 1.0, chosen)
    row = lax.broadcasted_iota(I32, (tm, tm), 0)
    col = lax.broadcasted_iota(I32, (tm, tm), 1)
    before = jnp.where(col < row, 1.0, 0.0).astype(BF16)
    prefix = _dot(before, chosen.astype(BF16))
    count = jnp.sum(chosen, axis=0, keepdims=True)
    cpad = jnp.floor((count + (PIECE - 1)) * (1.0 / PIECE)) * PIECE
    er = lax.broadcasted_iota(I32, (LANES, LANES), 0)
    ec = lax.broadcasted_iota(I32, (LANES, LANES), 1)
    cstart = _dot(jnp.broadcast_to(cpad, (SUBLANES, LANES)), jnp.where(er < ec, 1.0, 0.0))[0:1]
    gbase = carry_ref[...]
    carry_ref[...] = gbase + cpad
    cnt_ref[...] = gbase + cpad
    trow = lax.broadcasted_iota(I32, (SUBLANES, LANES), 0)
    tab_ref[...] = jnp.where(trow == 0, cstart, jnp.where(trow == 1, cpad, jnp.where(trow == 2, gbase, 0.0))
                             ).astype(I32)

    lane = lax.broadcasted_iota(I32, (tm, LANES), 1)
    slot = lax.broadcasted_iota(I32, (tm, SORT_ROWS), 1)
    pos_out = jnp.zeros((tm, LANES), I32)
    wts_out = jnp.zeros((tm, LANES), F32)
    perm = jnp.zeros((tm, SORT_ROWS), F32)
    where_local = prefix + cstart
    for k, oh in enumerate(onehots):
        pos = jnp.sum(jnp.where(oh, where_local, 0.0), axis=1, keepdims=True).astype(I32)
        pos_out = jnp.where(lane == k, pos, pos_out)
        wts_out = jnp.where(lane == k, ROUTED_SCALE * top_s[k] / total, wts_out)
        perm = jnp.where(slot == pos, 1.0, perm)
    pos_ref[...] = pos_out
    wts_ref[...] = wts_out
    h2b = h2.astype(BF16)
    sorted_rows = lax.dot_general(perm.astype(BF16), h2b, (((0,), (0,)), ((), ())), preferred_element_type=F32)
    xs_ref[...] = _pack_rows(sorted_rows, rounded=True)

    shared = _dot((_silu(_dot(h2b, sg_ref[...])) * _dot(h2b, su_ref[...])).astype(BF16), sd_ref[...])
    r0_ref[...] = DEEPNORM_ALPHA * x1 + gate2 * shared


def _postmix(st, x2, mod_l, z, attn, four, lw):
    n, d = x2.shape
    tm = st.tm
    hb = tm // CONV_HALO
    last_halo = n // CONV_HALO - 1
    tok = lambda w: pl.BlockSpec((tm, w), lambda i: (i, 0))
    in_specs = [tok(d), pl.BlockSpec((1, 1, 6 * d), lambda i: (st.mod_row(i), 0, 0)), tok(CONV_DIM),
                pl.BlockSpec((CONV_HALO, CONV_DIM), lambda i: (jnp.maximum(i * hb - 1, 0), 0)),
                pl.BlockSpec((CONV_HALO, CONV_DIM), lambda i: (jnp.minimum((i + 1) * hb, last_halo), 0)),
                tok(Q_DIM), tok(FOURIER_DIM)]
    args = [x2, mod_l, z, z, z, attn, four]
    weights = [lw['w_gates'], lw['conv_w'], lw['conv_b'], lw['conv_ln_g'], lw['conv_ln_b'], lw['w_conv_out'],
               lw['w_attn_out'], lw['w_fourier_out'], lw['w_out'], lw['ln1_g'], lw['ln1_b'],
               lw['router_hi'], lw['router_lo'], lw['router_bias'],
               lw['shared_w_gate'], lw['shared_w_up'], lw['shared_w_down']]
    in_specs += [_full(w.shape) for w in weights]
    args += weights
    return pl.pallas_call(
        functools.partial(_postmix_kernel, st=st),
        out_shape=[jax.ShapeDtypeStruct((n, d), F32),
                   jax.ShapeDtypeStruct((st.tiles * SORT_ROWS, PACKED), U32),
                   jax.ShapeDtypeStruct((n, LANES), I32),
                   jax.ShapeDtypeStruct((n, LANES), F32),
                   jax.ShapeDtypeStruct((st.tiles * SUBLANES, LANES), I32),
                   jax.ShapeDtypeStruct((1, LANES), F32)],
        grid=(st.tiles,),
        in_specs=in_specs,
        out_specs=[tok(d), pl.BlockSpec((SORT_ROWS, PACKED), lambda i: (i, 0)), tok(LANES), tok(LANES),
                   pl.BlockSpec((SUBLANES, LANES), lambda i: (i, 0)), _full((1, LANES))],
        scratch_shapes=[pltpu.VMEM((tm + 2 * CONV_HALO, CONV_DIM), F32),
                        pltpu.VMEM((1, LANES), F32)],
        compiler_params=pltpu.CompilerParams(dimension_semantics=("arbitrary",), vmem_limit_bytes=VMEM_LIMIT),
        name="postmix",
    )(*args)


ZERO_SPANS = (16, 8, 4, 2, 1)


def _move_kernel(pstart_ref, gap_ref, tail_ref, tab_ref, zeros_ref, src_ref, alias_ref, dst_ref, sem, zsem,
                 *, to_sorted):
    del alias_ref
    i = pl.program_id(0)

    def piece_copy(local_row, sorted_row):
        a = pl.multiple_of(local_row, PIECE)
        g = pl.multiple_of(sorted_row, PIECE)
        if to_sorted:
            return pltpu.make_async_copy(src_ref.at[pl.ds(a, PIECE)], dst_ref.at[pl.ds(g, PIECE)], sem)
        return pltpu.make_async_copy(src_ref.at[pl.ds(g, PIECE)], dst_ref.at[pl.ds(a, PIECE)], sem)

    def per_expert(e, issued):
        pieces = tab_ref[1, e] // PIECE
        local0 = i * SORT_ROWS + tab_ref[0, e]
        sorted0 = pstart_ref[e] + tab_ref[2, e]

        def one(p, c):
            piece_copy(local0 + p * PIECE, sorted0 + p * PIECE).start()
            return c

        lax.fori_loop(0, pieces, one, 0)
        return issued + pieces

    issued = lax.fori_loop(0, N_EXPERTS, per_expert, jnp.int32(0))

    def wait_one(p, c):
        piece_copy(0, 0).wait()
        return c

    lax.fori_loop(0, issued, wait_one, 0)

    if to_sorted:
        @pl.when(i == 0)
        def _():
            def zero_copy(row, span):
                r = pl.multiple_of(row, PIECE)
                return pltpu.make_async_copy(zeros_ref.at[pl.ds(0, span * PIECE)],
                                             dst_ref.at[pl.ds(r, span * PIECE)], zsem)

            def gaps(e, counts):
                row = gap_ref[0, e]
                g = gap_ref[1, e]
                new = []
                for span, cnt in zip(ZERO_SPANS, counts):
                    hit = (g & span) != 0

                    @pl.when(hit)
                    def _():
                        zero_copy(row, span).start()

                    row = row + jnp.where(hit, span * PIECE, 0)
                    new.append(cnt + jnp.where(hit, 1, 0))
                return tuple(new)

            counts = lax.fori_loop(0, N_EXPERTS, gaps, (jnp.int32(0),) * len(ZERO_SPANS))
            for span, cnt in zip(ZERO_SPANS, counts):
                lax.fori_loop(0, cnt, lambda p, c: (zero_copy(0, span).wait(), c)[1], 0)

            blk = EXPERT_ROWS // PIECE

            def tail(b, c):
                zero_copy(tail_ref[0] + b * EXPERT_ROWS, blk).start()
                return c

            lax.fori_loop(0, tail_ref[1], tail, 0)
            lax.fori_loop(0, tail_ref[1], lambda p, c: (zero_copy(0, blk).wait(), c)[1], 0)


def _move(pstart, gaps, tail, tab, zeros, src, dst_init, n_dst_rows, *, to_sorted):
    tiles = tab.shape[0] // SUBLANES
    any_spec = pl.BlockSpec(memory_space=pl.ANY)
    aliases = {} if to_sorted else {6: 0}
    return pl.pallas_call(
        functools.partial(_move_kernel, to_sorted=to_sorted),
        out_shape=jax.ShapeDtypeStruct((n_dst_rows, PACKED), U32),
        grid_spec=pltpu.PrefetchScalarGridSpec(
            num_scalar_prefetch=3,
            grid=(tiles,),
            in_specs=[pl.BlockSpec((SUBLANES, LANES), lambda i, *_: (i, 0), memory_space=pltpu.SMEM),
                      any_spec, any_spec, any_spec],
            out_specs=any_spec,
            scratch_shapes=[pltpu.SemaphoreType.DMA, pltpu.SemaphoreType.DMA]),
        input_output_aliases=aliases,
        compiler_params=pltpu.CompilerParams(vmem_limit_bytes=VMEM_LIMIT),
        name="moe_to_sorted" if to_sorted else "moe_to_tiles",
    )(pstart, gaps, tail, tab, zeros, src, dst_init)


def _experts_kernel(be_ref, bi_ref, nu_ref, xs_ref, wg_ref, wu_ref, wd_ref, ys_ref):
    used = pl.program_id(0) < nu_ref[0]

    @pl.when(used)
    def _():
        x = _unpack_rows(xs_ref[...])
        a = (_silu(_dot(x, wg_ref[0])) * _dot(x, wu_ref[0])).astype(BF16)
        ys_ref[...] = _pack_rows(_dot(a, wd_ref[0]))

    @pl.when(jnp.logical_not(used))
    def _():
        ys_ref[...] = jnp.zeros_like(ys_ref)


def _experts(block_e, block_i, n_used, xs, w_gate, w_up, w_down):
    n_rows = xs.shape[0]
    d = w_gate.shape[1]
    r = EXPERT_ROWS
    return pl.pallas_call(
        _experts_kernel,
        out_shape=jax.ShapeDtypeStruct((n_rows, PACKED), U32),
        grid_spec=pltpu.PrefetchScalarGridSpec(
            num_scalar_prefetch=3,
            grid=(n_rows // r,),
            in_specs=[pl.BlockSpec((r, PACKED), lambda j, be, bi, nu: (bi[j], 0)),
                      pl.BlockSpec((1, d, EXPERT_DIM), lambda j, be, bi, nu: (be[j], 0, 0)),
                      pl.BlockSpec((1, d, EXPERT_DIM), lambda j, be, bi, nu: (be[j], 0, 0)),
                      pl.BlockSpec((1, EXPERT_DIM, d), lambda j, be, bi, nu: (be[j], 0, 0))],
            out_specs=pl.BlockSpec((r, PACKED), lambda j, be, bi, nu: (j, 0))),
        compiler_params=pltpu.CompilerParams(dimension_semantics=("arbitrary",), vmem_limit_bytes=VMEM_LIMIT),
        name="moe_experts",
    )(block_e, block_i, n_used, xs, w_gate, w_up, w_down)


def _combine_kernel(r0_ref, pos_ref, wts_ref, mod_ref, g2_ref, b2_ref, ys_ref, o_ref):
    tm, d = r0_ref.shape
    pos = pos_ref[...]
    wts = wts_ref[...]
    slot = lax.broadcasted_iota(I32, (tm, SORT_ROWS), 1)
    weight = jnp.zeros((tm, SORT_ROWS), F32)
    for k in range(TOP_K):
        weight = jnp.where(slot == pos[:, k:k + 1], wts[:, k:k + 1], weight)
    routed = _dot(weight.astype(BF16), _unpack_rows(ys_ref[...]))
    gate2 = mod_ref[0][:, 5 * d:6 * d]
    o_ref[...] = _layer_norm(r0_ref[...] + gate2 * routed) * g2_ref[...] + b2_ref[...]


def _combine(st, r0, pos, wts, mod_l, ln2_g, ln2_b, ys_local):
    n, d = r0.shape
    tm = st.tm
    tok = lambda w: pl.BlockSpec((tm, w), lambda i: (i, 0))
    return pl.pallas_call(
        _combine_kernel,
        out_shape=jax.ShapeDtypeStruct((n, d), F32),
        grid=(st.tiles,),
        in_specs=[tok(d), tok(LANES), tok(LANES),
                  pl.BlockSpec((1, 1, 6 * d), lambda i: (st.mod_row(i), 0, 0)),
                  _full((1, d)), _full((1, d)),
                  pl.BlockSpec((SORT_ROWS, PACKED), lambda i: (i, 0))],
        out_specs=tok(d),
        compiler_params=pltpu.CompilerParams(vmem_limit_bytes=VMEM_LIMIT),
        name="moe_combine",
    )(r0, pos, wts, mod_l, ln2_g, ln2_b, ys_local)


def _moe_routed(st, xs_local, tab, counts, lw):
    r = EXPERT_ROWS
    n_blocks = -(-(st.tiles * (st.tm * TOP_K + N_EXPERTS * (PIECE - 1))) // r) + N_EXPERTS
    cnt = counts[0, :N_EXPERTS].astype(I32)
    padded = (cnt + r - 1) // r * r
    pend = jnp.cumsum(padded)
    pstart = (pend - padded).astype(I32)
    n_used = (pend[-1] // r).astype(I32)
    blocks = jnp.arange(n_blocks, dtype=I32)
    block_i = jnp.minimum(blocks, n_used - 1)
    block_e = jnp.sum((pend[None, :] <= (block_i * r)[:, None]).astype(I32), axis=1)
    block_e = jnp.minimum(block_e, N_EXPERTS - 1).astype(I32)
    gaps = jnp.stack([pstart + cnt, (padded - cnt) // PIECE]).astype(I32)
    tail = jnp.stack([n_used * r, n_blocks - n_used]).astype(I32)
    zeros = jnp.zeros((r, PACKED), U32)
    xs = _move(pstart, gaps, tail, tab, zeros, xs_local, zeros, n_blocks * r, to_sorted=True)
    ys = _experts(block_e, block_i, n_used.reshape(1), xs, lw['exp_w_gate'], lw['exp_w_up'], lw['exp_w_down'])
    return _move(pstart, gaps, tail, tab, zeros, ys, xs_local, xs_local.shape[0], to_sorted=False)


def _channel_dft():
    j = jnp.arange(FOURIER_DIM)
    same = (j[:, None] // FOURIER_GROUP_DIM) == (j[None, :] // FOURIER_GROUP_DIM)
    prod = ((j[:, None] % FOURIER_GROUP_DIM) * (j[None, :] % FOURIER_GROUP_DIM)) % FOURIER_GROUP_DIM
    ang = (2.0 * jnp.pi / FOURIER_GROUP_DIM) * prod.astype(F32)
    scale = FOURIER_GROUP_DIM ** -0.5
    return (jnp.where(same, jnp.cos(ang) * scale, 0.0).astype(BF16),
            jnp.where(same, jnp.sin(ang) * scale, 0.0).astype(BF16))


def _position_dft(n):
    j = jnp.arange(n, dtype=I32)
    ang = (2.0 * jnp.pi / n) * ((j[:, None] * j[None, :]) % n).astype(F32)
    scale = n ** -0.5
    return (jnp.cos(ang) * scale).astype(BF16), (jnp.sin(ang) * scale).astype(BF16)


def _rope_tables(s_ctx, n_lat):
    t = jnp.arange(n_lat)
    inv = ROPE_BASE ** (-jnp.arange(ROPE_HALF, dtype=F32) / ROPE_HALF)
    row_ang = (t // GRID_W).astype(F32)[:, None] * inv[None, :]
    col_ang = (t % GRID_W).astype(F32)[:, None] * inv[None, :]
    cos = jnp.concatenate([jnp.cos(row_ang)] * 2 + [jnp.cos(col_ang)] * 2, axis=1)
    sin = jnp.concatenate([-jnp.sin(row_ang), jnp.sin(row_ang), -jnp.sin(col_ang), jnp.sin(col_ang)], axis=1)
    reps = LANES // HEAD_DIM
    cos = jnp.concatenate([jnp.ones((s_ctx, LANES), F32), jnp.tile(cos, (1, reps))], axis=0)
    sin = jnp.concatenate([jnp.zeros((s_ctx, LANES), F32), jnp.tile(sin, (1, reps))], axis=0)
    return cos, sin


def _heads_first(a2, batch, seq, heads):
    return a2.reshape(batch, seq, heads, HEAD_DIM).transpose(0, 2, 1, 3)


def _tokens_first(o4):
    b, h, s, dh = o4.shape
    return o4.transpose(0, 2, 1, 3).reshape(b * s, h * dh)


def kernel(x_prompt, x_sample, cache_k, cache_v, c, c_ctx, w_ada, b_ada, w_in, conv_w, conv_b, conv_ln_g, conv_ln_b, w_conv_out, attn_sink, w_attn_out, w_fourier_out, w_out, post_ln_g, post_ln_b, router_w, router_bias, exp_w_gate, exp_w_up, exp_w_down, shared_w_gate, shared_w_up, shared_w_down):
    b_ctx, s_ctx, d = x_prompt.shape
    b_lat, s_lat, _ = x_sample.shape
    depth = w_in.shape[0]
    assert b_lat < MOD_ROWS and d == D_MODEL
    st = _Stream(b_ctx, s_ctx, b_lat, s_lat)
    nc = st.n_ctx

    cond = jnp.concatenate([c, c_ctx[None, :], jnp.zeros((MOD_ROWS - b_lat - 1, d), F32)], axis=0)
    mod = _modulation(cond, w_ada, b_ada)

    cc, sc = _channel_dft()
    cn_ctx, sn_ctx = _position_dft(s_ctx)
    cn_lat, sn_lat = _position_dft(s_lat)
    rope_cos, rope_sin = _rope_tables(s_ctx, s_lat)

    y = jnp.concatenate([x_prompt.reshape(nc, d), x_sample.reshape(b_lat * s_lat, d)], axis=0)
    new_k, new_v = [], []
    for l in range(depth):
        rw = jnp.pad(router_w[l], ((0, 0), (0, LANES - N_EXPERTS)))
        rw_hi = rw.astype(BF16)
        row = lambda a: a.reshape(1, -1)
        lw = dict(
            w_gates=w_in[l][:, PRE_COLS:].astype(BF16),
            conv_w=conv_w[l], conv_b=row(conv_b[l]), conv_ln_g=row(conv_ln_g[l]), conv_ln_b=row(conv_ln_b[l]),
            w_conv_out=w_conv_out[l].astype(BF16), w_attn_out=w_attn_out[l].astype(BF16),
            w_fourier_out=w_fourier_out[l].astype(BF16), w_out=w_out[l].astype(BF16),
            ln1_g=row(post_ln_g[l, 0]), ln1_b=row(post_ln_b[l, 0]),
            router_hi=rw_hi, router_lo=(rw - rw_hi.astype(F32)).astype(BF16),
            router_bias=jnp.pad(row(router_bias[l]), ((0, 0), (0, LANES - N_EXPERTS)), constant_values=-jnp.inf),
            exp_w_gate=exp_w_gate[l].astype(BF16), exp_w_up=exp_w_up[l].astype(BF16),
            exp_w_down=exp_w_down[l].astype(BF16),
            shared_w_gate=shared_w_gate[l].astype(BF16), shared_w_up=shared_w_up[l].astype(BF16),
            shared_w_down=shared_w_down[l].astype(BF16))
        mod_l = mod[l].reshape(MOD_ROWS, 1, 6 * d)
        sink = attn_sink[l]

        z, q, k, v, uc, us = _premix(st, y, mod_l, w_in[l][:, :PRE_COLS].astype(BF16), cc, sc, rope_cos, rope_sin)
        new_k.append(k[:nc].reshape(b_ctx, s_ctx, N_KV_HEADS, HEAD_DIM))
        new_v.append(v[:nc].reshape(b_ctx, s_ctx, N_KV_HEADS, HEAD_DIM))
        kb = k.astype(BF16)
        vb = v.astype(BF16)
        o_ctx = _ctx_attention(sink, _heads_first(q[:nc], b_ctx, s_ctx, N_HEADS),
                               _heads_first(kb[:nc], b_ctx, s_ctx, N_KV_HEADS),
                               _heads_first(vb[:nc], b_ctx, s_ctx, N_KV_HEADS))
        o_lat = _lat_attention(sink, _heads_first(q[nc:], b_lat, s_lat, N_HEADS),
                               _heads_first(kb[nc:], b_lat, s_lat, N_KV_HEADS),
                               _heads_first(vb[nc:], b_lat, s_lat, N_KV_HEADS),
                               cache_k[:, l].astype(BF16).transpose(0, 2, 1, 3),
                               cache_v[:, l].astype(BF16).transpose(0, 2, 1, 3))
        attn = jnp.concatenate([_tokens_first(o_ctx), _tokens_first(o_lat)], axis=0)
        four = jnp.concatenate([_fourier_positions(cn_ctx, sn_ctx, uc[:nc], us[:nc]),
                                _fourier_positions(cn_lat, sn_lat, uc[nc:], us[nc:])], axis=0)
        r0, xs_local, pos, wts, tab, counts = _postmix(st, y, mod_l, z, attn, four, lw)
        ys_local = _moe_routed(st, xs_local, tab, counts, lw)
        y = _combine(st, r0, pos, wts, mod_l, row(post_ln_g[l, 1]), row(post_ln_b[l, 1]), ys_local)

    return (y[:nc].reshape(b_ctx, s_ctx, d), y[nc:].reshape(b_lat, s_lat, d),
            jnp.stack(new_k, axis=1), jnp.stack(new_v, axis=1))
```

```python
import functools

import jax
import jax.numpy as jnp
from jax import lax
from jax.experimental import pallas as pl
from jax.experimental.pallas import tpu as pltpu

F32 = jnp.float32
BF16 = jnp.bfloat16
I32 = jnp.int32
U32 = jnp.uint32

D_MODEL = 1024
DEPTH = 2
GRID_W = 64
CONV_DIM = 256
CONV_WIDTH = 31
CONV_PAD = CONV_WIDTH // 2
CONV_HALO = 16
N_HEADS = 8
N_KV_HEADS = 2
HEAD_DIM = 64
Q_PER_KV = N_HEADS // N_KV_HEADS
Q_DIM = N_HEADS * HEAD_DIM
KV_DIM = N_KV_HEADS * HEAD_DIM
WINDOW = 128
ATTN_BLOCK = 128
ATTN_SCALE = HEAD_DIM ** -0.5
ROPE_AXIS_DIM = HEAD_DIM // 2
ROPE_HALF = ROPE_AXIS_DIM // 2
ROPE_BASE = 10000.0
NEG_INF = -1e30
FOURIER_GROUPS = 4
FOURIER_GROUP_DIM = 64
FOURIER_DIM = FOURIER_GROUPS * FOURIER_GROUP_DIM
PRE_COLS = 2 * CONV_DIM + Q_DIM + 2 * KV_DIM + FOURIER_DIM
N_EXPERTS = 64
TOP_K = 6
EXPERT_DIM = 256
ROUTED_SCALE = 2.5
LN_EPS = 1e-6
DEEPNORM_ALPHA = (2 * DEPTH) ** 0.25

LANES = 128
SUBLANES = 8
MOD_ROWS = 16
TOKEN_TILE = 256
EXPERT_ROWS = 256
PIECE = SUBLANES
SORT_ROWS = -(-(TOKEN_TILE * TOP_K + N_EXPERTS * (PIECE - 1)) // 256) * 256
PACKED = D_MODEL // 2
VMEM_LIMIT = 56 * 1024 * 1024


def _layer_norm(x):
    mu = jnp.mean(x, axis=-1, keepdims=True)
    xc = x - mu
    var = jnp.mean(xc * xc, axis=-1, keepdims=True)
    return xc * lax.rsqrt(var + LN_EPS)


def _silu(x):
    return x * jax.nn.sigmoid(x)


def _dot(a, b):
    return jnp.dot(a, b, preferred_element_type=F32)


def _full(shape):
    return pl.BlockSpec(shape, lambda *_: (0,) * len(shape), pipeline_mode=pl.Buffered(1))


def _pack_rows(y, rounded=False):
    bits = lax.bitcast_convert_type(y if rounded else y.astype(BF16).astype(F32), U32)
    return (bits[:, :PACKED] >> 16) | bits[:, PACKED:]


def _unpack_rows(u):
    lo = lax.bitcast_convert_type(u << 16, F32)
    hi = lax.bitcast_convert_type(u & jnp.uint32(0xFFFF0000), F32)
    return jnp.concatenate([lo, hi], axis=1).astype(BF16)


class _Stream:
    def __init__(self, b_ctx, s_ctx, b_lat, s_lat):
        self.tm = TOKEN_TILE
        assert s_ctx % self.tm == 0 and s_lat % self.tm == 0
        self.b_ctx, self.s_ctx, self.b_lat, self.s_lat = b_ctx, s_ctx, b_lat, s_lat
        self.n_ctx = b_ctx * s_ctx
        self.n = self.n_ctx + b_lat * s_lat
        self.tpb_c = s_ctx // self.tm
        self.tpb_l = s_lat // self.tm
        self.nct = self.n_ctx // self.tm
        self.tiles = self.n // self.tm
        self.ctx_row = b_lat

    def mod_row(self, i):
        return jnp.where(i >= self.nct, (i - self.nct) // self.tpb_l, self.ctx_row)

    def tile_in_seq(self, i):
        return jnp.where(i >= self.nct, (i - self.nct) % self.tpb_l, i % self.tpb_c)

    def tiles_per_seq(self, i):
        return jnp.where(i >= self.nct, self.tpb_l, self.tpb_c)

    def rope_block(self, i):
        return jnp.where(i >= self.nct, self.tpb_c + (i - self.nct) % self.tpb_l, i % self.tpb_c)


def _mod_kernel(c_ref, w_ref, b_ref, o_ref):
    o_ref[0] = _dot(_silu(c_ref[...]), w_ref[0]) + b_ref[0]


def _modulation(cond, w_ada, b_ada):
    depth, d, cols = w_ada.shape
    cb = cols // 4
    return pl.pallas_call(
        _mod_kernel,
        out_shape=jax.ShapeDtypeStruct((depth, MOD_ROWS, cols), F32),
        grid=(depth, cols // cb),
        in_specs=[pl.BlockSpec((MOD_ROWS, d), lambda l, j: (0, 0)),
                  pl.BlockSpec((1, d, cb), lambda l, j: (l, 0, j)),
                  pl.BlockSpec((1, 1, cb), lambda l, j: (l, 0, j))],
        out_specs=pl.BlockSpec((1, MOD_ROWS, cb), lambda l, j: (l, 0, j)),
        compiler_params=pltpu.CompilerParams(vmem_limit_bytes=VMEM_LIMIT),
        name="modulation",
    )(cond, w_ada, b_ada.reshape(depth, 1, cols))


def _premix_kernel(x_ref, mod_ref, w_ref, cc_ref, sc_ref, cos_ref, sin_ref,
                   z_ref, q_ref, k_ref, v_ref, uc_ref, us_ref):
    d = D_MODEL
    mod = mod_ref[0]
    h = _layer_norm(x_ref[...]) * (1.0 + mod[:, d:2 * d]) + mod[:, 0:d]
    p = _dot(h.astype(BF16), w_ref[...])
    c0 = CONV_DIM
    z_ref[...] = (p[:, 0:c0] * jax.nn.sigmoid(p[:, c0:2 * c0])).astype(z_ref.dtype)
    q0 = 2 * c0
    k0 = q0 + Q_DIM
    v0 = k0 + KV_DIM
    f0 = v0 + KV_DIM
    cos = cos_ref[...]
    sin = sin_ref[...]
    first = (lax.broadcasted_iota(I32, cos.shape, 1) % ROPE_AXIS_DIM) < ROPE_HALF

    def rot(t):
        sw = jnp.where(first, pltpu.roll(t, LANES - ROPE_HALF, 1), pltpu.roll(t, ROPE_HALF, 1))
        return t * cos + sw * sin

    q = p[:, q0:k0] * ATTN_SCALE
    q = jnp.concatenate([rot(q[:, j * LANES:(j + 1) * LANES]) for j in range(Q_DIM // LANES)], axis=1)
    q_ref[...] = q.astype(q_ref.dtype)
    k_ref[...] = rot(p[:, k0:v0])
    v_ref[...] = p[:, v0:f0]
    uf = p[:, f0:f0 + FOURIER_DIM].astype(BF16)
    uc_ref[...] = _dot(uf, cc_ref[...]).astype(uc_ref.dtype)
    us_ref[...] = _dot(uf, sc_ref[...]).astype(us_ref.dtype)


def _premix(st, x2, mod_l, w_pre, cc, sc, rope_cos, rope_sin):
    d = x2.shape[1]
    tm = st.tm
    tok = lambda w: pl.BlockSpec((tm, w), lambda i: (i, 0))
    rope = pl.BlockSpec((tm, LANES), lambda i: (st.rope_block(i), 0))
    return pl.pallas_call(
        _premix_kernel,
        out_shape=[jax.ShapeDtypeStruct((st.n, CONV_DIM), BF16),
                   jax.ShapeDtypeStruct((st.n, Q_DIM), BF16),
                   jax.ShapeDtypeStruct((st.n, KV_DIM), F32),
                   jax.ShapeDtypeStruct((st.n, KV_DIM), F32),
                   jax.ShapeDtypeStruct((st.n, FOURIER_DIM), BF16),
                   jax.ShapeDtypeStruct((st.n, FOURIER_DIM), BF16)],
        grid=(st.tiles,),
        in_specs=[tok(d), pl.BlockSpec((1, 1, 6 * d), lambda i: (st.mod_row(i), 0, 0)),
                  _full(w_pre.shape), _full(cc.shape), _full(sc.shape), rope, rope],
        out_specs=[tok(CONV_DIM), tok(Q_DIM), tok(KV_DIM), tok(KV_DIM), tok(FOURIER_DIM), tok(FOURIER_DIM)],
        compiler_params=pltpu.CompilerParams(vmem_limit_bytes=VMEM_LIMIT),
        name="premix",
    )(x2, mod_l, w_pre, cc, sc, rope_cos, rope_sin)


def _sink_softmax_pv(s, sink_col, v):
    m = jnp.maximum(jnp.max(s, axis=1, keepdims=True), sink_col)
    p = jnp.exp(s - m)
    denom = jnp.sum(p, axis=1, keepdims=True) + jnp.exp(sink_col - m)
    return _dot(p.astype(BF16), v) / denom


def _attend_heads(sink_ref, q_ref, keys, vals, bias, o_ref):
    rows = q_ref.shape[0]
    for h in range(N_KV_HEADS):
        heads = [h * Q_PER_KV + g for g in range(Q_PER_KV)]
        q = jnp.concatenate([q_ref[:, hd * HEAD_DIM:(hd + 1) * HEAD_DIM] for hd in heads], axis=0)
        s = lax.dot_general(q, keys[:, h * HEAD_DIM:(h + 1) * HEAD_DIM], (((1,), (1,)), ((), ())),
                            preferred_element_type=F32)
        if bias is not None:
            s = s + bias
        sink_col = jnp.concatenate([jnp.full((rows, 1), sink_ref[hd], F32) for hd in heads], axis=0)
        o = _sink_softmax_pv(s, sink_col, vals[:, h * HEAD_DIM:(h + 1) * HEAD_DIM])
        for g, hd in enumerate(heads):
            o_ref[:, hd * HEAD_DIM:(hd + 1) * HEAD_DIM] = o[g * rows:(g + 1) * rows].astype(o_ref.dtype)


def _ctx_attn_kernel(sink_ref, q_ref, k_ref, v_ref, o_ref):
    _attend_heads(sink_ref, q_ref, k_ref[...].astype(BF16), v_ref[...].astype(BF16), None, o_ref)


def _ctx_attention(sink, q, k, v, batch, seq):
    return pl.pallas_call(
        _ctx_attn_kernel,
        out_shape=jax.ShapeDtypeStruct((batch * seq, Q_DIM), BF16),
        grid=(batch,),
        in_specs=[pl.BlockSpec(memory_space=pltpu.SMEM),
                  pl.BlockSpec((seq, Q_DIM), lambda i: (i, 0)),
                  pl.BlockSpec((seq, KV_DIM), lambda i: (i, 0)),
                  pl.BlockSpec((seq, KV_DIM), lambda i: (i, 0))],
        out_specs=pl.BlockSpec((seq, Q_DIM), lambda i: (i, 0)),
        compiler_params=pltpu.CompilerParams(vmem_limit_bytes=VMEM_LIMIT),
        name="attn_ctx",
    )(sink, q, k, v)


def _lat_attn_kernel(sink_ref, q_ref, k_ref, v_ref, kc_ref, vc_ref, bias_ref, o_ref):
    i = pl.program_id(1)
    nblk = pl.num_programs(1)
    blk = ATTN_BLOCK
    starts = [pl.multiple_of(jnp.maximum(i - 1, 0) * blk, blk),
              pl.multiple_of(i * blk, blk),
              pl.multiple_of(jnp.minimum(i + 1, nblk - 1) * blk, blk)]
    keys = jnp.concatenate([k_ref[pl.ds(st, blk), :] for st in starts] + [kc_ref[...]], axis=0).astype(BF16)
    vals = jnp.concatenate([v_ref[pl.ds(st, blk), :] for st in starts] + [vc_ref[...]], axis=0).astype(BF16)
    _attend_heads(sink_ref, q_ref, keys, vals, bias_ref[0], o_ref)


def _window_bias(n_ctx_keys):
    blk = ATTN_BLOCK
    r = jnp.arange(blk)[:, None]
    j = jnp.arange(3 * blk)[None, :]
    band = (j >= r) & (j <= r + 2 * WINDOW)
    has_prev = j >= blk
    has_next = j < 2 * blk
    variants = [band & has_prev, band, band & has_next]
    out = []
    for m in variants:
        full = jnp.concatenate([m, jnp.ones((blk, n_ctx_keys), bool)], axis=1)
        out.append(jnp.tile(jnp.where(full, 0.0, NEG_INF).astype(F32), (Q_PER_KV, 1)))
    return jnp.stack(out)


def _lat_attention(sink, q, q_row0, k, v, kc, vc, batch, seq):
    blk = ATTN_BLOCK
    nblk = seq // blk
    assert nblk >= 2 and WINDOW == blk and q_row0 % blk == 0
    pc = kc.shape[0] // batch
    bias = _window_bias(pc)
    nkeys = 3 * blk + pc
    q0 = q_row0 // blk
    bias_map = lambda i, j: (jnp.where(j == 0, 0, jnp.where(j == nblk - 1, 2, 1)), 0, 0)
    return pl.pallas_call(
        _lat_attn_kernel,
        out_shape=jax.ShapeDtypeStruct((batch * seq, Q_DIM), BF16),
        grid=(batch, nblk),
        in_specs=[pl.BlockSpec(memory_space=pltpu.SMEM),
                  pl.BlockSpec((blk, Q_DIM), lambda i, j: (q0 + i * nblk + j, 0)),
                  pl.BlockSpec((seq, KV_DIM), lambda i, j: (i, 0)),
                  pl.BlockSpec((seq, KV_DIM), lambda i, j: (i, 0)),
                  pl.BlockSpec((pc, KV_DIM), lambda i, j: (i, 0)),
                  pl.BlockSpec((pc, KV_DIM), lambda i, j: (i, 0)),
                  pl.BlockSpec((1, Q_PER_KV * blk, nkeys), bias_map)],
        out_specs=pl.BlockSpec((blk, Q_DIM), lambda i, j: (i * nblk + j, 0)),
        compiler_params=pltpu.CompilerParams(vmem_limit_bytes=VMEM_LIMIT),
        name="attn_lat",
    )(sink, q, k, v, kc, vc, bias)


def _fourier_kernel(cn_ref, sn_ref, uc_ref, us_ref, o_ref):
    o_ref[...] = (_dot(cn_ref[...], uc_ref[...]) - _dot(sn_ref[...], us_ref[...])).astype(o_ref.dtype)


def _fourier_positions(cn, sn, uc, us, nb):
    seq = cn.shape[0]
    tf = min(512, seq)
    jt = seq // tf
    return pl.pallas_call(
        _fourier_kernel,
        out_shape=jax.ShapeDtypeStruct((nb * seq, FOURIER_DIM), BF16),
        grid=(jt, nb),
        in_specs=[pl.BlockSpec((tf, seq), lambda j, b: (j, 0)),
                  pl.BlockSpec((tf, seq), lambda j, b: (j, 0)),
                  pl.BlockSpec((seq, FOURIER_DIM), lambda j, b: (b, 0)),
                  pl.BlockSpec((seq, FOURIER_DIM), lambda j, b: (b, 0))],
        out_specs=pl.BlockSpec((tf, FOURIER_DIM), lambda j, b: (b * jt + j, 0)),
        compiler_params=pltpu.CompilerParams(vmem_limit_bytes=VMEM_LIMIT),
        name="fourier",
    )(cn, sn, uc, us)


def _postmix_kernel(x_ref, mod_ref, z_ref, zp_ref, zn_ref, attn_c_ref, attn_l_ref, four_c_ref, four_l_ref,
                    wg_ref, cw_ref, cb_ref, cg_ref,
                    cbeta_ref, wco_ref, wao_ref, wfo_ref, wo_ref, g1_ref, b1_ref, rwh_ref, rwl_ref, rb_ref,
                    sg_ref, su_ref, sd_ref,
                    r0_ref, xs_ref, pos_ref, wts_ref, tab_ref, cnt_ref, zs_ref, carry_ref, *, st):
    d = D_MODEL
    i = pl.program_id(0)
    tm = x_ref.shape[0]
    is_lat = i >= st.nct
    attn = jnp.where(is_lat, attn_l_ref[...], attn_c_ref[...])
    four = jnp.where(is_lat, four_l_ref[...], four_c_ref[...])
    x = x_ref[...]
    mod = mod_ref[0]
    shift1, scale1, gate1, shift2, scale2, gate2 = [mod[:, j * d:(j + 1) * d] for j in range(6)]
    h = (_layer_norm(x) * (1.0 + scale1) + shift1).astype(BF16)
    gates = jax.nn.sigmoid(_dot(h, wg_ref[...]))

    tib = st.tile_in_seq(i)
    zs_ref[0:CONV_HALO, :] = zp_ref[...].astype(F32) * jnp.where(tib != 0, 1.0, 0.0)
    zs_ref[CONV_HALO + tm:, :] = zn_ref[...].astype(F32) * jnp.where(tib != st.tiles_per_seq(i) - 1, 1.0, 0.0)
    zs_ref[CONV_HALO:CONV_HALO + tm, :] = z_ref[...].astype(F32)
    acc = jnp.broadcast_to(cb_ref[...], (tm, CONV_DIM))
    for j in range(CONV_WIDTH):
        off = CONV_HALO - CONV_PAD + j
        acc = acc + cw_ref[j:j + 1, :] * zs_ref[off:off + tm, :]
    conv = _silu(_layer_norm(acc) * cg_ref[...] + cbeta_ref[...]).astype(BF16)

    merged = (gates[:, 0:d] * _dot(conv, wco_ref[...])
              + gates[:, d:2 * d] * _dot(attn, wao_ref[...])
              + gates[:, 2 * d:3 * d] * _dot(four, wfo_ref[...]))
    mix = _dot(merged.astype(BF16), wo_ref[...])
    x1 = _layer_norm(DEEPNORM_ALPHA * x + gate1 * mix) * g1_ref[...] + b1_ref[...]
    h2 = _layer_norm(x1) * (1.0 + scale2) + shift2

    h2_hi = h2.astype(BF16)
    h2_lo = (h2 - h2_hi.astype(F32)).astype(BF16)
    logits = _dot(h2_hi, rwh_ref[...]) + _dot(h2_lo, rwh_ref[...]) + _dot(h2_hi, rwl_ref[...])
    scores = jax.nn.sigmoid(logits)
    sel = scores + rb_ref[...]
    lane_e = lax.broadcasted_iota(I32, (tm, LANES), 1)
    onehots, top_s = [], []
    for _ in range(TOP_K):
        mx = jnp.max(sel, axis=1, keepdims=True)
        idx = jnp.min(jnp.where(sel == mx, lane_e, LANES), axis=1, keepdims=True)
        oh = lane_e == idx
        onehots.append(oh)
        top_s.append(jnp.sum(jnp.where(oh, scores, 0.0), axis=1, keepdims=True))
        sel = jnp.where(oh, -jnp.inf, sel)
    total = top_s[0]
    for s_k in top_s[1:]:
        total = total + s_k

    @pl.when(i == 0)
    def _():
        carry_ref[...] = jnp.zeros_like(carry_ref)

    chosen = jnp.zeros((tm, LANES), F32)
    for oh in onehots:
        chosen = jnp.where(oh, 1.0, chosen)
    row = lax.broadcasted_iota(I32, (tm, tm), 0)
    col = lax.broadcasted_iota(I32, (tm, tm), 1)
    before = jnp.where(col < row, 1.0, 0.0).astype(BF16)
    prefix = _dot(before, chosen.astype(BF16))
    count = jnp.sum(chosen, axis=0, keepdims=True)
    cpad = jnp.floor((count + (PIECE - 1)) * (1.0 / PIECE)) * PIECE
    er = lax.broadcasted_iota(I32, (LANES, LANES), 0)
    ec = lax.broadcasted_iota(I32, (LANES, LANES), 1)
    cstart = _dot(jnp.broadcast_to(cpad, (SUBLANES, LANES)), jnp.where(er < ec, 1.0, 0.0))[0:1]
    gbase = carry_ref[...]
    carry_ref[...] = gbase + cpad
    cnt_ref[...] = gbase + cpad
    pieces = jnp.sum(cpad, axis=1, keepdims=True) * (1.0 / PIECE)
    trow = lax.broadcasted_iota(I32, (SUBLANES, LANES), 0)
    tab_ref[...] = jnp.where(trow == 0, cstart, jnp.where(trow == 1, cpad, jnp.where(trow == 2, gbase, pieces))
                             ).astype(I32)

    lane = lax.broadcasted_iota(I32, (tm, LANES), 1)
    slot = lax.broadcasted_iota(I32, (tm, SORT_ROWS), 1)
    pos_out = jnp.zeros((tm, LANES), I32)
    wts_out = jnp.zeros((tm, LANES), F32)
    perm = jnp.zeros((tm, SORT_ROWS), F32)
    where_local = prefix + cstart
    for k, oh in enumerate(onehots):
        pos = jnp.sum(jnp.where(oh, where_local, 0.0), axis=1, keepdims=True).astype(I32)
        pos_out = jnp.where(lane == k, pos, pos_out)
        wts_out = jnp.where(lane == k, ROUTED_SCALE * top_s[k] / total, wts_out)
        perm = jnp.where(slot == pos, 1.0, perm)
    pos_ref[...] = pos_out
    wts_ref[...] = wts_out
    h2b = h2.astype(BF16)
    sorted_rows = lax.dot_general(perm.astype(BF16), h2b, (((0,), (0,)), ((), ())), preferred_element_type=F32)
    xs_ref[...] = _pack_rows(sorted_rows, rounded=True)

    shared = _dot((_silu(_dot(h2b, sg_ref[...])) * _dot(h2b, su_ref[...])).astype(BF16), sd_ref[...])
    r0_ref[...] = DEEPNORM_ALPHA * x1 + gate2 * shared


def _postmix(st, x2, mod_l, z, attn_c, attn_l, four_c, four_l, lw):
    n, d = x2.shape
    tm = st.tm
    hb = tm // CONV_HALO
    last_halo = n // CONV_HALO - 1
    tok = lambda w: pl.BlockSpec((tm, w), lambda i: (i, 0))
    ctx_tok = lambda w: pl.BlockSpec((tm, w), lambda i: (jnp.minimum(i, st.nct - 1), 0))
    lat_tok = lambda w: pl.BlockSpec((tm, w), lambda i: (jnp.maximum(i - st.nct, 0), 0))
    in_specs = [tok(d), pl.BlockSpec((1, 1, 6 * d), lambda i: (st.mod_row(i), 0, 0)), tok(CONV_DIM),
                pl.BlockSpec((CONV_HALO, CONV_DIM), lambda i: (jnp.maximum(i * hb - 1, 0), 0)),
                pl.BlockSpec((CONV_HALO, CONV_DIM), lambda i: (jnp.minimum((i + 1) * hb, last_halo), 0)),
                ctx_tok(Q_DIM), lat_tok(Q_DIM), ctx_tok(FOURIER_DIM), lat_tok(FOURIER_DIM)]
    args = [x2, mod_l, z, z, z, attn_c, attn_l, four_c, four_l]
    weights = [lw['w_gates'], lw['conv_w'], lw['conv_b'], lw['conv_ln_g'], lw['conv_ln_b'], lw['w_conv_out'],
               lw['w_attn_out'], lw['w_fourier_out'], lw['w_out'], lw['ln1_g'], lw['ln1_b'],
               lw['router_hi'], lw['router_lo'], lw['router_bias'],
               lw['shared_w_gate'], lw['shared_w_up'], lw['shared_w_down']]
    in_specs += [_full(w.shape) for w in weights]
    args += weights
    return pl.pallas_call(
        functools.partial(_postmix_kernel, st=st),
        out_shape=[jax.ShapeDtypeStruct((n, d), F32),
                   jax.ShapeDtypeStruct((st.tiles * SORT_ROWS, PACKED), U32),
                   jax.ShapeDtypeStruct((n, LANES), I32),
                   jax.ShapeDtypeStruct((n, LANES), F32),
                   jax.ShapeDtypeStruct((st.tiles * SUBLANES, LANES), I32),
                   jax.ShapeDtypeStruct((1, LANES), F32)],
        grid=(st.tiles,),
        in_specs=in_specs,
        out_specs=[tok(d), pl.BlockSpec((SORT_ROWS, PACKED), lambda i: (i, 0)), tok(LANES), tok(LANES),
                   pl.BlockSpec((SUBLANES, LANES), lambda i: (i, 0)), _full((1, LANES))],
        scratch_shapes=[pltpu.VMEM((tm + 2 * CONV_HALO, CONV_DIM), F32),
                        pltpu.VMEM((1, LANES), F32)],
        compiler_params=pltpu.CompilerParams(dimension_semantics=("arbitrary",), vmem_limit_bytes=VMEM_LIMIT),
        name="postmix",
    )(*args)


ZERO_SPANS = (16, 8, 4, 2, 1)


def _for_each_piece(pstart_ref, tab_ref, fn):
    def per_expert(e, c):
        local0 = tab_ref[0, e]
        sorted0 = pstart_ref[e] + tab_ref[2, e]

        def one(p, c2):
            fn(pl.multiple_of(local0 + p * PIECE, PIECE), pl.multiple_of(sorted0 + p * PIECE, PIECE))
            return c2

        lax.fori_loop(0, tab_ref[1, e] // PIECE, one, 0)
        return c

    lax.fori_loop(0, N_EXPERTS, per_expert, 0)


def _to_sorted_kernel(pstart_ref, gap_ref, tail_ref, tab_ref, xs_ref, dst_ref, zeros_ref, sem, zsem):
    i = pl.program_id(0)

    def piece_copy(local_row, sorted_row):
        return pltpu.make_async_copy(xs_ref.at[pl.ds(local_row, PIECE)], dst_ref.at[pl.ds(sorted_row, PIECE)], sem)

    _for_each_piece(pstart_ref, tab_ref, lambda a, g: piece_copy(a, g).start())
    lax.fori_loop(0, tab_ref[3, 0], lambda p, c: (piece_copy(0, 0).wait(), c)[1], 0)

    @pl.when(i == 0)
    def _():
        zeros_ref[...] = jnp.zeros_like(zeros_ref)

        def zero_copy(row, span):
            r = pl.multiple_of(row, PIECE)
            return pltpu.make_async_copy(zeros_ref.at[pl.ds(0, span * PIECE)],
                                         dst_ref.at[pl.ds(r, span * PIECE)], zsem)

        def gaps(e, counts):
            row = gap_ref[0, e]
            g = gap_ref[1, e]
            new = []
            for span, cnt in zip(ZERO_SPANS, counts):
                hit = (g & span) != 0

                @pl.when(hit)
                def _():
                    zero_copy(row, span).start()

                row = row + jnp.where(hit, span * PIECE, 0)
                new.append(cnt + jnp.where(hit, 1, 0))
            return tuple(new)

        counts = lax.fori_loop(0, N_EXPERTS, gaps, (jnp.int32(0),) * len(ZERO_SPANS))
        for span, cnt in zip(ZERO_SPANS, counts):
            lax.fori_loop(0, cnt, lambda p, c: (zero_copy(0, span).wait(), c)[1], 0)

        blk = EXPERT_ROWS // PIECE

        def tail(b, c):
            zero_copy(tail_ref[0] + b * EXPERT_ROWS, blk).start()
            return c

        lax.fori_loop(0, tail_ref[1], tail, 0)
        lax.fori_loop(0, tail_ref[1], lambda p, c: (zero_copy(0, blk).wait(), c)[1], 0)


def _to_sorted(pstart, gaps, tail, tab, xs_local, n_rows):
    tiles = tab.shape[0] // SUBLANES
    return pl.pallas_call(
        _to_sorted_kernel,
        out_shape=jax.ShapeDtypeStruct((n_rows, PACKED), U32),
        grid_spec=pltpu.PrefetchScalarGridSpec(
            num_scalar_prefetch=3,
            grid=(tiles,),
            in_specs=[pl.BlockSpec((SUBLANES, LANES), lambda i, *_: (i, 0), memory_space=pltpu.SMEM),
                      pl.BlockSpec((SORT_ROWS, PACKED), lambda i, *_: (i, 0))],
            out_specs=pl.BlockSpec(memory_space=pl.ANY),
            scratch_shapes=[pltpu.VMEM((EXPERT_ROWS, PACKED), U32),
                            pltpu.SemaphoreType.DMA, pltpu.SemaphoreType.DMA]),
        compiler_params=pltpu.CompilerParams(dimension_semantics=("arbitrary",), vmem_limit_bytes=VMEM_LIMIT),
        name="moe_to_sorted",
    )(pstart, gaps, tail, tab, xs_local)


def _experts_kernel(be_ref, bi_ref, nu_ref, xs_ref, wg_ref, wu_ref, wd_ref, ys_ref, wgb_ref, wub_ref, wdb_ref):
    j = pl.program_id(0)
    used = j < nu_ref[0]

    @pl.when(jnp.logical_or(j == 0, be_ref[j] != be_ref[jnp.maximum(j - 1, 0)]))
    def _():
        wgb_ref[...] = wg_ref[0].astype(BF16)
        wub_ref[...] = wu_ref[0].astype(BF16)
        wdb_ref[...] = wd_ref[0].astype(BF16)

    @pl.when(used)
    def _():
        x = _unpack_rows(xs_ref[...])
        a = (_silu(_dot(x, wgb_ref[...])) * _dot(x, wub_ref[...])).astype(BF16)
        ys_ref[...] = _pack_rows(_dot(a, wdb_ref[...]))

    @pl.when(jnp.logical_not(used))
    def _():
        ys_ref[...] = jnp.zeros_like(ys_ref)


def _experts(block_e, block_i, n_used, xs, w_gate, w_up, w_down):
    n_rows = xs.shape[0]
    d = w_gate.shape[1]
    r = EXPERT_ROWS
    return pl.pallas_call(
        _experts_kernel,
        out_shape=jax.ShapeDtypeStruct((n_rows, PACKED), U32),
        grid_spec=pltpu.PrefetchScalarGridSpec(
            num_scalar_prefetch=3,
            grid=(n_rows // r,),
            in_specs=[pl.BlockSpec((r, PACKED), lambda j, be, bi, nu: (bi[j], 0)),
                      pl.BlockSpec((1, d, EXPERT_DIM), lambda j, be, bi, nu: (be[j], 0, 0)),
                      pl.BlockSpec((1, d, EXPERT_DIM), lambda j, be, bi, nu: (be[j], 0, 0)),
                      pl.BlockSpec((1, EXPERT_DIM, d), lambda j, be, bi, nu: (be[j], 0, 0))],
            out_specs=pl.BlockSpec((r, PACKED), lambda j, be, bi, nu: (j, 0)),
            scratch_shapes=[pltpu.VMEM((d, EXPERT_DIM), BF16), pltpu.VMEM((d, EXPERT_DIM), BF16),
                            pltpu.VMEM((EXPERT_DIM, d), BF16)]),
        compiler_params=pltpu.CompilerParams(dimension_semantics=("arbitrary",), vmem_limit_bytes=VMEM_LIMIT),
        name="moe_experts",
    )(block_e, block_i, n_used, xs, w_gate, w_up, w_down)


def _combine_kernel(pstart_ref, tab_ref, tab_next_ref, r0_ref, pos_ref, wts_ref, mod_ref, g2_ref, b2_ref, ys_ref,
                    o_ref, buf_ref, sem):
    tm, d = r0_ref.shape
    i = pl.program_id(0)
    slot = i % 2

    def piece_copy(slot_idx, local_row, sorted_row):
        return pltpu.make_async_copy(ys_ref.at[pl.ds(sorted_row, PIECE)],
                                     buf_ref.at[slot_idx, pl.ds(local_row, PIECE)], sem.at[slot_idx])

    @pl.when(i == 0)
    def _():
        buf_ref[...] = jnp.zeros_like(buf_ref)
        _for_each_piece(pstart_ref, tab_ref, lambda a, g: piece_copy(0, a, g).start())

    @pl.when(i + 1 < pl.num_programs(0))
    def _():
        _for_each_piece(pstart_ref, tab_next_ref, lambda a, g: piece_copy(1 - slot, a, g).start())

    lax.fori_loop(0, tab_ref[3, 0], lambda p, c: (piece_copy(slot, 0, 0).wait(), c)[1], 0)

    pos = pos_ref[...]
    wts = wts_ref[...]
    col = lax.broadcasted_iota(I32, (tm, SORT_ROWS), 1)
    weight = jnp.zeros((tm, SORT_ROWS), F32)
    for k in range(TOP_K):
        weight = jnp.where(col == pos[:, k:k + 1], wts[:, k:k + 1], weight)
    routed = _dot(weight.astype(BF16), _unpack_rows(buf_ref[slot]))
    gate2 = mod_ref[0][:, 5 * d:6 * d]
    o_ref[...] = _layer_norm(r0_ref[...] + gate2 * routed) * g2_ref[...] + b2_ref[...]


def _combine(st, pstart, tab, r0, pos, wts, mod_l, ln2_g, ln2_b, ys_sorted):
    n, d = r0.shape
    tm = st.tm
    tok = lambda w: pl.BlockSpec((tm, w), lambda i, *_: (i, 0))
    last = st.tiles - 1
    return pl.pallas_call(
        _combine_kernel,
        out_shape=jax.ShapeDtypeStruct((n, d), F32),
        grid_spec=pltpu.PrefetchScalarGridSpec(
            num_scalar_prefetch=1,
            grid=(st.tiles,),
            in_specs=[pl.BlockSpec((SUBLANES, LANES), lambda i, *_: (i, 0), memory_space=pltpu.SMEM),
                      pl.BlockSpec((SUBLANES, LANES), lambda i, *_: (jnp.minimum(i + 1, last), 0),
                                   memory_space=pltpu.SMEM),
                      tok(d), tok(LANES), tok(LANES),
                      pl.BlockSpec((1, 1, 6 * d), lambda i, *_: (st.mod_row(i), 0, 0)),
                      _full((1, d)), _full((1, d)),
                      pl.BlockSpec(memory_space=pl.ANY)],
            out_specs=tok(d),
            scratch_shapes=[pltpu.VMEM((2, SORT_ROWS, PACKED), U32), pltpu.SemaphoreType.DMA((2,))]),
        compiler_params=pltpu.CompilerParams(dimension_semantics=("arbitrary",), vmem_limit_bytes=VMEM_LIMIT),
        name="moe_combine",
    )(pstart, tab, tab, r0, pos, wts, mod_l, ln2_g, ln2_b, ys_sorted)


def _moe_routed(st, xs_local, tab, counts, lw):
    r = EXPERT_ROWS
    n_blocks = -(-(st.tiles * (st.tm * TOP_K + N_EXPERTS * (PIECE - 1))) // r) + N_EXPERTS
    cnt = counts[0, :N_EXPERTS].astype(I32)
    padded = (cnt + r - 1) // r * r
    pend = jnp.cumsum(padded)
    pstart = (pend - padded).astype(I32)
    n_used = (pend[-1] // r).astype(I32)
    blocks = jnp.arange(n_blocks, dtype=I32)
    block_i = jnp.minimum(blocks, n_used - 1)
    block_e = jnp.sum((pend[None, :] <= (block_i * r)[:, None]).astype(I32), axis=1)
    block_e = jnp.minimum(block_e, N_EXPERTS - 1).astype(I32)
    gaps = jnp.stack([pstart + cnt, (padded - cnt) // PIECE]).astype(I32)
    tail = jnp.stack([n_used * r, n_blocks - n_used]).astype(I32)
    xs = _to_sorted(pstart, gaps, tail, tab, xs_local, n_blocks * r)
    ys = _experts(block_e, block_i, n_used.reshape(1), xs, lw['exp_w_gate'], lw['exp_w_up'], lw['exp_w_down'])
    return pstart, ys


def _channel_dft():
    j = jnp.arange(FOURIER_DIM)
    same = (j[:, None] // FOURIER_GROUP_DIM) == (j[None, :] // FOURIER_GROUP_DIM)
    prod = ((j[:, None] % FOURIER_GROUP_DIM) * (j[None, :] % FOURIER_GROUP_DIM)) % FOURIER_GROUP_DIM
    ang = (2.0 * jnp.pi / FOURIER_GROUP_DIM) * prod.astype(F32)
    scale = FOURIER_GROUP_DIM ** -0.5
    return (jnp.where(same, jnp.cos(ang) * scale, 0.0).astype(BF16),
            jnp.where(same, jnp.sin(ang) * scale, 0.0).astype(BF16))


def _position_dft(n):
    m = 1 << (n.bit_length() // 2)
    assert n % m == 0
    k = jnp.arange(n, dtype=I32)[None, :]
    ang_hi = (2.0 * jnp.pi / n) * ((jnp.arange(n // m, dtype=I32)[:, None] * m * k) % n).astype(F32)
    ang_lo = (2.0 * jnp.pi / n) * ((jnp.arange(m, dtype=I32)[:, None] * k) % n).astype(F32)
    scale = n ** -0.5
    ch, sh = (jnp.cos(ang_hi) * scale)[:, None, :], (jnp.sin(ang_hi) * scale)[:, None, :]
    cl, sl = jnp.cos(ang_lo)[None, :, :], jnp.sin(ang_lo)[None, :, :]
    return ((ch * cl - sh * sl).reshape(n, n).astype(BF16), (sh * cl + ch * sl).reshape(n, n).astype(BF16))


def _rope_tables(s_ctx, n_lat):
    t = jnp.arange(n_lat)
    inv = ROPE_BASE ** (-jnp.arange(ROPE_HALF, dtype=F32) / ROPE_HALF)
    row_ang = (t // GRID_W).astype(F32)[:, None] * inv[None, :]
    col_ang = (t % GRID_W).astype(F32)[:, None] * inv[None, :]
    cos = jnp.concatenate([jnp.cos(row_ang)] * 2 + [jnp.cos(col_ang)] * 2, axis=1)
    sin = jnp.concatenate([-jnp.sin(row_ang), jnp.sin(row_ang), -jnp.sin(col_ang), jnp.sin(col_ang)], axis=1)
    reps = LANES // HEAD_DIM
    cos = jnp.concatenate([jnp.ones((s_ctx, LANES), F32), jnp.tile(cos, (1, reps))], axis=0)
    sin = jnp.concatenate([jnp.zeros((s_ctx, LANES), F32), jnp.tile(sin, (1, reps))], axis=0)
    return cos, sin


def kernel(x_prompt, x_sample, cache_k, cache_v, c, c_ctx, w_ada, b_ada, w_in, conv_w, conv_b, conv_ln_g, conv_ln_b, w_conv_out, attn_sink, w_attn_out, w_fourier_out, w_out, post_ln_g, post_ln_b, router_w, router_bias, exp_w_gate, exp_w_up, exp_w_down, shared_w_gate, shared_w_up, shared_w_down):
    b_ctx, s_ctx, d = x_prompt.shape
    b_lat, s_lat, _ = x_sample.shape
    depth = w_in.shape[0]
    assert b_lat < MOD_ROWS and d == D_MODEL
    st = _Stream(b_ctx, s_ctx, b_lat, s_lat)
    nc = st.n_ctx

    cond = jnp.concatenate([c, c_ctx[None, :], jnp.zeros((MOD_ROWS - b_lat - 1, d), F32)], axis=0)
    mod = _modulation(cond, w_ada, b_ada)

    cc, sc = _channel_dft()
    cn_ctx, sn_ctx = _position_dft(s_ctx)
    cn_lat, sn_lat = _position_dft(s_lat)
    rope_cos, rope_sin = _rope_tables(s_ctx, s_lat)

    y = jnp.concatenate([x_prompt.reshape(nc, d), x_sample.reshape(b_lat * s_lat, d)], axis=0)
    new_k, new_v = [], []
    for l in range(depth):
        rw = jnp.pad(router_w[l], ((0, 0), (0, LANES - N_EXPERTS)))
        rw_hi = rw.astype(BF16)
        row = lambda a: a.reshape(1, -1)
        lw = dict(
            w_gates=w_in[l][:, PRE_COLS:].astype(BF16),
            conv_w=conv_w[l], conv_b=row(conv_b[l]), conv_ln_g=row(conv_ln_g[l]), conv_ln_b=row(conv_ln_b[l]),
            w_conv_out=w_conv_out[l].astype(BF16), w_attn_out=w_attn_out[l].astype(BF16),
            w_fourier_out=w_fourier_out[l].astype(BF16), w_out=w_out[l].astype(BF16),
            ln1_g=row(post_ln_g[l, 0]), ln1_b=row(post_ln_b[l, 0]),
            router_hi=rw_hi, router_lo=(rw - rw_hi.astype(F32)).astype(BF16),
            router_bias=jnp.pad(row(router_bias[l]), ((0, 0), (0, LANES - N_EXPERTS)), constant_values=-jnp.inf),
            exp_w_gate=exp_w_gate[l], exp_w_up=exp_w_up[l], exp_w_down=exp_w_down[l],
            shared_w_gate=shared_w_gate[l].astype(BF16), shared_w_up=shared_w_up[l].astype(BF16),
            shared_w_down=shared_w_down[l].astype(BF16))
        mod_l = mod[l].reshape(MOD_ROWS, 1, 6 * d)
        sink = attn_sink[l]

        z, q, k, v, uc, us = _premix(st, y, mod_l, w_in[l][:, :PRE_COLS].astype(BF16), cc, sc, rope_cos, rope_sin)
        new_k.append(k[:nc].reshape(b_ctx, s_ctx, N_KV_HEADS, HEAD_DIM))
        new_v.append(v[:nc].reshape(b_ctx, s_ctx, N_KV_HEADS, HEAD_DIM))
        attn_c = _ctx_attention(sink, q, k, v, b_ctx, s_ctx)
        attn_l = _lat_attention(sink, q, nc, k[nc:], v[nc:], cache_k[:, l].reshape(-1, KV_DIM),
                                cache_v[:, l].reshape(-1, KV_DIM), b_lat, s_lat)
        four_c = _fourier_positions(cn_ctx, sn_ctx, uc, us, b_ctx)
        four_l = _fourier_positions(cn_lat, sn_lat, uc[nc:], us[nc:], b_lat)
        r0, xs_local, pos, wts, tab, counts = _postmix(st, y, mod_l, z, attn_c, attn_l, four_c, four_l, lw)
        pstart, ys_sorted = _moe_routed(st, xs_local, tab, counts, lw)
        y = _combine(st, pstart, tab, r0, pos, wts, mod_l, row(post_ln_g[l, 1]), row(post_ln_b[l, 1]), ys_sorted)

    return (y[:nc].reshape(b_ctx, s_ctx, d), y[nc:].reshape(b_lat, s_lat, d),
            jnp.stack(new_k, axis=1), jnp.stack(new_v, axis=1))
```

```python
import functools

import jax
import jax.numpy as jnp
from jax import lax
from jax.experimental import pallas as pl
from jax.experimental.pallas import tpu as pltpu

F32 = jnp.float32
BF16 = jnp.bfloat16
I32 = jnp.int32
U32 = jnp.uint32

D_MODEL = 1024
DEPTH = 2
GRID_W = 64
CONV_DIM = 256
CONV_WIDTH = 31
CONV_PAD = CONV_WIDTH // 2
CONV_HALO = 16
N_HEADS = 8
N_KV_HEADS = 2
HEAD_DIM = 64
Q_PER_KV = N_HEADS // N_KV_HEADS
Q_DIM = N_HEADS * HEAD_DIM
KV_DIM = N_KV_HEADS * HEAD_DIM
WINDOW = 128
ATTN_BLOCK = 128
ATTN_SCALE = HEAD_DIM ** -0.5
ROPE_AXIS_DIM = HEAD_DIM // 2
ROPE_HALF = ROPE_AXIS_DIM // 2
ROPE_BASE = 10000.0
NEG_INF = -1e30
FOURIER_GROUPS = 4
FOURIER_GROUP_DIM = 64
FOURIER_DIM = FOURIER_GROUPS * FOURIER_GROUP_DIM
PRE_COLS = 2 * CONV_DIM + Q_DIM + 2 * KV_DIM + FOURIER_DIM
N_EXPERTS = 64
TOP_K = 6
EXPERT_DIM = 256
ROUTED_SCALE = 2.5
LN_EPS = 1e-6
DEEPNORM_ALPHA = (2 * DEPTH) ** 0.25

LANES = 128
SUBLANES = 8
MOD_ROWS = 16
TOKEN_TILE = 256
EXPERT_ROWS = 512
PIECE = SUBLANES
SORT_ROWS = -(-(TOKEN_TILE * TOP_K + N_EXPERTS * (PIECE - 1)) // 256) * 256
SORT_PIECES = SORT_ROWS // PIECE
TAB_COLS = LANES + SORT_PIECES
PIECE_UNROLL = 4
PACKED = D_MODEL // 2
VMEM_LIMIT = 56 * 1024 * 1024


def _layer_norm(x):
    mu = jnp.mean(x, axis=-1, keepdims=True)
    xc = x - mu
    var = jnp.mean(xc * xc, axis=-1, keepdims=True)
    return xc * lax.rsqrt(var + LN_EPS)


def _silu(x):
    return x * jax.nn.sigmoid(x)


def _dot(a, b):
    return jnp.dot(a, b, preferred_element_type=F32)


def _full(shape):
    return pl.BlockSpec(shape, lambda *_: (0,) * len(shape), pipeline_mode=pl.Buffered(1))


def _pack_rows(y, rounded=False):
    bits = lax.bitcast_convert_type(y if rounded else y.astype(BF16).astype(F32), U32)
    return (bits[:, :PACKED] >> 16) | bits[:, PACKED:]


def _unpack_rows(u):
    lo = lax.bitcast_convert_type(u << 16, F32)
    hi = lax.bitcast_convert_type(u & jnp.uint32(0xFFFF0000), F32)
    return jnp.concatenate([lo, hi], axis=1).astype(BF16)


class _Stream:
    def __init__(self, b_ctx, s_ctx, b_lat, s_lat):
        self.tm = TOKEN_TILE
        assert s_ctx % self.tm == 0 and s_lat % self.tm == 0
        self.b_ctx, self.s_ctx, self.b_lat, self.s_lat = b_ctx, s_ctx, b_lat, s_lat
        self.n_ctx = b_ctx * s_ctx
        self.n = self.n_ctx + b_lat * s_lat
        self.tpb_c = s_ctx // self.tm
        self.tpb_l = s_lat // self.tm
        self.nct = self.n_ctx // self.tm
        self.tiles = self.n // self.tm
        self.ctx_row = b_lat

    def mod_row(self, i):
        return jnp.where(i >= self.nct, (i - self.nct) // self.tpb_l, self.ctx_row)

    def tile_in_seq(self, i):
        return jnp.where(i >= self.nct, (i - self.nct) % self.tpb_l, i % self.tpb_c)

    def tiles_per_seq(self, i):
        return jnp.where(i >= self.nct, self.tpb_l, self.tpb_c)

    def rope_block(self, i):
        return jnp.where(i >= self.nct, self.tpb_c + (i - self.nct) % self.tpb_l, i % self.tpb_c)


def _mod_kernel(c_ref, w_ref, b_ref, o_ref):
    o_ref[0] = _dot(_silu(c_ref[...]), w_ref[0]) + b_ref[0]


def _modulation(cond, w_ada, b_ada):
    depth, d, cols = w_ada.shape
    cb = cols // 4
    return pl.pallas_call(
        _mod_kernel,
        out_shape=jax.ShapeDtypeStruct((depth, MOD_ROWS, cols), F32),
        grid=(depth, cols // cb),
        in_specs=[pl.BlockSpec((MOD_ROWS, d), lambda l, j: (0, 0)),
                  pl.BlockSpec((1, d, cb), lambda l, j: (l, 0, j)),
                  pl.BlockSpec((1, 1, cb), lambda l, j: (l, 0, j))],
        out_specs=pl.BlockSpec((1, MOD_ROWS, cb), lambda l, j: (l, 0, j)),
        compiler_params=pltpu.CompilerParams(vmem_limit_bytes=VMEM_LIMIT),
        name="modulation",
    )(cond, w_ada, b_ada.reshape(depth, 1, cols))


def _premix_kernel(xc_ref, xl_ref, mod_ref, w_ref, cc_ref, sc_ref, cos_ref, sin_ref, z_ref, *outs, nct):
    d = D_MODEL
    is_lat = pl.program_id(0) >= nct
    mod = mod_ref[0]
    h = _layer_norm(jnp.where(is_lat, xl_ref[...], xc_ref[...])) * (1.0 + mod[:, d:2 * d]) + mod[:, 0:d]
    p = _dot(h.astype(BF16), w_ref[...])
    c0 = CONV_DIM
    z_ref[...] = (p[:, 0:c0] * jax.nn.sigmoid(p[:, c0:2 * c0])).astype(z_ref.dtype)
    q0 = 2 * c0
    k0 = q0 + Q_DIM
    v0 = k0 + KV_DIM
    f0 = v0 + KV_DIM
    cos = cos_ref[...]
    sin = sin_ref[...]
    first = (lax.broadcasted_iota(I32, cos.shape, 1) % ROPE_AXIS_DIM) < ROPE_HALF

    def rot(t):
        sw = jnp.where(first, pltpu.roll(t, LANES - ROPE_HALF, 1), pltpu.roll(t, ROPE_HALF, 1))
        return t * cos + sw * sin

    q = p[:, q0:k0] * ATTN_SCALE
    q = jnp.concatenate([rot(q[:, j * LANES:(j + 1) * LANES]) for j in range(Q_DIM // LANES)], axis=1)
    uf = p[:, f0:f0 + FOURIER_DIM].astype(BF16)
    results = [q.astype(BF16), rot(p[:, k0:v0]), p[:, v0:f0],
               _dot(uf, cc_ref[...]).astype(BF16), _dot(uf, sc_ref[...]).astype(BF16)]

    @pl.when(jnp.logical_not(is_lat))
    def _():
        for r, o_ref in zip(results, outs[0::2]):
            o_ref[...] = r

    @pl.when(is_lat)
    def _():
        for r, o_ref in zip(results, outs[1::2]):
            o_ref[...] = r


def _premix(st, x_ctx, x_lat, mod_l, w_pre, cc, sc, rope_cos, rope_sin):
    d = x_ctx.shape[1]
    tm = st.tm
    tok = lambda w: pl.BlockSpec((tm, w), lambda i: (i, 0))
    ctx_tok = lambda w: pl.BlockSpec((tm, w), lambda i: (jnp.minimum(i, st.nct - 1), 0))
    lat_tok = lambda w: pl.BlockSpec((tm, w), lambda i: (jnp.maximum(i - st.nct, 0), 0))
    rope = pl.BlockSpec((tm, LANES), lambda i: (st.rope_block(i), 0))
    n_lat = st.n - st.n_ctx
    out_shape = [jax.ShapeDtypeStruct((st.n, CONV_DIM), BF16)]
    out_specs = [tok(CONV_DIM)]
    for w, dt in ((Q_DIM, BF16), (KV_DIM, F32), (KV_DIM, F32), (FOURIER_DIM, BF16), (FOURIER_DIM, BF16)):
        out_shape += [jax.ShapeDtypeStruct((st.n_ctx, w), dt), jax.ShapeDtypeStruct((n_lat, w), dt)]
        out_specs += [ctx_tok(w), lat_tok(w)]
    return pl.pallas_call(
        functools.partial(_premix_kernel, nct=st.nct),
        out_shape=out_shape,
        grid=(st.tiles,),
        in_specs=[ctx_tok(d), lat_tok(d), pl.BlockSpec((1, 1, 6 * d), lambda i: (st.mod_row(i), 0, 0)),
                  _full(w_pre.shape), _full(cc.shape), _full(sc.shape), rope, rope],
        out_specs=out_specs,
        compiler_params=pltpu.CompilerParams(dimension_semantics=("arbitrary",), vmem_limit_bytes=VMEM_LIMIT),
        name="premix",
    )(x_ctx, x_lat, mod_l, w_pre, cc, sc, rope_cos, rope_sin)


def _sink_softmax_pv(s, sink_col, v):
    m = jnp.maximum(jnp.max(s, axis=1, keepdims=True), sink_col)
    p = jnp.exp(s - m)
    denom = jnp.sum(p, axis=1, keepdims=True) + jnp.exp(sink_col - m)
    return _dot(p.astype(BF16), v) / denom


def _attend_heads(sink_ref, q_ref, keys, vals, bias, o_ref):
    rows = q_ref.shape[0]
    for h in range(N_KV_HEADS):
        heads = [h * Q_PER_KV + g for g in range(Q_PER_KV)]
        q = jnp.concatenate([q_ref[:, hd * HEAD_DIM:(hd + 1) * HEAD_DIM] for hd in heads], axis=0)
        s = lax.dot_general(q, keys[:, h * HEAD_DIM:(h + 1) * HEAD_DIM], (((1,), (1,)), ((), ())),
                            preferred_element_type=F32)
        if bias is not None:
            s = s + bias
        sink_col = jnp.concatenate([jnp.full((rows, 1), sink_ref[hd], F32) for hd in heads], axis=0)
        o = _sink_softmax_pv(s, sink_col, vals[:, h * HEAD_DIM:(h + 1) * HEAD_DIM])
        for g, hd in enumerate(heads):
            o_ref[:, hd * HEAD_DIM:(hd + 1) * HEAD_DIM] = o[g * rows:(g + 1) * rows].astype(o_ref.dtype)


def _ctx_attn_kernel(sink_ref, q_ref, k_ref, v_ref, o_ref):
    _attend_heads(sink_ref, q_ref, k_ref[...].astype(BF16), v_ref[...].astype(BF16), None, o_ref)


def _ctx_attention(sink, q, k, v, batch, seq):
    return pl.pallas_call(
        _ctx_attn_kernel,
        out_shape=jax.ShapeDtypeStruct((batch * seq, Q_DIM), BF16),
        grid=(batch,),
        in_specs=[pl.BlockSpec(memory_space=pltpu.SMEM),
                  pl.BlockSpec((seq, Q_DIM), lambda i: (i, 0)),
                  pl.BlockSpec((seq, KV_DIM), lambda i: (i, 0)),
                  pl.BlockSpec((seq, KV_DIM), lambda i: (i, 0))],
        out_specs=pl.BlockSpec((seq, Q_DIM), lambda i: (i, 0)),
        compiler_params=pltpu.CompilerParams(vmem_limit_bytes=VMEM_LIMIT),
        name="attn_ctx",
    )(sink, q, k, v)


def _lat_attn_kernel(sink_ref, q_ref, k_ref, v_ref, kc_ref, vc_ref, bias_ref, o_ref):
    i = pl.program_id(1)
    nblk = pl.num_programs(1)
    blk = ATTN_BLOCK
    starts = [pl.multiple_of(jnp.maximum(i - 1, 0) * blk, blk),
              pl.multiple_of(i * blk, blk),
              pl.multiple_of(jnp.minimum(i + 1, nblk - 1) * blk, blk)]
    keys = jnp.concatenate([k_ref[pl.ds(st, blk), :] for st in starts] + [kc_ref[...]], axis=0).astype(BF16)
    vals = jnp.concatenate([v_ref[pl.ds(st, blk), :] for st in starts] + [vc_ref[...]], axis=0).astype(BF16)
    _attend_heads(sink_ref, q_ref, keys, vals, bias_ref[0], o_ref)


def _window_bias(n_ctx_keys):
    blk = ATTN_BLOCK
    r = jnp.arange(blk)[:, None]
    j = jnp.arange(3 * blk)[None, :]
    band = (j >= r) & (j <= r + 2 * WINDOW)
    has_prev = j >= blk
    has_next = j < 2 * blk
    variants = [band & has_prev, band, band & has_next]
    out = []
    for m in variants:
        full = jnp.concatenate([m, jnp.ones((blk, n_ctx_keys), bool)], axis=1)
        out.append(jnp.tile(jnp.where(full, 0.0, NEG_INF).astype(F32), (Q_PER_KV, 1)))
    return jnp.stack(out)


def _lat_attention(sink, q, k, v, kc, vc, batch, seq):
    blk = ATTN_BLOCK
    nblk = seq // blk
    assert nblk >= 2 and WINDOW == blk
    pc = kc.shape[0] // batch
    bias = _window_bias(pc)
    nkeys = 3 * blk + pc
    bias_map = lambda i, j: (jnp.where(j == 0, 0, jnp.where(j == nblk - 1, 2, 1)), 0, 0)
    return pl.pallas_call(
        _lat_attn_kernel,
        out_shape=jax.ShapeDtypeStruct((batch * seq, Q_DIM), BF16),
        grid=(batch, nblk),
        in_specs=[pl.BlockSpec(memory_space=pltpu.SMEM),
                  pl.BlockSpec((blk, Q_DIM), lambda i, j: (i * nblk + j, 0)),
                  pl.BlockSpec((seq, KV_DIM), lambda i, j: (i, 0)),
                  pl.BlockSpec((seq, KV_DIM), lambda i, j: (i, 0)),
                  pl.BlockSpec((pc, KV_DIM), lambda i, j: (i, 0)),
                  pl.BlockSpec((pc, KV_DIM), lambda i, j: (i, 0)),
                  pl.BlockSpec((1, Q_PER_KV * blk, nkeys), bias_map)],
        out_specs=pl.BlockSpec((blk, Q_DIM), lambda i, j: (i * nblk + j, 0)),
        compiler_params=pltpu.CompilerParams(vmem_limit_bytes=VMEM_LIMIT),
        name="attn_lat",
    )(sink, q, k, v, kc, vc, bias)


def _fourier_kernel(cn_ref, sn_ref, uc_ref, us_ref, o_ref):
    o_ref[...] = (_dot(cn_ref[...], uc_ref[...]) - _dot(sn_ref[...], us_ref[...])).astype(o_ref.dtype)


def _fourier_positions(cn, sn, uc, us, nb):
    seq = cn.shape[0]
    tf = min(512, seq)
    jt = seq // tf
    return pl.pallas_call(
        _fourier_kernel,
        out_shape=jax.ShapeDtypeStruct((nb * seq, FOURIER_DIM), BF16),
        grid=(jt, nb),
        in_specs=[pl.BlockSpec((tf, seq), lambda j, b: (j, 0)),
                  pl.BlockSpec((tf, seq), lambda j, b: (j, 0)),
                  pl.BlockSpec((seq, FOURIER_DIM), lambda j, b: (b, 0)),
                  pl.BlockSpec((seq, FOURIER_DIM), lambda j, b: (b, 0))],
        out_specs=pl.BlockSpec((tf, FOURIER_DIM), lambda j, b: (b * jt + j, 0)),
        compiler_params=pltpu.CompilerParams(vmem_limit_bytes=VMEM_LIMIT),
        name="fourier",
    )(cn, sn, uc, us)


def _postmix_kernel(xc_ref, xl_ref, mod_ref, z_ref, zp_ref, zn_ref, attn_c_ref, attn_l_ref, four_c_ref, four_l_ref,
                    wg_ref, cw_ref, cb_ref, cg_ref,
                    cbeta_ref, wco_ref, wao_ref, wfo_ref, wo_ref, g1_ref, b1_ref, rwh_ref, rwl_ref, rb_ref,
                    sg_ref, su_ref, sd_ref,
                    r0_ref, xs_ref, pos_ref, wts_ref, tab_ref, cnt_ref, zs_ref, carry_ref, *, st):
    d = D_MODEL
    i = pl.program_id(0)
    tm = xc_ref.shape[0]
    is_lat = i >= st.nct
    attn = jnp.where(is_lat, attn_l_ref[...], attn_c_ref[...])
    four = jnp.where(is_lat, four_l_ref[...], four_c_ref[...])
    x = jnp.where(is_lat, xl_ref[...], xc_ref[...])
    mod = mod_ref[0]
    shift1, scale1, gate1, shift2, scale2, gate2 = [mod[:, j * d:(j + 1) * d] for j in range(6)]
    h = (_layer_norm(x) * (1.0 + scale1) + shift1).astype(BF16)
    gates = jax.nn.sigmoid(_dot(h, wg_ref[...]))

    tib = st.tile_in_seq(i)
    zs_ref[0:CONV_HALO, :] = zp_ref[...].astype(F32) * jnp.where(tib != 0, 1.0, 0.0)
    zs_ref[CONV_HALO + tm:, :] = zn_ref[...].astype(F32) * jnp.where(tib != st.tiles_per_seq(i) - 1, 1.0, 0.0)
    zs_ref[CONV_HALO:CONV_HALO + tm, :] = z_ref[...].astype(F32)
    acc = jnp.broadcast_to(cb_ref[...], (tm, CONV_DIM))
    first_off = CONV_HALO - CONV_PAD
    window = zs_ref[...]
    rows = window.shape[0]
    for r in range(SUBLANES):
        offs = [o for o in range(first_off, first_off + CONV_WIDTH) if o % SUBLANES == r]
        shifted = window if r == 0 else pltpu.roll(window, rows - r, 0)
        for o in offs:
            acc = acc + cw_ref[o - first_off:o - first_off + 1, :] * shifted[o - r:o - r + tm]
    conv =_silu(_layer_norm(acc) * cg_ref[...] + cbeta_ref[...]).astype(BF16)

    merged = (gates[:, 0:d] * _dot(conv, wco_ref[...])
              + gates[:, d:2 * d] * _dot(attn, wao_ref[...])
              + gates[:, 2 * d:3 * d] * _dot(four, wfo_ref[...]))
    mix = _dot(merged.astype(BF16), wo_ref[...])
    x1 = _layer_norm(DEEPNORM_ALPHA * x + gate1 * mix) * g1_ref[...] + b1_ref[...]
    h2 = _layer_norm(x1) * (1.0 + scale2) + shift2

    h2_hi = h2.astype(BF16)
    h2_lo = (h2 - h2_hi.astype(F32)).astype(BF16)
    logits = _dot(h2_hi, rwh_ref[...]) + _dot(h2_lo, rwh_ref[...]) + _dot(h2_hi, rwl_ref[...])
    scores = jax.nn.sigmoid(logits)
    sel = scores + rb_ref[...]
    lane_e = lax.broadcasted_iota(I32, (tm, LANES), 1)
    onehots, top_s = [], []
    for _ in range(TOP_K):
        mx = jnp.max(sel, axis=1, keepdims=True)
        idx = jnp.min(jnp.where(sel == mx, lane_e, LANES), axis=1, keepdims=True)
        oh = lane_e == idx
        onehots.append(oh)
        top_s.append(jnp.sum(jnp.where(oh, scores, 0.0), axis=1, keepdims=True))
        sel = jnp.where(oh, -jnp.inf, sel)
    total = top_s[0]
    for s_k in top_s[1:]:
        total = total + s_k

    @pl.when(i == 0)
    def _():
        carry_ref[...] = jnp.zeros_like(carry_ref)

    chosen = jnp.zeros((tm, LANES), F32)
    for oh in onehots:
        chosen = jnp.where(oh, 1.0, chosen)
    row = lax.broadcasted_iota(I32, (tm, tm), 0)
    col = lax.broadcasted_iota(I32, (tm, tm), 1)
    before = jnp.where(col < row, 1.0, 0.0).astype(BF16)
    prefix = _dot(before, chosen.astype(BF16))
    count = jnp.sum(chosen, axis=0, keepdims=True)
    cpad = jnp.floor((count + (PIECE - 1)) * (1.0 / PIECE)) * PIECE
    er = lax.broadcasted_iota(I32, (LANES, LANES), 0)
    ec = lax.broadcasted_iota(I32, (LANES, LANES), 1)
    cstart = _dot(jnp.broadcast_to(cpad, (SUBLANES, LANES)), jnp.where(er < ec, 1.0, 0.0))[0:1]
    gbase = carry_ref[...]
    carry_ref[...] = gbase + cpad
    cnt_ref[...] = gbase + cpad
    pieces = jnp.sum(cpad, axis=1, keepdims=True) * (1.0 / PIECE)
    trow = lax.broadcasted_iota(I32, (SUBLANES, LANES), 0)
    per_expert = jnp.where(trow == 0, cstart, jnp.where(trow == 1, cpad, jnp.where(trow == 2, gbase, pieces)))
    piece_row = (lax.broadcasted_iota(I32, (SORT_PIECES, LANES), 0) * PIECE).astype(F32)
    owns = jnp.where((piece_row >= cstart) & (piece_row < cstart + cpad), 1.0, 0.0).astype(BF16)
    expert_id = lax.broadcasted_iota(I32, (SUBLANES, LANES), 1).astype(BF16)
    piece_expert = lax.dot_general(expert_id, owns, (((1,), (1,)), ((), ())), preferred_element_type=F32)
    tab_ref[...] = jnp.concatenate([per_expert, piece_expert], axis=1).astype(I32)

    lane = lax.broadcasted_iota(I32, (tm, LANES), 1)
    slot = lax.broadcasted_iota(I32, (tm, SORT_ROWS), 1)
    pos_out = jnp.zeros((tm, LANES), I32)
    wts_out = jnp.zeros((tm, LANES), F32)
    perm = jnp.zeros((tm, SORT_ROWS), F32)
    where_local = prefix + cstart
    for k, oh in enumerate(onehots):
        pos = jnp.sum(jnp.where(oh, where_local, 0.0), axis=1, keepdims=True).astype(I32)
        pos_out = jnp.where(lane == k, pos, pos_out)
        wts_out = jnp.where(lane == k, ROUTED_SCALE * top_s[k] / total, wts_out)
        perm = jnp.where(slot == pos, 1.0, perm)
    pos_ref[...] = pos_out
    wts_ref[...] = wts_out
    h2b = h2.astype(BF16)
    sorted_rows = lax.dot_general(perm.astype(BF16), h2b, (((0,), (0,)), ((), ())), preferred_element_type=F32)
    xs_ref[...] = _pack_rows(sorted_rows, rounded=True)

    shared = _dot((_silu(_dot(h2b, sg_ref[...])) * _dot(h2b, su_ref[...])).astype(BF16), sd_ref[...])
    r0_ref[...] = DEEPNORM_ALPHA * x1 + gate2 * shared


def _postmix(st, x_ctx, x_lat, mod_l, z, attn_c, attn_l, four_c, four_l, lw):
    n, d = st.n, x_ctx.shape[1]
    tm = st.tm
    hb = tm // CONV_HALO
    last_halo = n // CONV_HALO - 1
    tok = lambda w: pl.BlockSpec((tm, w), lambda i: (i, 0))
    ctx_tok = lambda w: pl.BlockSpec((tm, w), lambda i: (jnp.minimum(i, st.nct - 1), 0))
    lat_tok = lambda w: pl.BlockSpec((tm, w), lambda i: (jnp.maximum(i - st.nct, 0), 0))
    in_specs = [ctx_tok(d), lat_tok(d), pl.BlockSpec((1, 1, 6 * d), lambda i: (st.mod_row(i), 0, 0)), tok(CONV_DIM),
                pl.BlockSpec((CONV_HALO, CONV_DIM), lambda i: (jnp.maximum(i * hb - 1, 0), 0)),
                pl.BlockSpec((CONV_HALO, CONV_DIM), lambda i: (jnp.minimum((i + 1) * hb, last_halo), 0)),
                ctx_tok(Q_DIM), lat_tok(Q_DIM), ctx_tok(FOURIER_DIM), lat_tok(FOURIER_DIM)]
    args = [x_ctx, x_lat, mod_l, z, z, z, attn_c, attn_l, four_c, four_l]
    weights = [lw['w_gates'], lw['conv_w'], lw['conv_b'], lw['conv_ln_g'], lw['conv_ln_b'], lw['w_conv_out'],
               lw['w_attn_out'], lw['w_fourier_out'], lw['w_out'], lw['ln1_g'], lw['ln1_b'],
               lw['router_hi'], lw['router_lo'], lw['router_bias'],
               lw['shared_w_gate'], lw['shared_w_up'], lw['shared_w_down']]
    in_specs += [_full(w.shape) for w in weights]
    args += weights
    return pl.pallas_call(
        functools.partial(_postmix_kernel, st=st),
        out_shape=[jax.ShapeDtypeStruct((n, d), F32),
                   jax.ShapeDtypeStruct((st.tiles * SORT_ROWS, PACKED), U32),
                   jax.ShapeDtypeStruct((n, LANES), I32),
                   jax.ShapeDtypeStruct((n, LANES), F32),
                   jax.ShapeDtypeStruct((st.tiles * SUBLANES, TAB_COLS), I32),
                   jax.ShapeDtypeStruct((1, LANES), F32)],
        grid=(st.tiles,),
        in_specs=in_specs,
        out_specs=[tok(d), pl.BlockSpec((SORT_ROWS, PACKED), lambda i: (i, 0)), tok(LANES), tok(LANES),
                   pl.BlockSpec((SUBLANES, TAB_COLS), lambda i: (i, 0)), _full((1, LANES))],
        scratch_shapes=[pltpu.VMEM((tm + 2 * CONV_HALO, CONV_DIM), F32),
                        pltpu.VMEM((1, LANES), F32)],
        compiler_params=pltpu.CompilerParams(dimension_semantics=("arbitrary",), vmem_limit_bytes=VMEM_LIMIT),
        name="postmix",
    )(*args)


ZERO_SPANS = tuple(1 << b for b in reversed(range((EXPERT_ROWS // PIECE).bit_length() - 1)))


def _for_each_piece(pstart_ref, tab_ref, fn):
    def one(p):
        e = tab_ref[0, LANES + p]
        local_row = p * PIECE
        sorted_row = pstart_ref[e] + tab_ref[2, e] + (local_row - tab_ref[0, e])
        fn(pl.multiple_of(local_row, PIECE), pl.multiple_of(sorted_row, PIECE))

    n = tab_ref[3, 0]
    groups = n // PIECE_UNROLL

    def group(g, c):
        for u in range(PIECE_UNROLL):
            one(g * PIECE_UNROLL + u)
        return c

    lax.fori_loop(0, groups, group, 0)
    lax.fori_loop(groups * PIECE_UNROLL, n, lambda p, c: (one(p), c)[1], 0)


def _wait_pieces(n_pieces, src_ref, dst_ref, sem):
    @pl.when(n_pieces > 0)
    def _():
        rows = n_pieces * PIECE
        pltpu.make_async_copy(src_ref.at[pl.ds(0, rows)], dst_ref.at[pl.ds(0, rows)], sem).wait()


def _to_sorted_kernel(pstart_ref, gap_ref, tail_ref, tab_ref, xs_ref, dst_ref, zeros_ref, sem, zsem):
    i = pl.program_id(0)

    def piece_copy(local_row, sorted_row):
        return pltpu.make_async_copy(xs_ref.at[pl.ds(local_row, PIECE)], dst_ref.at[pl.ds(sorted_row, PIECE)], sem)

    _for_each_piece(pstart_ref, tab_ref, lambda a, g: piece_copy(a, g).start())
    _wait_pieces(tab_ref[3, 0], xs_ref, dst_ref, sem)

    @pl.when(i == 0)
    def _():
        zeros_ref[...] = jnp.zeros_like(zeros_ref)

        def zero_copy(row, span):
            r = pl.multiple_of(row, PIECE)
            return pltpu.make_async_copy(zeros_ref.at[pl.ds(0, span * PIECE)],
                                         dst_ref.at[pl.ds(r, span * PIECE)], zsem)

        def gaps(e, counts):
            row = gap_ref[0, e]
            g = gap_ref[1, e]
            new = []
            for span, cnt in zip(ZERO_SPANS, counts):
                hit = (g & span) != 0

                @pl.when(hit)
                def _():
                    zero_copy(row, span).start()

                row = row + jnp.where(hit, span * PIECE, 0)
                new.append(cnt + jnp.where(hit, 1, 0))
            return tuple(new)

        counts = lax.fori_loop(0, N_EXPERTS, gaps, (jnp.int32(0),) * len(ZERO_SPANS))
        for span, cnt in zip(ZERO_SPANS, counts):
            lax.fori_loop(0, cnt, lambda p, c: (zero_copy(0, span).wait(), c)[1], 0)

        blk = EXPERT_ROWS // PIECE

        def tail(b, c):
            zero_copy(tail_ref[0] + b * EXPERT_ROWS, blk).start()
            return c

        lax.fori_loop(0, tail_ref[1], tail, 0)
        lax.fori_loop(0, tail_ref[1], lambda p, c: (zero_copy(0, blk).wait(), c)[1], 0)


def _to_sorted(pstart, gaps, tail, tab, xs_local, n_rows):
    tiles = tab.shape[0] // SUBLANES
    return pl.pallas_call(
        _to_sorted_kernel,
        out_shape=jax.ShapeDtypeStruct((n_rows, PACKED), U32),
        grid_spec=pltpu.PrefetchScalarGridSpec(
            num_scalar_prefetch=3,
            grid=(tiles,),
            in_specs=[pl.BlockSpec((SUBLANES, TAB_COLS), lambda i, *_: (i, 0), memory_space=pltpu.SMEM),
                      pl.BlockSpec((SORT_ROWS, PACKED), lambda i, *_: (i, 0))],
            out_specs=pl.BlockSpec(memory_space=pl.ANY),
            scratch_shapes=[pltpu.VMEM((EXPERT_ROWS, PACKED), U32),
                            pltpu.SemaphoreType.DMA, pltpu.SemaphoreType.DMA]),
        compiler_params=pltpu.CompilerParams(dimension_semantics=("arbitrary",), vmem_limit_bytes=VMEM_LIMIT),
        name="moe_to_sorted",
    )(pstart, gaps, tail, tab, xs_local)


def _experts_kernel(be_ref, bi_ref, nu_ref, xs_ref, wg_ref, wu_ref, wd_ref, ys_ref, wgb_ref, wub_ref, wdb_ref):
    j = pl.program_id(0)
    used = j < nu_ref[0]

    @pl.when(jnp.logical_or(j == 0, be_ref[j] != be_ref[jnp.maximum(j - 1, 0)]))
    def _():
        wgb_ref[...] = wg_ref[0].astype(BF16)
        wub_ref[...] = wu_ref[0].astype(BF16)
        wdb_ref[...] = wd_ref[0].astype(BF16)

    @pl.when(used)
    def _():
        x = _unpack_rows(xs_ref[...])
        a = (_silu(_dot(x, wgb_ref[...])) * _dot(x, wub_ref[...])).astype(BF16)
        ys_ref[...] = _pack_rows(_dot(a, wdb_ref[...]))

    @pl.when(jnp.logical_not(used))
    def _():
        ys_ref[...] = jnp.zeros_like(ys_ref)


def _experts(block_e, block_i, n_used, xs, w_gate, w_up, w_down):
    n_rows = xs.shape[0]
    d = w_gate.shape[1]
    r = EXPERT_ROWS
    return pl.pallas_call(
        _experts_kernel,
        out_shape=jax.ShapeDtypeStruct((n_rows, PACKED), U32),
        grid_spec=pltpu.PrefetchScalarGridSpec(
            num_scalar_prefetch=3,
            grid=(n_rows // r,),
            in_specs=[pl.BlockSpec((r, PACKED), lambda j, be, bi, nu: (bi[j], 0)),
                      pl.BlockSpec((1, d, EXPERT_DIM), lambda j, be, bi, nu: (be[j], 0, 0)),
                      pl.BlockSpec((1, d, EXPERT_DIM), lambda j, be, bi, nu: (be[j], 0, 0)),
                      pl.BlockSpec((1, EXPERT_DIM, d), lambda j, be, bi, nu: (be[j], 0, 0))],
            out_specs=pl.BlockSpec((r, PACKED), lambda j, be, bi, nu: (j, 0)),
            scratch_shapes=[pltpu.VMEM((d, EXPERT_DIM), BF16), pltpu.VMEM((d, EXPERT_DIM), BF16),
                            pltpu.VMEM((EXPERT_DIM, d), BF16)]),
        compiler_params=pltpu.CompilerParams(dimension_semantics=("arbitrary",), vmem_limit_bytes=VMEM_LIMIT),
        name="moe_experts",
    )(block_e, block_i, n_used, xs, w_gate, w_up, w_down)


def _combine_kernel(pstart_ref, tab_ref, tab_next_ref, r0_ref, pos_ref, wts_ref, mod_ref, g2_ref, b2_ref, ys_ref,
                    oc_ref, ol_ref, buf_ref, sem, *, nct):
    tm, d = r0_ref.shape
    i = pl.program_id(0)
    slot = i % 2

    def piece_copy(slot_idx, local_row, sorted_row):
        return pltpu.make_async_copy(ys_ref.at[pl.ds(sorted_row, PIECE)],
                                     buf_ref.at[slot_idx, pl.ds(local_row, PIECE)], sem.at[slot_idx])

    @pl.when(i == 0)
    def _():
        buf_ref[...] = jnp.zeros_like(buf_ref)
        _for_each_piece(pstart_ref, tab_ref, lambda a, g: piece_copy(0, a, g).start())

    @pl.when(i + 1 < pl.num_programs(0))
    def _():
        _for_each_piece(pstart_ref, tab_next_ref, lambda a, g: piece_copy(1 - slot, a, g).start())

    _wait_pieces(tab_ref[3, 0], ys_ref, buf_ref.at[slot], sem.at[slot])

    pos = pos_ref[...]
    wts = wts_ref[...]
    col = lax.broadcasted_iota(I32, (tm, SORT_ROWS), 1)
    weight = jnp.zeros((tm, SORT_ROWS), F32)
    for k in range(TOP_K):
        weight = jnp.where(col == pos[:, k:k + 1], wts[:, k:k + 1], weight)
    routed = _dot(weight.astype(BF16), _unpack_rows(buf_ref[slot]))
    gate2 = mod_ref[0][:, 5 * d:6 * d]
    out = _layer_norm(r0_ref[...] + gate2 * routed) * g2_ref[...] + b2_ref[...]
    @pl.when(i < nct)
    def _():
        oc_ref[...] = out

    @pl.when(i >= nct)
    def _():
        ol_ref[...] = out


def _combine(st, pstart, tab, r0, pos, wts, mod_l, ln2_g, ln2_b, ys_sorted):
    n, d = r0.shape
    tm = st.tm
    tok = lambda w: pl.BlockSpec((tm, w), lambda i, *_: (i, 0))
    last = st.tiles - 1
    out_shape = [jax.ShapeDtypeStruct((st.n_ctx, d), F32), jax.ShapeDtypeStruct((n - st.n_ctx, d), F32)]
    out_specs = [pl.BlockSpec((tm, d), lambda i, *_: (jnp.minimum(i, st.nct - 1), 0)),
                 pl.BlockSpec((tm, d), lambda i, *_: (jnp.maximum(i - st.nct, 0), 0))]
    return pl.pallas_call(
        functools.partial(_combine_kernel, nct=st.nct),
        out_shape=out_shape,
        grid_spec=pltpu.PrefetchScalarGridSpec(
            num_scalar_prefetch=1,
            grid=(st.tiles,),
            in_specs=[pl.BlockSpec((SUBLANES, TAB_COLS), lambda i, *_: (i, 0), memory_space=pltpu.SMEM),
                      pl.BlockSpec((SUBLANES, TAB_COLS), lambda i, *_: (jnp.minimum(i + 1, last), 0),
                                   memory_space=pltpu.SMEM),
                      tok(d), tok(LANES), tok(LANES),
                      pl.BlockSpec((1, 1, 6 * d), lambda i, *_: (st.mod_row(i), 0, 0)),
                      _full((1, d)), _full((1, d)),
                      pl.BlockSpec(memory_space=pl.ANY)],
            out_specs=out_specs,
            scratch_shapes=[pltpu.VMEM((2, SORT_ROWS, PACKED), U32), pltpu.SemaphoreType.DMA((2,))]),
        compiler_params=pltpu.CompilerParams(dimension_semantics=("arbitrary",), vmem_limit_bytes=VMEM_LIMIT),
        name="moe_combine",
    )(pstart, tab, tab, r0, pos, wts, mod_l, ln2_g, ln2_b, ys_sorted)


def _moe_routed(st, xs_local, tab, counts, lw):
    r = EXPERT_ROWS
    n_blocks = -(-(st.tiles * (st.tm * TOP_K + N_EXPERTS * (PIECE - 1))) // r) + N_EXPERTS
    cnt = counts[0, :N_EXPERTS].astype(I32)
    padded = (cnt + r - 1) // r * r
    pend = jnp.cumsum(padded)
    pstart = (pend - padded).astype(I32)
    n_used = (pend[-1] // r).astype(I32)
    blocks = jnp.arange(n_blocks, dtype=I32)
    block_i = jnp.minimum(blocks, n_used - 1)
    block_e = jnp.sum((pend[None, :] <= (block_i * r)[:, None]).astype(I32), axis=1)
    block_e = jnp.minimum(block_e, N_EXPERTS - 1).astype(I32)
    gaps = jnp.stack([pstart + cnt, (padded - cnt) // PIECE]).astype(I32)
    tail = jnp.stack([n_used * r, n_blocks - n_used]).astype(I32)
    xs = _to_sorted(pstart, gaps, tail, tab, xs_local, n_blocks * r)
    ys = _experts(block_e, block_i, n_used.reshape(1), xs, lw['exp_w_gate'], lw['exp_w_up'], lw['exp_w_down'])
    return pstart, ys


def _channel_dft():
    j = jnp.arange(FOURIER_DIM)
    same = (j[:, None] // FOURIER_GROUP_DIM) == (j[None, :] // FOURIER_GROUP_DIM)
    prod = ((j[:, None] % FOURIER_GROUP_DIM) * (j[None, :] % FOURIER_GROUP_DIM)) % FOURIER_GROUP_DIM
    ang = (2.0 * jnp.pi / FOURIER_GROUP_DIM) * prod.astype(F32)
    scale = FOURIER_GROUP_DIM ** -0.5
    return (jnp.where(same, jnp.cos(ang) * scale, 0.0).astype(BF16),
            jnp.where(same, jnp.sin(ang) * scale, 0.0).astype(BF16))


def _position_dft(n):
    m = 1 << (n.bit_length() // 2)
    assert n % m == 0
    k = jnp.arange(n, dtype=I32)[None, :]
    ang_hi = (2.0 * jnp.pi / n) * ((jnp.arange(n // m, dtype=I32)[:, None] * m * k) % n).astype(F32)
    ang_lo = (2.0 * jnp.pi / n) * ((jnp.arange(m, dtype=I32)[:, None] * k) % n).astype(F32)
    scale = n ** -0.5
    ch, sh = (jnp.cos(ang_hi) * scale)[:, None, :], (jnp.sin(ang_hi) * scale)[:, None, :]
    cl, sl = jnp.cos(ang_lo)[None, :, :], jnp.sin(ang_lo)[None, :, :]
    return ((ch * cl - sh * sl).reshape(n, n).astype(BF16), (sh * cl + ch * sl).reshape(n, n).astype(BF16))


def _rope_tables(s_ctx, n_lat):
    t = jnp.arange(n_lat)
    inv = ROPE_BASE ** (-jnp.arange(ROPE_HALF, dtype=F32) / ROPE_HALF)
    row_ang = (t // GRID_W).astype(F32)[:, None] * inv[None, :]
    col_ang = (t % GRID_W).astype(F32)[:, None] * inv[None, :]
    cos = jnp.concatenate([jnp.cos(row_ang)] * 2 + [jnp.cos(col_ang)] * 2, axis=1)
    sin = jnp.concatenate([-jnp.sin(row_ang), jnp.sin(row_ang), -jnp.sin(col_ang), jnp.sin(col_ang)], axis=1)
    reps = LANES // HEAD_DIM
    cos = jnp.concatenate([jnp.ones((s_ctx, LANES), F32), jnp.tile(cos, (1, reps))], axis=0)
    sin = jnp.concatenate([jnp.zeros((s_ctx, LANES), F32), jnp.tile(sin, (1, reps))], axis=0)
    return cos, sin


def kernel(x_prompt, x_sample, cache_k, cache_v, c, c_ctx, w_ada, b_ada, w_in, conv_w, conv_b, conv_ln_g, conv_ln_b, w_conv_out, attn_sink, w_attn_out, w_fourier_out, w_out, post_ln_g, post_ln_b, router_w, router_bias, exp_w_gate, exp_w_up, exp_w_down, shared_w_gate, shared_w_up, shared_w_down):
    b_ctx, s_ctx, d = x_prompt.shape
    b_lat, s_lat, _ = x_sample.shape
    depth = w_in.shape[0]
    assert b_lat < MOD_ROWS and d == D_MODEL
    st = _Stream(b_ctx, s_ctx, b_lat, s_lat)
    nc = st.n_ctx

    cond = jnp.concatenate([c, c_ctx[None, :], jnp.zeros((MOD_ROWS - b_lat - 1, d), F32)], axis=0)
    mod = _modulation(cond, w_ada, b_ada)

    cc, sc = _channel_dft()
    cn_ctx, sn_ctx = _position_dft(s_ctx)
    cn_lat, sn_lat = _position_dft(s_lat)
    rope_cos, rope_sin = _rope_tables(s_ctx, s_lat)

    y_ctx = x_prompt.reshape(nc, d)
    y_lat = x_sample.reshape(b_lat * s_lat, d)
    new_k, new_v = [], []
    for l in range(depth):
        rw = jnp.pad(router_w[l], ((0, 0), (0, LANES - N_EXPERTS)))
        rw_hi = rw.astype(BF16)
        row = lambda a: a.reshape(1, -1)
        lw = dict(
            w_gates=w_in[l][:, PRE_COLS:].astype(BF16),
            conv_w=conv_w[l], conv_b=row(conv_b[l]), conv_ln_g=row(conv_ln_g[l]), conv_ln_b=row(conv_ln_b[l]),
            w_conv_out=w_conv_out[l].astype(BF16), w_attn_out=w_attn_out[l].astype(BF16),
            w_fourier_out=w_fourier_out[l].astype(BF16), w_out=w_out[l].astype(BF16),
            ln1_g=row(post_ln_g[l, 0]), ln1_b=row(post_ln_b[l, 0]),
            router_hi=rw_hi, router_lo=(rw - rw_hi.astype(F32)).astype(BF16),
            router_bias=jnp.pad(row(router_bias[l]), ((0, 0), (0, LANES - N_EXPERTS)), constant_values=-jnp.inf),
            exp_w_gate=exp_w_gate[l], exp_w_up=exp_w_up[l], exp_w_down=exp_w_down[l],
            shared_w_gate=shared_w_gate[l].astype(BF16), shared_w_up=shared_w_up[l].astype(BF16),
            shared_w_down=shared_w_down[l].astype(BF16))
        mod_l = mod[l].reshape(MOD_ROWS, 1, 6 * d)
        sink = attn_sink[l]

        z, q_c, q_l, k_c, k_l, v_c, v_l, uc_c, uc_l, us_c, us_l = _premix(
            st, y_ctx, y_lat, mod_l, w_in[l][:, :PRE_COLS].astype(BF16), cc, sc, rope_cos, rope_sin)
        new_k.append(k_c.reshape(b_ctx, s_ctx, N_KV_HEADS, HEAD_DIM))
        new_v.append(v_c.reshape(b_ctx, s_ctx, N_KV_HEADS, HEAD_DIM))
        attn_c = _ctx_attention(sink, q_c, k_c, v_c, b_ctx, s_ctx)
        attn_l = _lat_attention(sink, q_l, k_l, v_l, cache_k[:, l].reshape(-1, KV_DIM),
                                cache_v[:, l].reshape(-1, KV_DIM), b_lat, s_lat)
        four_c = _fourier_positions(cn_ctx, sn_ctx, uc_c, us_c, b_ctx)
        four_l = _fourier_positions(cn_lat, sn_lat, uc_l, us_l, b_lat)
        r0, xs_local, pos, wts, tab, counts = _postmix(st, y_ctx, y_lat, mod_l, z, attn_c, attn_l, four_c, four_l, lw)
        pstart, ys_sorted = _moe_routed(st, xs_local, tab, counts, lw)
        y_ctx, y_lat = _combine(st, pstart, tab, r0, pos, wts, mod_l, row(post_ln_g[l, 1]), row(post_ln_b[l, 1]),
                                ys_sorted)

    return (y_ctx.reshape(b_ctx, s_ctx, d), y_lat.reshape(b_lat, s_lat, d),
            jnp.stack(new_k, axis=1), jnp.stack(new_v, axis=1))
```

```python
import functools

import jax
import jax.numpy as jnp
from jax import lax
from jax.experimental import pallas as pl
from jax.experimental.pallas import tpu as pltpu

F32 = jnp.float32
BF16 = jnp.bfloat16
I32 = jnp.int32
U32 = jnp.uint32

D_MODEL = 1024
DEPTH = 2
GRID_W = 64
CONV_DIM = 256
CONV_WIDTH = 31
CONV_PAD = CONV_WIDTH // 2
CONV_HALO = 16
N_HEADS = 8
N_KV_HEADS = 2
HEAD_DIM = 64
Q_PER_KV = N_HEADS // N_KV_HEADS
Q_DIM = N_HEADS * HEAD_DIM
KV_DIM = N_KV_HEADS * HEAD_DIM
WINDOW = 128
ATTN_BLOCK = 128
ATTN_SCALE = HEAD_DIM ** -0.5
ROPE_AXIS_DIM = HEAD_DIM // 2
ROPE_HALF = ROPE_AXIS_DIM // 2
ROPE_BASE = 10000.0
NEG_INF = -1e30
FOURIER_GROUPS = 4
FOURIER_GROUP_DIM = 64
FOURIER_DIM = FOURIER_GROUPS * FOURIER_GROUP_DIM
PRE_COLS = 2 * CONV_DIM + Q_DIM + 2 * KV_DIM + FOURIER_DIM
N_EXPERTS = 64
TOP_K = 6
EXPERT_DIM = 256
ROUTED_SCALE = 2.5
LN_EPS = 1e-6
DEEPNORM_ALPHA = (2 * DEPTH) ** 0.25

LANES = 128
SUBLANES = 8
MOD_ROWS = 16
TOKEN_TILE = 256
EXPERT_ROWS = 512
PIECE = SUBLANES
SORT_ROWS = -(-(TOKEN_TILE * TOP_K + N_EXPERTS * (PIECE - 1)) // 256) * 256
SORT_PIECES = SORT_ROWS // PIECE
TAB_COLS = LANES + SORT_PIECES
ATTN_STACK = Q_PER_KV
TAB_SIZE = SUBLANES * TAB_COLS
PIECE_UNROLL = 4
PACKED = D_MODEL // 2
VMEM_LIMIT = 56 * 1024 * 1024


def _layer_norm(x):
    mu = jnp.mean(x, axis=-1, keepdims=True)
    xc = x - mu
    var = jnp.mean(xc * xc, axis=-1, keepdims=True)
    return xc * lax.rsqrt(var + LN_EPS)


def _silu(x):
    return x * jax.nn.sigmoid(x)


def _dot(a, b):
    return jnp.dot(a, b, preferred_element_type=F32)


def _full(shape):
    return pl.BlockSpec(shape, lambda *_: (0,) * len(shape), pipeline_mode=pl.Buffered(1))


def _pack_rows(y, rounded=False):
    bits = lax.bitcast_convert_type(y if rounded else y.astype(BF16).astype(F32), U32)
    return (bits[:, :PACKED] >> 16) | bits[:, PACKED:]


def _unpack_rows(u):
    lo = lax.bitcast_convert_type(u << 16, F32)
    hi = lax.bitcast_convert_type(u & jnp.uint32(0xFFFF0000), F32)
    return jnp.concatenate([lo, hi], axis=1).astype(BF16)


class _Stream:
    def __init__(self, b_ctx, s_ctx, b_lat, s_lat):
        self.tm = TOKEN_TILE
        assert s_ctx % self.tm == 0 and s_lat % self.tm == 0
        self.b_ctx, self.s_ctx, self.b_lat, self.s_lat = b_ctx, s_ctx, b_lat, s_lat
        self.n_ctx = b_ctx * s_ctx
        self.n = self.n_ctx + b_lat * s_lat
        self.tpb_c = s_ctx // self.tm
        self.tpb_l = s_lat // self.tm
        self.nct = self.n_ctx // self.tm
        self.tiles = self.n // self.tm
        self.ctx_row = b_lat

    def mod_row(self, i):
        return jnp.where(i >= self.nct, (i - self.nct) // self.tpb_l, self.ctx_row)

    def tile_in_seq(self, i):
        return jnp.where(i >= self.nct, (i - self.nct) % self.tpb_l, i % self.tpb_c)

    def tiles_per_seq(self, i):
        return jnp.where(i >= self.nct, self.tpb_l, self.tpb_c)

    def rope_block(self, i):
        return jnp.where(i >= self.nct, self.tpb_c + (i - self.nct) % self.tpb_l, i % self.tpb_c)


def _mod_kernel(c_ref, w_ref, b_ref, o_ref):
    o_ref[0] = _dot(_silu(c_ref[...]), w_ref[0]) + b_ref[0]


def _modulation(cond, w_ada, b_ada):
    depth, d, cols = w_ada.shape
    cb = cols // 4
    return pl.pallas_call(
        _mod_kernel,
        out_shape=jax.ShapeDtypeStruct((depth, MOD_ROWS, cols), F32),
        grid=(depth, cols // cb),
        in_specs=[pl.BlockSpec((MOD_ROWS, d), lambda l, j: (0, 0)),
                  pl.BlockSpec((1, d, cb), lambda l, j: (l, 0, j)),
                  pl.BlockSpec((1, 1, cb), lambda l, j: (l, 0, j))],
        out_specs=pl.BlockSpec((1, MOD_ROWS, cb), lambda l, j: (l, 0, j)),
        compiler_params=pltpu.CompilerParams(vmem_limit_bytes=VMEM_LIMIT),
        name="modulation",
    )(cond, w_ada, b_ada.reshape(depth, 1, cols))


def _premix_kernel(xc_ref, xl_ref, mod_ref, w_ref, cc_ref, sc_ref, cos_ref, sin_ref, z_ref, *outs, nct):
    d = D_MODEL
    is_lat = pl.program_id(0) >= nct
    mod = mod_ref[0]
    h = _layer_norm(jnp.where(is_lat, xl_ref[...], xc_ref[...])) * (1.0 + mod[:, d:2 * d]) + mod[:, 0:d]
    p = _dot(h.astype(BF16), w_ref[...])
    c0 = CONV_DIM
    z_ref[...] = (p[:, 0:c0] * jax.nn.sigmoid(p[:, c0:2 * c0])).astype(z_ref.dtype)
    q0 = 2 * c0
    k0 = q0 + Q_DIM
    v0 = k0 + KV_DIM
    f0 = v0 + KV_DIM
    cos = cos_ref[...]
    sin = sin_ref[...]
    first = (lax.broadcasted_iota(I32, cos.shape, 1) % ROPE_AXIS_DIM) < ROPE_HALF

    def rot(t):
        sw = jnp.where(first, pltpu.roll(t, LANES - ROPE_HALF, 1), pltpu.roll(t, ROPE_HALF, 1))
        return t * cos + sw * sin

    q = p[:, q0:k0] * ATTN_SCALE
    q = jnp.concatenate([rot(q[:, j * LANES:(j + 1) * LANES]) for j in range(Q_DIM // LANES)], axis=1)
    uf = p[:, f0:f0 + FOURIER_DIM].astype(BF16)
    results = [q.astype(BF16), rot(p[:, k0:v0]), p[:, v0:f0],
               _dot(uf, cc_ref[...]).astype(BF16), _dot(uf, sc_ref[...]).astype(BF16)]

    @pl.when(jnp.logical_not(is_lat))
    def _():
        for r, o_ref in zip(results, outs[0::2]):
            o_ref[...] = r

    @pl.when(is_lat)
    def _():
        for r, o_ref in zip(results, outs[1::2]):
            o_ref[...] = r


def _premix(st, x_ctx, x_lat, mod_l, w_pre, cc, sc, rope_cos, rope_sin):
    d = x_ctx.shape[1]
    tm = st.tm
    tok = lambda w: pl.BlockSpec((tm, w), lambda i: (i, 0))
    ctx_tok = lambda w: pl.BlockSpec((tm, w), lambda i: (jnp.minimum(i, st.nct - 1), 0))
    lat_tok = lambda w: pl.BlockSpec((tm, w), lambda i: (jnp.maximum(i - st.nct, 0), 0))
    rope = pl.BlockSpec((tm, LANES), lambda i: (st.rope_block(i), 0))
    n_lat = st.n - st.n_ctx
    out_shape = [jax.ShapeDtypeStruct((st.n, CONV_DIM), BF16)]
    out_specs = [tok(CONV_DIM)]
    for w, dt in ((Q_DIM, BF16), (KV_DIM, F32), (KV_DIM, F32), (FOURIER_DIM, BF16), (FOURIER_DIM, BF16)):
        out_shape += [jax.ShapeDtypeStruct((st.n_ctx, w), dt), jax.ShapeDtypeStruct((n_lat, w), dt)]
        out_specs += [ctx_tok(w), lat_tok(w)]
    return pl.pallas_call(
        functools.partial(_premix_kernel, nct=st.nct),
        out_shape=out_shape,
        grid=(st.tiles,),
        in_specs=[ctx_tok(d), lat_tok(d), pl.BlockSpec((1, 1, 6 * d), lambda i: (st.mod_row(i), 0, 0)),
                  _full(w_pre.shape), _full(cc.shape), _full(sc.shape), rope, rope],
        out_specs=out_specs,
        compiler_params=pltpu.CompilerParams(dimension_semantics=("arbitrary",), vmem_limit_bytes=VMEM_LIMIT),
        name="premix",
    )(x_ctx, x_lat, mod_l, w_pre, cc, sc, rope_cos, rope_sin)


def _sink_softmax_pv(s, sink_col, v):
    m = jnp.maximum(jnp.max(s, axis=1, keepdims=True), sink_col)
    p = jnp.exp(s - m)
    denom = jnp.sum(p, axis=1, keepdims=True) + jnp.exp(sink_col - m)
    return _dot(p.astype(BF16), v) / denom


def _attend_heads(sink_ref, q_ref, keys, vals, bias, o_ref):
    rows = q_ref.shape[0]
    for first in range(0, N_HEADS, ATTN_STACK):
        h = first // Q_PER_KV
        heads = list(range(first, first + ATTN_STACK))
        q = jnp.concatenate([q_ref[:, hd * HEAD_DIM:(hd + 1) * HEAD_DIM] for hd in heads], axis=0)
        s = lax.dot_general(q, keys[:, h * HEAD_DIM:(h + 1) * HEAD_DIM], (((1,), (1,)), ((), ())),
                            preferred_element_type=F32)
        if bias is not None:
            s = s + bias
        sink_col = jnp.concatenate([jnp.full((rows, 1), sink_ref[hd], F32) for hd in heads], axis=0)
        o = _sink_softmax_pv(s, sink_col, vals[:, h * HEAD_DIM:(h + 1) * HEAD_DIM])
        for g, hd in enumerate(heads):
            o_ref[:, hd * HEAD_DIM:(hd + 1) * HEAD_DIM] = o[g * rows:(g + 1) * rows].astype(o_ref.dtype)


def _ctx_attn_kernel(sink_ref, q_ref, k_ref, v_ref, o_ref):
    _attend_heads(sink_ref, q_ref, k_ref[...].astype(BF16), v_ref[...].astype(BF16), None, o_ref)


def _ctx_attention(sink, q, k, v, batch, seq):
    return pl.pallas_call(
        _ctx_attn_kernel,
        out_shape=jax.ShapeDtypeStruct((batch * seq, Q_DIM), BF16),
        grid=(batch,),
        in_specs=[pl.BlockSpec(memory_space=pltpu.SMEM),
                  pl.BlockSpec((seq, Q_DIM), lambda i: (i, 0)),
                  pl.BlockSpec((seq, KV_DIM), lambda i: (i, 0)),
                  pl.BlockSpec((seq, KV_DIM), lambda i: (i, 0))],
        out_specs=pl.BlockSpec((seq, Q_DIM), lambda i: (i, 0)),
        compiler_params=pltpu.CompilerParams(vmem_limit_bytes=VMEM_LIMIT),
        name="attn_ctx",
    )(sink, q, k, v)


def _lat_attn_kernel(sink_ref, q_ref, k_ref, v_ref, kc_ref, vc_ref, bias_ref, o_ref):
    i = pl.program_id(1)
    nblk = pl.num_programs(1)
    blk = ATTN_BLOCK
    starts = [pl.multiple_of(jnp.maximum(i - 1, 0) * blk, blk),
              pl.multiple_of(i * blk, blk),
              pl.multiple_of(jnp.minimum(i + 1, nblk - 1) * blk, blk)]
    keys = jnp.concatenate([k_ref[pl.ds(st, blk), :] for st in starts] + [kc_ref[...]], axis=0).astype(BF16)
    vals = jnp.concatenate([v_ref[pl.ds(st, blk), :] for st in starts] + [vc_ref[...]], axis=0).astype(BF16)
    _attend_heads(sink_ref, q_ref, keys, vals, bias_ref[0], o_ref)


def _window_bias(n_ctx_keys):
    blk = ATTN_BLOCK
    r = jnp.arange(blk)[:, None]
    j = jnp.arange(3 * blk)[None, :]
    band = (j >= r) & (j <= r + 2 * WINDOW)
    has_prev = j >= blk
    has_next = j < 2 * blk
    variants = [band & has_prev, band, band & has_next]
    out = []
    for m in variants:
        full = jnp.concatenate([m, jnp.ones((blk, n_ctx_keys), bool)], axis=1)
        out.append(jnp.tile(jnp.where(full, 0.0, NEG_INF).astype(F32), (ATTN_STACK, 1)))
    return jnp.stack(out)


def _lat_attention(sink, q, k, v, kc, vc, batch, seq):
    blk = ATTN_BLOCK
    nblk = seq // blk
    assert nblk >= 2 and WINDOW == blk
    pc = kc.shape[0] // batch
    bias = _window_bias(pc)
    nkeys = 3 * blk + pc
    bias_map = lambda i, j: (jnp.where(j == 0, 0, jnp.where(j == nblk - 1, 2, 1)), 0, 0)
    return pl.pallas_call(
        _lat_attn_kernel,
        out_shape=jax.ShapeDtypeStruct((batch * seq, Q_DIM), BF16),
        grid=(batch, nblk),
        in_specs=[pl.BlockSpec(memory_space=pltpu.SMEM),
                  pl.BlockSpec((blk, Q_DIM), lambda i, j: (i * nblk + j, 0)),
                  pl.BlockSpec((seq, KV_DIM), lambda i, j: (i, 0)),
                  pl.BlockSpec((seq, KV_DIM), lambda i, j: (i, 0)),
                  pl.BlockSpec((pc, KV_DIM), lambda i, j: (i, 0)),
                  pl.BlockSpec((pc, KV_DIM), lambda i, j: (i, 0)),
                  pl.BlockSpec((1, ATTN_STACK * blk, nkeys), bias_map)],
        out_specs=pl.BlockSpec((blk, Q_DIM), lambda i, j: (i * nblk + j, 0)),
        compiler_params=pltpu.CompilerParams(vmem_limit_bytes=VMEM_LIMIT),
        name="attn_lat",
    )(sink, q, k, v, kc, vc, bias)


def _fourier_kernel(cn_ref, sn_ref, uc_ref, us_ref, o_ref):
    o_ref[...] = (_dot(cn_ref[...], uc_ref[...]) - _dot(sn_ref[...], us_ref[...])).astype(o_ref.dtype)


def _fourier_positions(cn, sn, uc, us, nb):
    seq = cn.shape[0]
    tf = min(512, seq)
    jt = seq // tf
    return pl.pallas_call(
        _fourier_kernel,
        out_shape=jax.ShapeDtypeStruct((nb * seq, FOURIER_DIM), BF16),
        grid=(jt, nb),
        in_specs=[pl.BlockSpec((tf, seq), lambda j, b: (j, 0)),
                  pl.BlockSpec((tf, seq), lambda j, b: (j, 0)),
                  pl.BlockSpec((seq, FOURIER_DIM), lambda j, b: (b, 0)),
                  pl.BlockSpec((seq, FOURIER_DIM), lambda j, b: (b, 0))],
        out_specs=pl.BlockSpec((tf, FOURIER_DIM), lambda j, b: (b * jt + j, 0)),
        compiler_params=pltpu.CompilerParams(vmem_limit_bytes=VMEM_LIMIT),
        name="fourier",
    )(cn, sn, uc, us)


def _postmix_kernel(xc_ref, xl_ref, mod_ref, z_ref, zp_ref, zn_ref, attn_c_ref, attn_l_ref, four_c_ref, four_l_ref,
                    wg_ref, cw_ref, cb_ref, cg_ref,
                    cbeta_ref, wco_ref, wao_ref, wfo_ref, wo_ref, g1_ref, b1_ref, rwh_ref, rwl_ref, rb_ref,
                    sg_ref, su_ref, sd_ref,
                    r0_ref, xs_ref, pos_ref, wts_ref, tab_ref, cnt_ref, zs_ref, carry_ref, *, st):
    d = D_MODEL
    i = pl.program_id(0)
    tm = xc_ref.shape[0]
    is_lat = i >= st.nct
    attn = jnp.where(is_lat, attn_l_ref[...], attn_c_ref[...])
    four = jnp.where(is_lat, four_l_ref[...], four_c_ref[...])
    x = jnp.where(is_lat, xl_ref[...], xc_ref[...])
    mod = mod_ref[0]
    shift1, scale1, gate1, shift2, scale2, gate2 = [mod[:, j * d:(j + 1) * d] for j in range(6)]
    h = (_layer_norm(x) * (1.0 + scale1) + shift1).astype(BF16)
    gates = jax.nn.sigmoid(_dot(h, wg_ref[...]))

    tib = st.tile_in_seq(i)
    zs_ref[0:CONV_HALO, :] = zp_ref[...].astype(F32) * jnp.where(tib != 0, 1.0, 0.0)
    zs_ref[CONV_HALO + tm:, :] = zn_ref[...].astype(F32) * jnp.where(tib != st.tiles_per_seq(i) - 1, 1.0, 0.0)
    zs_ref[CONV_HALO:CONV_HALO + tm, :] = z_ref[...].astype(F32)
    acc = jnp.broadcast_to(cb_ref[...], (tm, CONV_DIM))
    first_off = CONV_HALO - CONV_PAD
    window = zs_ref[...]
    rows = window.shape[0]
    for r in range(SUBLANES):
        offs = [o for o in range(first_off, first_off + CONV_WIDTH) if o % SUBLANES == r]
        shifted = window if r == 0 else pltpu.roll(window, rows - r, 0)
        for o in offs:
            acc = acc + cw_ref[o - first_off:o - first_off + 1, :] * shifted[o - r:o - r + tm]
    conv = _silu(_layer_norm(acc) * cg_ref[...] + cbeta_ref[...]).astype(BF16)

    merged = (gates[:, 0:d] * _dot(conv, wco_ref[...])
              + gates[:, d:2 * d] * _dot(attn, wao_ref[...])
              + gates[:, 2 * d:3 * d] * _dot(four, wfo_ref[...]))
    mix = _dot(merged.astype(BF16), wo_ref[...])
    x1 = _layer_norm(DEEPNORM_ALPHA * x + gate1 * mix) * g1_ref[...] + b1_ref[...]
    h2 = _layer_norm(x1) * (1.0 + scale2) + shift2

    h2_hi = h2.astype(BF16)
    h2_lo = (h2 - h2_hi.astype(F32)).astype(BF16)
    nt = (((1,), (1,)), ((), ()))
    logits_t = (lax.dot_general(rwh_ref[...], h2_hi, nt, preferred_element_type=F32)
                + lax.dot_general(rwh_ref[...], h2_lo, nt, preferred_element_type=F32)
                + lax.dot_general(rwl_ref[...], h2_hi, nt, preferred_element_type=F32))
    scores_t = jax.nn.sigmoid(logits_t)
    sel_t = scores_t + rb_ref[...]
    expert_t = lax.broadcasted_iota(I32, (LANES, tm), 0)
    onehots_t, top_s = [], []
    for _ in range(TOP_K):
        mx = jnp.max(sel_t, axis=0, keepdims=True)
        idx = jnp.min(jnp.where(sel_t == mx, expert_t, LANES), axis=0, keepdims=True)
        oh = expert_t == idx
        onehots_t.append(oh)
        top_s.append(jnp.sum(jnp.where(oh, scores_t, 0.0), axis=0, keepdims=True))
        sel_t = jnp.where(oh, -jnp.inf, sel_t)
    total = top_s[0]
    for s_k in top_s[1:]:
        total = total + s_k

    @pl.when(i == 0)
    def _():
        carry_ref[...] = jnp.zeros_like(carry_ref)

    chosen_t = jnp.zeros((LANES, tm), F32)
    for oh in onehots_t:
        chosen_t = jnp.where(oh, 1.0, chosen_t)
    chosen_tb = chosen_t.astype(BF16)
    tok_r = lax.broadcasted_iota(I32, (tm, tm), 0)
    tok_c = lax.broadcasted_iota(I32, (tm, tm), 1)
    prefix_t = _dot(chosen_tb, jnp.where(tok_r < tok_c, 1.0, 0.0).astype(BF16))
    cpad_t = jnp.floor((jnp.sum(chosen_t, axis=1, keepdims=True) + (PIECE - 1)) * (1.0 / PIECE)) * PIECE
    exp_r = lax.broadcasted_iota(I32, (LANES, LANES), 0)
    exp_c = lax.broadcasted_iota(I32, (LANES, LANES), 1)
    cstart_t = _dot(jnp.where(exp_c < exp_r, 1.0, 0.0).astype(BF16),
                    jnp.broadcast_to(cpad_t, (LANES, tm)).astype(BF16))

    where_t = prefix_t + cstart_t
    krow = lax.broadcasted_iota(I32, (LANES, tm), 0)
    pos_t = jnp.zeros((LANES, tm), F32)
    wts_t = jnp.zeros((LANES, tm), F32)
    for k, oh in enumerate(onehots_t):
        pos_t = jnp.where(krow == k, jnp.sum(jnp.where(oh, where_t, 0.0), axis=0, keepdims=True), pos_t)
        wts_t = jnp.where(krow == k, ROUTED_SCALE * top_s[k] / total, wts_t)
    pos_ref[...] = pos_t.T.astype(I32)
    wts_ref[...] = wts_t.T

    chosen = chosen_t.T
    count = jnp.sum(chosen, axis=0, keepdims=True)
    cpad = jnp.floor((count + (PIECE - 1)) * (1.0 / PIECE)) * PIECE
    er = lax.broadcasted_iota(I32, (LANES, LANES), 0)
    ec = lax.broadcasted_iota(I32, (LANES, LANES), 1)
    cstart = _dot(jnp.broadcast_to(cpad, (SUBLANES, LANES)), jnp.where(er < ec, 1.0, 0.0))[0:1]
    gbase = carry_ref[...]
    carry_ref[...] = gbase + cpad
    cnt_ref[...] = gbase + cpad
    pieces = jnp.sum(cpad, axis=1, keepdims=True) * (1.0 / PIECE)
    trow = lax.broadcasted_iota(I32, (SUBLANES, LANES), 0)
    per_expert = jnp.where(trow == 0, cstart, jnp.where(trow == 1, cpad, jnp.where(trow == 2, gbase, pieces)))
    piece_row = (lax.broadcasted_iota(I32, (SORT_PIECES, LANES), 0) * PIECE).astype(F32)
    owns = jnp.where((piece_row >= cstart) & (piece_row < cstart + cpad), 1.0, 0.0).astype(BF16)
    expert_id = lax.broadcasted_iota(I32, (SUBLANES, LANES), 1).astype(BF16)
    piece_expert = lax.dot_general(expert_id, owns, (((1,), (1,)), ((), ())), preferred_element_type=F32)
    tab_ref[...] = jnp.concatenate([per_expert, piece_expert], axis=1).astype(I32)

    slot_e = lax.broadcasted_iota(I32, (SORT_ROWS, LANES), 0).astype(F32)
    slot_expert = jnp.where((slot_e >= cstart) & (slot_e < cstart + cpad), 1.0, 0.0).astype(BF16)
    rank_t = jnp.where(chosen_t > 0.0, prefix_t, -1.0)
    looked = _dot(slot_expert, jnp.concatenate([rank_t.astype(BF16), cstart_t.astype(BF16)], axis=1))
    slot_t = lax.broadcasted_iota(I32, (SORT_ROWS, tm), 0).astype(F32)
    perm = jnp.where(looked[:, :tm] + looked[:, tm:] == slot_t, 1.0, 0.0).astype(BF16)
    h2b = h2.astype(BF16)
    xs_ref[...] = _pack_rows(_dot(perm, h2b), rounded=True)

    shared = _dot((_silu(_dot(h2b, sg_ref[...])) * _dot(h2b, su_ref[...])).astype(BF16), sd_ref[...])
    r0_ref[...] = DEEPNORM_ALPHA * x1 + gate2 * shared


def _postmix(st, x_ctx, x_lat, mod_l, z, attn_c, attn_l, four_c, four_l, lw):
    n, d = st.n, x_ctx.shape[1]
    tm = st.tm
    hb = tm // CONV_HALO
    last_halo = n // CONV_HALO - 1
    tok = lambda w: pl.BlockSpec((tm, w), lambda i: (i, 0))
    ctx_tok = lambda w: pl.BlockSpec((tm, w), lambda i: (jnp.minimum(i, st.nct - 1), 0))
    lat_tok = lambda w: pl.BlockSpec((tm, w), lambda i: (jnp.maximum(i - st.nct, 0), 0))
    in_specs = [ctx_tok(d), lat_tok(d), pl.BlockSpec((1, 1, 6 * d), lambda i: (st.mod_row(i), 0, 0)), tok(CONV_DIM),
                pl.BlockSpec((CONV_HALO, CONV_DIM), lambda i: (jnp.maximum(i * hb - 1, 0), 0)),
                pl.BlockSpec((CONV_HALO, CONV_DIM), lambda i: (jnp.minimum((i + 1) * hb, last_halo), 0)),
                ctx_tok(Q_DIM), lat_tok(Q_DIM), ctx_tok(FOURIER_DIM), lat_tok(FOURIER_DIM)]
    args = [x_ctx, x_lat, mod_l, z, z, z, attn_c, attn_l, four_c, four_l]
    weights = [lw['w_gates'], lw['conv_w'], lw['conv_b'], lw['conv_ln_g'], lw['conv_ln_b'], lw['w_conv_out'],
               lw['w_attn_out'], lw['w_fourier_out'], lw['w_out'], lw['ln1_g'], lw['ln1_b'],
               lw['router_hi'], lw['router_lo'], lw['router_bias'],
               lw['shared_w_gate'], lw['shared_w_up'], lw['shared_w_down']]
    in_specs += [_full(w.shape) for w in weights]
    args += weights
    return pl.pallas_call(
        functools.partial(_postmix_kernel, st=st),
        out_shape=[jax.ShapeDtypeStruct((n, d), F32),
                   jax.ShapeDtypeStruct((st.tiles * SORT_ROWS, PACKED), U32),
                   jax.ShapeDtypeStruct((n, LANES), I32),
                   jax.ShapeDtypeStruct((n, LANES), F32),
                   jax.ShapeDtypeStruct((st.tiles * SUBLANES, TAB_COLS), I32),
                   jax.ShapeDtypeStruct((1, LANES), F32)],
        grid=(st.tiles,),
        in_specs=in_specs,
        out_specs=[tok(d), pl.BlockSpec((SORT_ROWS, PACKED), lambda i: (i, 0)), tok(LANES), tok(LANES),
                   pl.BlockSpec((SUBLANES, TAB_COLS), lambda i: (i, 0)), _full((1, LANES))],
        scratch_shapes=[pltpu.VMEM((tm + 2 * CONV_HALO, CONV_DIM), F32),
                        pltpu.VMEM((1, LANES), F32)],
        compiler_params=pltpu.CompilerParams(dimension_semantics=("arbitrary",), vmem_limit_bytes=VMEM_LIMIT),
        name="postmix",
    )(*args)


ZERO_SPANS = tuple(1 << b for b in reversed(range((EXPERT_ROWS // PIECE).bit_length() - 1)))


def _tab(tab_ref, r, c):
    return tab_ref[r * TAB_COLS + c]


def _for_each_piece(pstart_ref, tab_ref, fn):
    def one(p):
        e = _tab(tab_ref, 0, LANES + p)
        local_row = p * PIECE
        sorted_row = pstart_ref[e] + _tab(tab_ref, 2, e) + (local_row - _tab(tab_ref, 0, e))
        fn(pl.multiple_of(local_row, PIECE), pl.multiple_of(sorted_row, PIECE))

    n = _tab(tab_ref, 3, 0)
    groups = n // PIECE_UNROLL

    def group(g, c):
        for u in range(PIECE_UNROLL):
            one(g * PIECE_UNROLL + u)
        return c

    lax.fori_loop(0, groups, group, 0)
    lax.fori_loop(groups * PIECE_UNROLL, n, lambda p, c: (one(p), c)[1], 0)


def _wait_pieces(n_pieces, src_ref, dst_ref, sem):
    @pl.when(n_pieces > 0)
    def _():
        rows = n_pieces * PIECE
        pltpu.make_async_copy(src_ref.at[pl.ds(0, rows)], dst_ref.at[pl.ds(0, rows)], sem).wait()


def _to_sorted_kernel(pstart_ref, gap_ref, tail_ref, tab_ref, xs_ref, dst_ref, zeros_ref, sem, zsem):
    i = pl.program_id(0)

    def piece_copy(local_row, sorted_row):
        return pltpu.make_async_copy(xs_ref.at[pl.ds(local_row, PIECE)], dst_ref.at[pl.ds(sorted_row, PIECE)], sem)

    _for_each_piece(pstart_ref, tab_ref, lambda a, g: piece_copy(a, g).start())
    _wait_pieces(_tab(tab_ref, 3, 0), xs_ref, dst_ref, sem)

    @pl.when(i == 0)
    def _():
        zeros_ref[...] = jnp.zeros_like(zeros_ref)

        def zero_copy(row, span):
            r = pl.multiple_of(row, PIECE)
            return pltpu.make_async_copy(zeros_ref.at[pl.ds(0, span * PIECE)],
                                         dst_ref.at[pl.ds(r, span * PIECE)], zsem)

        def gaps(e, counts):
            row = gap_ref[0, e]
            g = gap_ref[1, e]
            new = []
            for span, cnt in zip(ZERO_SPANS, counts):
                hit = (g & span) != 0

                @pl.when(hit)
                def _():
                    zero_copy(row, span).start()

                row = row + jnp.where(hit, span * PIECE, 0)
                new.append(cnt + jnp.where(hit, 1, 0))
            return tuple(new)

        counts = lax.fori_loop(0, N_EXPERTS, gaps, (jnp.int32(0),) * len(ZERO_SPANS))
        for span, cnt in zip(ZERO_SPANS, counts):
            lax.fori_loop(0, cnt, lambda p, c: (zero_copy(0, span).wait(), c)[1], 0)

        blk = EXPERT_ROWS // PIECE

        def tail(b, c):
            zero_copy(tail_ref[0] + b * EXPERT_ROWS, blk).start()
            return c

        lax.fori_loop(0, tail_ref[1], tail, 0)
        lax.fori_loop(0, tail_ref[1], lambda p, c: (zero_copy(0, blk).wait(), c)[1], 0)


def _to_sorted(pstart, gaps, tail, tab, xs_local, n_rows):
    tiles = tab.shape[0] // TAB_SIZE
    return pl.pallas_call(
        _to_sorted_kernel,
        out_shape=jax.ShapeDtypeStruct((n_rows, PACKED), U32),
        grid_spec=pltpu.PrefetchScalarGridSpec(
            num_scalar_prefetch=3,
            grid=(tiles,),
            in_specs=[pl.BlockSpec((TAB_SIZE,), lambda i, *_: (i,), memory_space=pltpu.SMEM),
                      pl.BlockSpec((SORT_ROWS, PACKED), lambda i, *_: (i, 0))],
            out_specs=pl.BlockSpec(memory_space=pl.ANY),
            scratch_shapes=[pltpu.VMEM((EXPERT_ROWS, PACKED), U32),
                            pltpu.SemaphoreType.DMA, pltpu.SemaphoreType.DMA]),
        compiler_params=pltpu.CompilerParams(dimension_semantics=("arbitrary",), vmem_limit_bytes=VMEM_LIMIT),
        name="moe_to_sorted",
    )(pstart, gaps, tail, tab, xs_local)


def _experts_kernel(be_ref, bi_ref, nu_ref, xs_ref, wg_ref, wu_ref, wd_ref, ys_ref, wgb_ref, wub_ref, wdb_ref):
    j = pl.program_id(0)
    used = j < nu_ref[0]

    @pl.when(jnp.logical_or(j == 0, be_ref[j] != be_ref[jnp.maximum(j - 1, 0)]))
    def _():
        wgb_ref[...] = wg_ref[0, 0].astype(BF16)
        wub_ref[...] = wu_ref[0, 0].astype(BF16)
        wdb_ref[...] = wd_ref[0, 0].astype(BF16)

    @pl.when(used)
    def _():
        x = _unpack_rows(xs_ref[...])
        a = (_silu(_dot(x, wgb_ref[...])) * _dot(x, wub_ref[...])).astype(BF16)
        ys_ref[...] = _pack_rows(_dot(a, wdb_ref[...]))

    @pl.when(jnp.logical_not(used))
    def _():
        ys_ref[...] = jnp.zeros_like(ys_ref)


def _experts(block_e, block_i, n_used, xs, w_gate, w_up, w_down, layer):
    n_rows = xs.shape[0]
    d = w_gate.shape[2]
    r = EXPERT_ROWS
    return pl.pallas_call(
        _experts_kernel,
        out_shape=jax.ShapeDtypeStruct((n_rows, PACKED), U32),
        grid_spec=pltpu.PrefetchScalarGridSpec(
            num_scalar_prefetch=3,
            grid=(n_rows // r,),
            in_specs=[pl.BlockSpec((r, PACKED), lambda j, be, bi, nu: (bi[j], 0)),
                      pl.BlockSpec((1, 1, d, EXPERT_DIM), lambda j, be, bi, nu: (layer, be[j], 0, 0)),
                      pl.BlockSpec((1, 1, d, EXPERT_DIM), lambda j, be, bi, nu: (layer, be[j], 0, 0)),
                      pl.BlockSpec((1, 1, EXPERT_DIM, d), lambda j, be, bi, nu: (layer, be[j], 0, 0))],
            out_specs=pl.BlockSpec((r, PACKED), lambda j, be, bi, nu: (j, 0)),
            scratch_shapes=[pltpu.VMEM((d, EXPERT_DIM), BF16), pltpu.VMEM((d, EXPERT_DIM), BF16),
                            pltpu.VMEM((EXPERT_DIM, d), BF16)]),
        compiler_params=pltpu.CompilerParams(dimension_semantics=("arbitrary",), vmem_limit_bytes=VMEM_LIMIT),
        name="moe_experts",
    )(block_e, block_i, n_used, xs, w_gate, w_up, w_down)


def _combine_kernel(pstart_ref, tab_ref, tab_next_ref, r0_ref, pos_ref, wts_ref, mod_ref, g2_ref, b2_ref, ys_ref,
                    oc_ref, ol_ref, buf_ref, sem, *, nct):
    tm, d = r0_ref.shape
    i = pl.program_id(0)
    slot = i % 2

    def piece_copy(slot_idx, local_row, sorted_row):
        return pltpu.make_async_copy(ys_ref.at[pl.ds(sorted_row, PIECE)],
                                     buf_ref.at[slot_idx, pl.ds(local_row, PIECE)], sem.at[slot_idx])

    @pl.when(i == 0)
    def _():
        buf_ref[...] = jnp.zeros_like(buf_ref)
        _for_each_piece(pstart_ref, tab_ref, lambda a, g: piece_copy(0, a, g).start())

    @pl.when(i + 1 < pl.num_programs(0))
    def _():
        _for_each_piece(pstart_ref, tab_next_ref, lambda a, g: piece_copy(1 - slot, a, g).start())

    _wait_pieces(_tab(tab_ref, 3, 0), ys_ref, buf_ref.at[slot], sem.at[slot])

    pos = pos_ref[...]
    wts = wts_ref[...]
    col = lax.broadcasted_iota(I32, (tm, SORT_ROWS), 1)
    weight = jnp.zeros((tm, SORT_ROWS), F32)
    for k in range(TOP_K):
        weight = jnp.where(col == pos[:, k:k + 1], wts[:, k:k + 1], weight)
    routed = _dot(weight.astype(BF16), _unpack_rows(buf_ref[slot]))
    gate2 = mod_ref[0][:, 5 * d:6 * d]
    out = _layer_norm(r0_ref[...] + gate2 * routed) * g2_ref[...] + b2_ref[...]
    @pl.when(i < nct)
    def _():
        oc_ref[...] = out

    @pl.when(i >= nct)
    def _():
        ol_ref[...] = out


def _combine(st, pstart, tab, r0, pos, wts, mod_l, ln2_g, ln2_b, ys_sorted):
    n, d = r0.shape
    tm = st.tm
    tok = lambda w: pl.BlockSpec((tm, w), lambda i, *_: (i, 0))
    last = st.tiles - 1
    out_shape = [jax.ShapeDtypeStruct((st.n_ctx, d), F32), jax.ShapeDtypeStruct((n - st.n_ctx, d), F32)]
    out_specs = [pl.BlockSpec((tm, d), lambda i, *_: (jnp.minimum(i, st.nct - 1), 0)),
                 pl.BlockSpec((tm, d), lambda i, *_: (jnp.maximum(i - st.nct, 0), 0))]
    return pl.pallas_call(
        functools.partial(_combine_kernel, nct=st.nct),
        out_shape=out_shape,
        grid_spec=pltpu.PrefetchScalarGridSpec(
            num_scalar_prefetch=1,
            grid=(st.tiles,),
            in_specs=[pl.BlockSpec((TAB_SIZE,), lambda i, *_: (i,), memory_space=pltpu.SMEM),
                      pl.BlockSpec((TAB_SIZE,), lambda i, *_: (jnp.minimum(i + 1, last),), memory_space=pltpu.SMEM),
                      tok(d), tok(LANES), tok(LANES),
                      pl.BlockSpec((1, 1, 6 * d), lambda i, *_: (st.mod_row(i), 0, 0)),
                      _full((1, d)), _full((1, d)),
                      pl.BlockSpec(memory_space=pl.ANY)],
            out_specs=out_specs,
            scratch_shapes=[pltpu.VMEM((2, SORT_ROWS, PACKED), U32), pltpu.SemaphoreType.DMA((2,))]),
        compiler_params=pltpu.CompilerParams(dimension_semantics=("arbitrary",), vmem_limit_bytes=VMEM_LIMIT),
        name="moe_combine",
    )(pstart, tab, tab, r0, pos, wts, mod_l, ln2_g, ln2_b, ys_sorted)


def _moe_routed(st, xs_local, tab, counts, lw, layer):
    r = EXPERT_ROWS
    n_blocks = -(-(st.tiles * (st.tm * TOP_K + N_EXPERTS * (PIECE - 1))) // r) + N_EXPERTS
    cnt = counts[0, :N_EXPERTS].astype(I32)
    padded = (cnt + r - 1) // r * r
    pend = jnp.cumsum(padded)
    pstart = (pend - padded).astype(I32)
    n_used = (pend[-1] // r).astype(I32)
    blocks = jnp.arange(n_blocks, dtype=I32)
    block_i = jnp.minimum(blocks, n_used - 1)
    block_e = jnp.sum((pend[None, :] <= (block_i * r)[:, None]).astype(I32), axis=1)
    block_e = jnp.minimum(block_e, N_EXPERTS - 1).astype(I32)
    gaps = jnp.stack([pstart + cnt, (padded - cnt) // PIECE]).astype(I32)
    tail = jnp.stack([n_used * r, n_blocks - n_used]).astype(I32)
    xs = _to_sorted(pstart, gaps, tail, tab, xs_local, n_blocks * r)
    ys = _experts(block_e, block_i, n_used.reshape(1), xs, lw['exp_w_gate'], lw['exp_w_up'], lw['exp_w_down'], layer)
    return pstart, ys


def _channel_dft():
    j = jnp.arange(FOURIER_DIM)
    same = (j[:, None] // FOURIER_GROUP_DIM) == (j[None, :] // FOURIER_GROUP_DIM)
    prod = ((j[:, None] % FOURIER_GROUP_DIM) * (j[None, :] % FOURIER_GROUP_DIM)) % FOURIER_GROUP_DIM
    ang = (2.0 * jnp.pi / FOURIER_GROUP_DIM) * prod.astype(F32)
    scale = FOURIER_GROUP_DIM ** -0.5
    return (jnp.where(same, jnp.cos(ang) * scale, 0.0).astype(BF16),
            jnp.where(same, jnp.sin(ang) * scale, 0.0).astype(BF16))


def _position_dft(n):
    m = 1 << (n.bit_length() // 2)
    assert n % m == 0
    k = jnp.arange(n, dtype=I32)[None, :]
    ang_hi = (2.0 * jnp.pi / n) * ((jnp.arange(n // m, dtype=I32)[:, None] * m * k) % n).astype(F32)
    ang_lo = (2.0 * jnp.pi / n) * ((jnp.arange(m, dtype=I32)[:, None] * k) % n).astype(F32)
    scale = n ** -0.5
    ch, sh = (jnp.cos(ang_hi) * scale)[:, None, :], (jnp.sin(ang_hi) * scale)[:, None, :]
    cl, sl = jnp.cos(ang_lo)[None, :, :], jnp.sin(ang_lo)[None, :, :]
    return ((ch * cl - sh * sl).reshape(n, n).astype(BF16), (sh * cl + ch * sl).reshape(n, n).astype(BF16))


def _rope_tables(s_ctx, n_lat):
    t = jnp.arange(n_lat)
    inv = ROPE_BASE ** (-jnp.arange(ROPE_HALF, dtype=F32) / ROPE_HALF)
    row_ang = (t // GRID_W).astype(F32)[:, None] * inv[None, :]
    col_ang = (t % GRID_W).astype(F32)[:, None] * inv[None, :]
    cos = jnp.concatenate([jnp.cos(row_ang)] * 2 + [jnp.cos(col_ang)] * 2, axis=1)
    sin = jnp.concatenate([-jnp.sin(row_ang), jnp.sin(row_ang), -jnp.sin(col_ang), jnp.sin(col_ang)], axis=1)
    reps = LANES // HEAD_DIM
    cos = jnp.concatenate([jnp.ones((s_ctx, LANES), F32), jnp.tile(cos, (1, reps))], axis=0)
    sin = jnp.concatenate([jnp.zeros((s_ctx, LANES), F32), jnp.tile(sin, (1, reps))], axis=0)
    return cos, sin


def kernel(x_prompt, x_sample, cache_k, cache_v, c, c_ctx, w_ada, b_ada, w_in, conv_w, conv_b, conv_ln_g, conv_ln_b, w_conv_out, attn_sink, w_attn_out, w_fourier_out, w_out, post_ln_g, post_ln_b, router_w, router_bias, exp_w_gate, exp_w_up, exp_w_down, shared_w_gate, shared_w_up, shared_w_down):
    b_ctx, s_ctx, d = x_prompt.shape
    b_lat, s_lat, _ = x_sample.shape
    depth = w_in.shape[0]
    assert b_lat < MOD_ROWS and d == D_MODEL
    st = _Stream(b_ctx, s_ctx, b_lat, s_lat)
    nc = st.n_ctx

    cond = jnp.concatenate([c, c_ctx[None, :], jnp.zeros((MOD_ROWS - b_lat - 1, d), F32)], axis=0)
    mod = _modulation(cond, w_ada, b_ada)

    cc, sc = _channel_dft()
    cn_ctx, sn_ctx = _position_dft(s_ctx)
    cn_lat, sn_lat = _position_dft(s_lat)
    rope_cos, rope_sin = _rope_tables(s_ctx, s_lat)

    y_ctx = x_prompt.reshape(nc, d)
    y_lat = x_sample.reshape(b_lat * s_lat, d)
    new_k, new_v = [], []
    for l in range(depth):
        rw = jnp.pad(router_w[l].T, ((0, LANES - N_EXPERTS), (0, 0)))
        rw_hi = rw.astype(BF16)
        row = lambda a: a.reshape(1, -1)
        lw = dict(
            w_gates=w_in[l][:, PRE_COLS:].astype(BF16),
            conv_w=conv_w[l], conv_b=row(conv_b[l]), conv_ln_g=row(conv_ln_g[l]), conv_ln_b=row(conv_ln_b[l]),
            w_conv_out=w_conv_out[l].astype(BF16), w_attn_out=w_attn_out[l].astype(BF16),
            w_fourier_out=w_fourier_out[l].astype(BF16), w_out=w_out[l].astype(BF16),
            ln1_g=row(post_ln_g[l, 0]), ln1_b=row(post_ln_b[l, 0]),
            router_hi=rw_hi, router_lo=(rw - rw_hi.astype(F32)).astype(BF16),
            router_bias=jnp.broadcast_to(
                jnp.pad(router_bias[l], (0, LANES - N_EXPERTS), constant_values=-jnp.inf)[:, None],
                (LANES, TOKEN_TILE)),
            exp_w_gate=exp_w_gate, exp_w_up=exp_w_up, exp_w_down=exp_w_down,
            shared_w_gate=shared_w_gate[l].astype(BF16), shared_w_up=shared_w_up[l].astype(BF16),
            shared_w_down=shared_w_down[l].astype(BF16))
        mod_l = mod[l].reshape(MOD_ROWS, 1, 6 * d)
        sink = attn_sink[l]

        z, q_c, q_l, k_c, k_l, v_c, v_l, uc_c, uc_l, us_c, us_l = _premix(
            st, y_ctx, y_lat, mod_l, w_in[l][:, :PRE_COLS].astype(BF16), cc, sc, rope_cos, rope_sin)
        new_k.append(k_c.reshape(b_ctx, s_ctx, N_KV_HEADS, HEAD_DIM))
        new_v.append(v_c.reshape(b_ctx, s_ctx, N_KV_HEADS, HEAD_DIM))
        attn_c = _ctx_attention(sink, q_c, k_c, v_c, b_ctx, s_ctx)
        attn_l = _lat_attention(sink, q_l, k_l, v_l, cache_k[:, l].reshape(-1, KV_DIM),
                                cache_v[:, l].reshape(-1, KV_DIM), b_lat, s_lat)
        four_c = _fourier_positions(cn_ctx, sn_ctx, uc_c, us_c, b_ctx)
        four_l = _fourier_positions(cn_lat, sn_lat, uc_l, us_l, b_lat)
        r0, xs_local, pos, wts, tab, counts = _postmix(st, y_ctx, y_lat, mod_l, z, attn_c, attn_l, four_c, four_l, lw)
        tab = tab.reshape(-1)
        pstart, ys_sorted = _moe_routed(st, xs_local, tab, counts, lw, l)
        y_ctx, y_lat = _combine(st, pstart, tab, r0, pos, wts, mod_l, row(post_ln_g[l, 1]), row(post_ln_b[l, 1]),
                                ys_sorted)

    return (y_ctx.reshape(b_ctx, s_ctx, d), y_lat.reshape(b_lat, s_lat, d),
            jnp.stack(new_k, axis=1), jnp.stack(new_v, axis=1))
```

```python
import functools

import jax
import jax.numpy as jnp
from jax import lax
from jax.experimental import pallas as pl
from jax.experimental.pallas import tpu as pltpu

F32 = jnp.float32
BF16 = jnp.bfloat16
I32 = jnp.int32
U32 = jnp.uint32

D_MODEL = 1024
DEPTH = 2
GRID_W = 64
CONV_DIM = 256
CONV_WIDTH = 31
CONV_PAD = CONV_WIDTH // 2
CONV_HALO = 16
N_HEADS = 8
N_KV_HEADS = 2
HEAD_DIM = 64
Q_PER_KV = N_HEADS // N_KV_HEADS
Q_DIM = N_HEADS * HEAD_DIM
KV_DIM = N_KV_HEADS * HEAD_DIM
WINDOW = 128
ATTN_BLOCK = 128
ATTN_SCALE = HEAD_DIM ** -0.5
ROPE_AXIS_DIM = HEAD_DIM // 2
ROPE_HALF = ROPE_AXIS_DIM // 2
ROPE_BASE = 10000.0
NEG_INF = -1e30
FOURIER_GROUPS = 4
FOURIER_GROUP_DIM = 64
FOURIER_DIM = FOURIER_GROUPS * FOURIER_GROUP_DIM
PRE_COLS = 2 * CONV_DIM + Q_DIM + 2 * KV_DIM + FOURIER_DIM
N_EXPERTS = 64
TOP_K = 6
EXPERT_DIM = 256
ROUTED_SCALE = 2.5
LN_EPS = 1e-6
DEEPNORM_ALPHA = (2 * DEPTH) ** 0.25

LANES = 128
SUBLANES = 8
MOD_ROWS = 16
TOKEN_TILE = 256
EXPERT_ROWS = 512
PIECE = SUBLANES
SORT_ROWS = -(-(TOKEN_TILE * TOP_K + N_EXPERTS * (PIECE - 1)) // 256) * 256
SORT_PIECES = SORT_ROWS // PIECE
TAB_COLS = LANES + SORT_PIECES
ATTN_STACK = Q_PER_KV
TAB_SIZE = SUBLANES * TAB_COLS
PIECE_UNROLL = 4
PACKED = D_MODEL // 2
VMEM_LIMIT = 56 * 1024 * 1024


def _layer_norm(x):
    mu = jnp.mean(x, axis=-1, keepdims=True)
    xc = x - mu
    var = jnp.mean(xc * xc, axis=-1, keepdims=True)
    return xc * lax.rsqrt(var + LN_EPS)


def _silu(x):
    return x * jax.nn.sigmoid(x)


def _dot(a, b):
    return jnp.dot(a, b, preferred_element_type=F32)


def _full(shape):
    return pl.BlockSpec(shape, lambda *_: (0,) * len(shape), pipeline_mode=pl.Buffered(1))


def _pack_rows(y, rounded=False):
    bits = lax.bitcast_convert_type(y if rounded else y.astype(BF16).astype(F32), U32)
    return (bits[:, :PACKED] >> 16) | bits[:, PACKED:]


def _unpack_rows(u):
    lo = lax.bitcast_convert_type(u << 16, F32)
    hi = lax.bitcast_convert_type(u & jnp.uint32(0xFFFF0000), F32)
    return jnp.concatenate([lo, hi], axis=1).astype(BF16)


class _Stream:
    def __init__(self, b_ctx, s_ctx, b_lat, s_lat):
        self.tm = TOKEN_TILE
        assert s_ctx % self.tm == 0 and s_lat % self.tm == 0
        self.b_ctx, self.s_ctx, self.b_lat, self.s_lat = b_ctx, s_ctx, b_lat, s_lat
        self.n_ctx = b_ctx * s_ctx
        self.n = self.n_ctx + b_lat * s_lat
        self.tpb_c = s_ctx // self.tm
        self.tpb_l = s_lat // self.tm
        self.nct = self.n_ctx // self.tm
        self.tiles = self.n // self.tm
        self.ctx_row = b_lat

    def mod_row(self, i):
        return jnp.where(i >= self.nct, (i - self.nct) // self.tpb_l, self.ctx_row)

    def tile_in_seq(self, i):
        return jnp.where(i >= self.nct, (i - self.nct) % self.tpb_l, i % self.tpb_c)

    def tiles_per_seq(self, i):
        return jnp.where(i >= self.nct, self.tpb_l, self.tpb_c)

    def rope_block(self, i):
        return jnp.where(i >= self.nct, self.tpb_c + (i - self.nct) % self.tpb_l, i % self.tpb_c)


def _mod_kernel(c_ref, w_ref, b_ref, o_ref):
    o_ref[0] = _dot(_silu(c_ref[...]), w_ref[0]) + b_ref[0]


def _modulation(cond, w_ada, b_ada):
    depth, d, cols = w_ada.shape
    cb = cols // 4
    return pl.pallas_call(
        _mod_kernel,
        out_shape=jax.ShapeDtypeStruct((depth, MOD_ROWS, cols), F32),
        grid=(depth, cols // cb),
        in_specs=[pl.BlockSpec((MOD_ROWS, d), lambda l, j: (0, 0)),
                  pl.BlockSpec((1, d, cb), lambda l, j: (l, 0, j)),
                  pl.BlockSpec((1, 1, cb), lambda l, j: (l, 0, j))],
        out_specs=pl.BlockSpec((1, MOD_ROWS, cb), lambda l, j: (l, 0, j)),
        compiler_params=pltpu.CompilerParams(vmem_limit_bytes=VMEM_LIMIT),
        name="modulation",
    )(cond, w_ada, b_ada.reshape(depth, 1, cols))


def _premix_kernel(xc_ref, xl_ref, mod_ref, w_ref, cc_ref, sc_ref, cos_ref, sin_ref, z_ref, *outs, nct):
    d = D_MODEL
    is_lat = pl.program_id(0) >= nct
    mod = mod_ref[0]
    h = _layer_norm(jnp.where(is_lat, xl_ref[...], xc_ref[...])) * (1.0 + mod[:, d:2 * d]) + mod[:, 0:d]
    p = _dot(h.astype(BF16), w_ref[...])
    c0 = CONV_DIM
    z_ref[...] = (p[:, 0:c0] * jax.nn.sigmoid(p[:, c0:2 * c0])).astype(z_ref.dtype)
    q0 = 2 * c0
    k0 = q0 + Q_DIM
    v0 = k0 + KV_DIM
    f0 = v0 + KV_DIM
    cos = cos_ref[...]
    sin = sin_ref[...]
    first = (lax.broadcasted_iota(I32, cos.shape, 1) % ROPE_AXIS_DIM) < ROPE_HALF

    def rot(t):
        sw = jnp.where(first, pltpu.roll(t, LANES - ROPE_HALF, 1), pltpu.roll(t, ROPE_HALF, 1))
        return t * cos + sw * sin

    q = p[:, q0:k0] * ATTN_SCALE
    q = jnp.concatenate([rot(q[:, j * LANES:(j + 1) * LANES]) for j in range(Q_DIM // LANES)], axis=1)
    uf = p[:, f0:f0 + FOURIER_DIM].astype(BF16)
    results = [q.astype(BF16), rot(p[:, k0:v0]), p[:, v0:f0],
               _dot(uf, cc_ref[...]).astype(BF16), _dot(uf, sc_ref[...]).astype(BF16)]

    @pl.when(jnp.logical_not(is_lat))
    def _():
        for r, o_ref in zip(results, outs[0::2]):
            o_ref[...] = r

    @pl.when(is_lat)
    def _():
        for r, o_ref in zip(results, outs[1::2]):
            o_ref[...] = r


def _premix(st, x_ctx, x_lat, mod_l, w_pre, cc, sc, rope_cos, rope_sin):
    d = x_ctx.shape[1]
    tm = st.tm
    tok = lambda w: pl.BlockSpec((tm, w), lambda i: (i, 0))
    ctx_tok = lambda w: pl.BlockSpec((tm, w), lambda i: (jnp.minimum(i, st.nct - 1), 0))
    lat_tok = lambda w: pl.BlockSpec((tm, w), lambda i: (jnp.maximum(i - st.nct, 0), 0))
    rope = pl.BlockSpec((tm, LANES), lambda i: (st.rope_block(i), 0))
    n_lat = st.n - st.n_ctx
    out_shape = [jax.ShapeDtypeStruct((st.n, CONV_DIM), BF16)]
    out_specs = [tok(CONV_DIM)]
    for w, dt in ((Q_DIM, BF16), (KV_DIM, F32), (KV_DIM, F32), (FOURIER_DIM, BF16), (FOURIER_DIM, BF16)):
        out_shape += [jax.ShapeDtypeStruct((st.n_ctx, w), dt), jax.ShapeDtypeStruct((n_lat, w), dt)]
        out_specs += [ctx_tok(w), lat_tok(w)]
    return pl.pallas_call(
        functools.partial(_premix_kernel, nct=st.nct),
        out_shape=out_shape,
        grid=(st.tiles,),
        in_specs=[ctx_tok(d), lat_tok(d), pl.BlockSpec((1, 1, 6 * d), lambda i: (st.mod_row(i), 0, 0)),
                  _full(w_pre.shape), _full(cc.shape), _full(sc.shape), rope, rope],
        out_specs=out_specs,
        compiler_params=pltpu.CompilerParams(dimension_semantics=("arbitrary",), vmem_limit_bytes=VMEM_LIMIT),
        name="premix",
    )(x_ctx, x_lat, mod_l, w_pre, cc, sc, rope_cos, rope_sin)


def _sink_softmax_pv(s, sink_col, v):
    m = jnp.maximum(jnp.max(s, axis=1, keepdims=True), sink_col)
    p = jnp.exp(s - m)
    denom = jnp.sum(p, axis=1, keepdims=True) + jnp.exp(sink_col - m)
    return _dot(p.astype(BF16), v) / denom


def _attend_heads(sink_ref, q_ref, keys, vals, bias, o_ref):
    rows = q_ref.shape[0]
    for first in range(0, N_HEADS, ATTN_STACK):
        h = first // Q_PER_KV
        heads = list(range(first, first + ATTN_STACK))
        q = jnp.concatenate([q_ref[:, hd * HEAD_DIM:(hd + 1) * HEAD_DIM] for hd in heads], axis=0)
        s = lax.dot_general(q, keys[:, h * HEAD_DIM:(h + 1) * HEAD_DIM], (((1,), (1,)), ((), ())),
                            preferred_element_type=F32)
        if bias is not None:
            s = s + bias
        sink_col = jnp.concatenate([jnp.full((rows, 1), sink_ref[hd], F32) for hd in heads], axis=0)
        o = _sink_softmax_pv(s, sink_col, vals[:, h * HEAD_DIM:(h + 1) * HEAD_DIM])
        for g, hd in enumerate(heads):
            o_ref[:, hd * HEAD_DIM:(hd + 1) * HEAD_DIM] = o[g * rows:(g + 1) * rows].astype(o_ref.dtype)


def _ctx_attn_kernel(sink_ref, q_ref, k_ref, v_ref, o_ref):
    _attend_heads(sink_ref, q_ref, k_ref[...].astype(BF16), v_ref[...].astype(BF16), None, o_ref)


def _ctx_attention(sink, q, k, v, batch, seq):
    return pl.pallas_call(
        _ctx_attn_kernel,
        out_shape=jax.ShapeDtypeStruct((batch * seq, Q_DIM), BF16),
        grid=(batch,),
        in_specs=[pl.BlockSpec(memory_space=pltpu.SMEM),
                  pl.BlockSpec((seq, Q_DIM), lambda i: (i, 0)),
                  pl.BlockSpec((seq, KV_DIM), lambda i: (i, 0)),
                  pl.BlockSpec((seq, KV_DIM), lambda i: (i, 0))],
        out_specs=pl.BlockSpec((seq, Q_DIM), lambda i: (i, 0)),
        compiler_params=pltpu.CompilerParams(vmem_limit_bytes=VMEM_LIMIT),
        name="attn_ctx",
    )(sink, q, k, v)


def _lat_attn_kernel(sink_ref, q_ref, k_ref, v_ref, kc_ref, vc_ref, bias_ref, o_ref):
    i = pl.program_id(1)
    nblk = pl.num_programs(1)
    blk = ATTN_BLOCK
    starts = [pl.multiple_of(jnp.maximum(i - 1, 0) * blk, blk),
              pl.multiple_of(i * blk, blk),
              pl.multiple_of(jnp.minimum(i + 1, nblk - 1) * blk, blk)]
    keys = jnp.concatenate([k_ref[pl.ds(st, blk), :] for st in starts] + [kc_ref[...]], axis=0).astype(BF16)
    vals = jnp.concatenate([v_ref[pl.ds(st, blk), :] for st in starts] + [vc_ref[...]], axis=0).astype(BF16)
    _attend_heads(sink_ref, q_ref, keys, vals, bias_ref[0], o_ref)


def _window_bias(n_ctx_keys):
    blk = ATTN_BLOCK
    r = jnp.arange(blk)[:, None]
    j = jnp.arange(3 * blk)[None, :]
    band = (j >= r) & (j <= r + 2 * WINDOW)
    has_prev = j >= blk
    has_next = j < 2 * blk
    variants = [band & has_prev, band, band & has_next]
    out = []
    for m in variants:
        full = jnp.concatenate([m, jnp.ones((blk, n_ctx_keys), bool)], axis=1)
        out.append(jnp.tile(jnp.where(full, 0.0, NEG_INF).astype(F32), (ATTN_STACK, 1)))
    return jnp.stack(out)


def _lat_attention(sink, q, k, v, kc, vc, batch, seq):
    blk = ATTN_BLOCK
    nblk = seq // blk
    assert nblk >= 2 and WINDOW == blk
    pc = kc.shape[0] // batch
    bias = _window_bias(pc)
    nkeys = 3 * blk + pc
    bias_map = lambda i, j: (jnp.where(j == 0, 0, jnp.where(j == nblk - 1, 2, 1)), 0, 0)
    return pl.pallas_call(
        _lat_attn_kernel,
        out_shape=jax.ShapeDtypeStruct((batch * seq, Q_DIM), BF16),
        grid=(batch, nblk),
        in_specs=[pl.BlockSpec(memory_space=pltpu.SMEM),
                  pl.BlockSpec((blk, Q_DIM), lambda i, j: (i * nblk + j, 0)),
                  pl.BlockSpec((seq, KV_DIM), lambda i, j: (i, 0)),
                  pl.BlockSpec((seq, KV_DIM), lambda i, j: (i, 0)),
                  pl.BlockSpec((pc, KV_DIM), lambda i, j: (i, 0)),
                  pl.BlockSpec((pc, KV_DIM), lambda i, j: (i, 0)),
                  pl.BlockSpec((1, ATTN_STACK * blk, nkeys), bias_map)],
        out_specs=pl.BlockSpec((blk, Q_DIM), lambda i, j: (i * nblk + j, 0)),
        compiler_params=pltpu.CompilerParams(vmem_limit_bytes=VMEM_LIMIT),
        name="attn_lat",
    )(sink, q, k, v, kc, vc, bias)


def _fourier_kernel(cn_ref, sn_ref, uc_ref, us_ref, o_ref):
    o_ref[...] = (_dot(cn_ref[...], uc_ref[...]) - _dot(sn_ref[...], us_ref[...])).astype(o_ref.dtype)


def _fourier_positions(cn, sn, uc, us, nb):
    seq = cn.shape[0]
    tf = min(512, seq)
    jt = seq // tf
    return pl.pallas_call(
        _fourier_kernel,
        out_shape=jax.ShapeDtypeStruct((nb * seq, FOURIER_DIM), BF16),
        grid=(jt, nb),
        in_specs=[pl.BlockSpec((tf, seq), lambda j, b: (j, 0)),
                  pl.BlockSpec((tf, seq), lambda j, b: (j, 0)),
                  pl.BlockSpec((seq, FOURIER_DIM), lambda j, b: (b, 0)),
                  pl.BlockSpec((seq, FOURIER_DIM), lambda j, b: (b, 0))],
        out_specs=pl.BlockSpec((tf, FOURIER_DIM), lambda j, b: (b * jt + j, 0)),
        compiler_params=pltpu.CompilerParams(vmem_limit_bytes=VMEM_LIMIT),
        name="fourier",
    )(cn, sn, uc, us)


def _postmix_kernel(xc_ref, xl_ref, mod_ref, z_ref, zp_ref, zn_ref, attn_c_ref, attn_l_ref, four_c_ref, four_l_ref,
                    wg_ref, cw_ref, cb_ref, cg_ref,
                    cbeta_ref, wco_ref, wao_ref, wfo_ref, wo_ref, g1_ref, b1_ref, rwh_ref, rwl_ref, rb_ref,
                    sg_ref, su_ref, sd_ref,
                    r0_ref, xs_ref, pos_ref, wts_ref, tab_ref, cnt_ref, zs_ref, carry_ref, *, st):
    d = D_MODEL
    i = pl.program_id(0)
    tm = xc_ref.shape[0]
    is_lat = i >= st.nct
    attn = jnp.where(is_lat, attn_l_ref[...], attn_c_ref[...])
    four = jnp.where(is_lat, four_l_ref[...], four_c_ref[...])
    x = jnp.where(is_lat, xl_ref[...], xc_ref[...])
    mod = mod_ref[0]
    shift1, scale1, gate1, shift2, scale2, gate2 = [mod[:, j * d:(j + 1) * d] for j in range(6)]
    h = (_layer_norm(x) * (1.0 + scale1) + shift1).astype(BF16)
    gates = jax.nn.sigmoid(_dot(h, wg_ref[...]))

    tib = st.tile_in_seq(i)
    zs_ref[0:CONV_HALO, :] = zp_ref[...].astype(F32) * jnp.where(tib != 0, 1.0, 0.0)
    zs_ref[CONV_HALO + tm:, :] = zn_ref[...].astype(F32) * jnp.where(tib != st.tiles_per_seq(i) - 1, 1.0, 0.0)
    zs_ref[CONV_HALO:CONV_HALO + tm, :] = z_ref[...].astype(F32)
    acc = jnp.broadcast_to(cb_ref[...], (tm, CONV_DIM))
    first_off = CONV_HALO - CONV_PAD
    window = zs_ref[...]
    rows = window.shape[0]
    for r in range(SUBLANES):
        offs = [o for o in range(first_off, first_off + CONV_WIDTH) if o % SUBLANES == r]
        shifted = window if r == 0 else pltpu.roll(window, rows - r, 0)
        for o in offs:
            acc = acc + cw_ref[o - first_off:o - first_off + 1, :] * shifted[o - r:o - r + tm]
    conv = _silu(_layer_norm(acc) * cg_ref[...] + cbeta_ref[...]).astype(BF16)

    merged = (gates[:, 0:d] * _dot(conv, wco_ref[...])
              + gates[:, d:2 * d] * _dot(attn, wao_ref[...])
              + gates[:, 2 * d:3 * d] * _dot(four, wfo_ref[...]))
    mix = _dot(merged.astype(BF16), wo_ref[...])
    x1 = _layer_norm(DEEPNORM_ALPHA * x + gate1 * mix) * g1_ref[...] + b1_ref[...]
    h2 = _layer_norm(x1) * (1.0 + scale2) + shift2

    h2_hi = h2.astype(BF16)
    h2_lo = (h2 - h2_hi.astype(F32)).astype(BF16)
    nt = (((1,), (1,)), ((), ()))
    logits_t = (lax.dot_general(rwh_ref[...], h2_hi, nt, preferred_element_type=F32)
                + lax.dot_general(rwh_ref[...], h2_lo, nt, preferred_element_type=F32)
                + lax.dot_general(rwl_ref[...], h2_hi, nt, preferred_element_type=F32))
    scores_t = jax.nn.sigmoid(logits_t)
    sel_t = scores_t + rb_ref[...]
    expert_t = lax.broadcasted_iota(I32, (LANES, tm), 0)
    onehots_t, top_s = [], []
    for _ in range(TOP_K):
        mx = jnp.max(sel_t, axis=0, keepdims=True)
        idx = jnp.min(jnp.where(sel_t == mx, expert_t, LANES), axis=0, keepdims=True)
        oh = expert_t == idx
        onehots_t.append(oh)
        top_s.append(jnp.sum(jnp.where(oh, scores_t, 0.0), axis=0, keepdims=True))
        sel_t = jnp.where(oh, -jnp.inf, sel_t)
    total = top_s[0]
    for s_k in top_s[1:]:
        total = total + s_k

    @pl.when(i == 0)
    def _():
        carry_ref[...] = jnp.zeros_like(carry_ref)

    chosen_t = jnp.zeros((LANES, tm), F32)
    for oh in onehots_t:
        chosen_t = jnp.where(oh, 1.0, chosen_t)
    chosen_tb = chosen_t.astype(BF16)
    tok_r = lax.broadcasted_iota(I32, (tm, tm), 0)
    tok_c = lax.broadcasted_iota(I32, (tm, tm), 1)
    prefix_t = _dot(chosen_tb, jnp.where(tok_r < tok_c, 1.0, 0.0).astype(BF16))
    cpad_t = jnp.floor((jnp.sum(chosen_t, axis=1, keepdims=True) + (PIECE - 1)) * (1.0 / PIECE)) * PIECE
    exp_r = lax.broadcasted_iota(I32, (LANES, LANES), 0)
    exp_c = lax.broadcasted_iota(I32, (LANES, LANES), 1)
    cstart_t = _dot(jnp.where(exp_c < exp_r, 1.0, 0.0).astype(BF16),
                    jnp.broadcast_to(cpad_t, (LANES, tm)).astype(BF16))

    where_t = prefix_t + cstart_t
    krow = lax.broadcasted_iota(I32, (LANES, tm), 0)
    pos_t = jnp.zeros((LANES, tm), F32)
    wts_t = jnp.zeros((LANES, tm), F32)
    for k, oh in enumerate(onehots_t):
        pos_t = jnp.where(krow == k, jnp.sum(jnp.where(oh, where_t, 0.0), axis=0, keepdims=True), pos_t)
        wts_t = jnp.where(krow == k, ROUTED_SCALE * top_s[k] / total, wts_t)
    pos_ref[...] = pos_t.T.astype(I32)
    wts_ref[...] = wts_t.T

    chosen = chosen_t.T
    count = jnp.sum(chosen, axis=0, keepdims=True)
    cpad = jnp.floor((count + (PIECE - 1)) * (1.0 / PIECE)) * PIECE
    er = lax.broadcasted_iota(I32, (LANES, LANES), 0)
    ec = lax.broadcasted_iota(I32, (LANES, LANES), 1)
    cstart = _dot(jnp.broadcast_to(cpad, (SUBLANES, LANES)), jnp.where(er < ec, 1.0, 0.0))[0:1]
    gbase = carry_ref[...]
    carry_ref[...] = gbase + cpad
    cnt_ref[...] = gbase + cpad
    pieces = jnp.sum(cpad, axis=1, keepdims=True) * (1.0 / PIECE)
    trow = lax.broadcasted_iota(I32, (SUBLANES, LANES), 0)
    per_expert = jnp.where(trow == 0, cstart, jnp.where(trow == 1, cpad, jnp.where(trow == 2, gbase, pieces)))
    piece_row = (lax.broadcasted_iota(I32, (SORT_PIECES, LANES), 0) * PIECE).astype(F32)
    owns = jnp.where((piece_row >= cstart) & (piece_row < cstart + cpad), 1.0, 0.0).astype(BF16)
    expert_id = lax.broadcasted_iota(I32, (SUBLANES, LANES), 1).astype(BF16)
    piece_expert = lax.dot_general(expert_id, owns, (((1,), (1,)), ((), ())), preferred_element_type=F32)
    tab_ref[...] = jnp.concatenate([per_expert, piece_expert], axis=1).astype(I32)

    slot_e = lax.broadcasted_iota(I32, (SORT_ROWS, LANES), 0).astype(F32)
    slot_expert = jnp.where((slot_e >= cstart) & (slot_e < cstart + cpad), 1.0, 0.0).astype(BF16)
    rank_t = jnp.where(chosen_t > 0.0, prefix_t, -1.0)
    looked = _dot(slot_expert, jnp.concatenate([rank_t.astype(BF16), cstart_t.astype(BF16)], axis=1))
    slot_t = lax.broadcasted_iota(I32, (SORT_ROWS, tm), 0).astype(F32)
    perm = jnp.where(looked[:, :tm] + looked[:, tm:] == slot_t, 1.0, 0.0).astype(BF16)
    h2b = h2.astype(BF16)
    xs_ref[...] = _pack_rows(_dot(perm, h2b), rounded=True)

    shared = _dot((_silu(_dot(h2b, sg_ref[...])) * _dot(h2b, su_ref[...])).astype(BF16), sd_ref[...])
    r0_ref[...] = DEEPNORM_ALPHA * x1 + gate2 * shared


def _postmix(st, x_ctx, x_lat, mod_l, z, attn_c, attn_l, four_c, four_l, lw):
    n, d = st.n, x_ctx.shape[1]
    tm = st.tm
    hb = tm // CONV_HALO
    last_halo = n // CONV_HALO - 1
    tok = lambda w: pl.BlockSpec((tm, w), lambda i: (i, 0))
    ctx_tok = lambda w: pl.BlockSpec((tm, w), lambda i: (jnp.minimum(i, st.nct - 1), 0))
    lat_tok = lambda w: pl.BlockSpec((tm, w), lambda i: (jnp.maximum(i - st.nct, 0), 0))
    in_specs = [ctx_tok(d), lat_tok(d), pl.BlockSpec((1, 1, 6 * d), lambda i: (st.mod_row(i), 0, 0)), tok(CONV_DIM),
                pl.BlockSpec((CONV_HALO, CONV_DIM), lambda i: (jnp.maximum(i * hb - 1, 0), 0)),
                pl.BlockSpec((CONV_HALO, CONV_DIM), lambda i: (jnp.minimum((i + 1) * hb, last_halo), 0)),
                ctx_tok(Q_DIM), lat_tok(Q_DIM), ctx_tok(FOURIER_DIM), lat_tok(FOURIER_DIM)]
    args = [x_ctx, x_lat, mod_l, z, z, z, attn_c, attn_l, four_c, four_l]
    weights = [lw['w_gates'], lw['conv_w'], lw['conv_b'], lw['conv_ln_g'], lw['conv_ln_b'], lw['w_conv_out'],
               lw['w_attn_out'], lw['w_fourier_out'], lw['w_out'], lw['ln1_g'], lw['ln1_b'],
               lw['router_hi'], lw['router_lo'], lw['router_bias'],
               lw['shared_w_gate'], lw['shared_w_up'], lw['shared_w_down']]
    in_specs += [_full(w.shape) for w in weights]
    args += weights
    return pl.pallas_call(
        functools.partial(_postmix_kernel, st=st),
        out_shape=[jax.ShapeDtypeStruct((n, d), F32),
                   jax.ShapeDtypeStruct((st.tiles * SORT_ROWS, PACKED), U32),
                   jax.ShapeDtypeStruct((n, LANES), I32),
                   jax.ShapeDtypeStruct((n, LANES), F32),
                   jax.ShapeDtypeStruct((st.tiles * SUBLANES, TAB_COLS), I32),
                   jax.ShapeDtypeStruct((1, LANES), F32)],
        grid=(st.tiles,),
        in_specs=in_specs,
        out_specs=[tok(d), pl.BlockSpec((SORT_ROWS, PACKED), lambda i: (i, 0)), tok(LANES), tok(LANES),
                   pl.BlockSpec((SUBLANES, TAB_COLS), lambda i: (i, 0)), _full((1, LANES))],
        scratch_shapes=[pltpu.VMEM((tm + 2 * CONV_HALO, CONV_DIM), F32),
                        pltpu.VMEM((1, LANES), F32)],
        compiler_params=pltpu.CompilerParams(dimension_semantics=("arbitrary",), vmem_limit_bytes=VMEM_LIMIT),
        name="postmix",
    )(*args)


ZERO_SPANS = tuple(1 << b for b in reversed(range((EXPERT_ROWS // PIECE).bit_length() - 1)))


def _tab(tab_ref, r, c):
    return tab_ref[r * TAB_COLS + c]


def _for_each_piece(pstart_ref, tab_ref, fn):
    def one(p, prio):
        e = _tab(tab_ref, 0, LANES + p)
        local_row = p * PIECE
        sorted_row = pstart_ref[e] + _tab(tab_ref, 2, e) + (local_row - _tab(tab_ref, 0, e))
        fn(pl.multiple_of(local_row, PIECE), pl.multiple_of(sorted_row, PIECE), prio)

    n = _tab(tab_ref, 3, 0)
    groups = n // PIECE_UNROLL

    def group(g, c):
        for u in range(PIECE_UNROLL):
            one(g * PIECE_UNROLL + u, u % 2)
        return c

    lax.fori_loop(0, groups, group, 0)
    lax.fori_loop(groups * PIECE_UNROLL, n, lambda p, c: (one(p, 0), c)[1], 0)


def _wait_pieces(n_pieces, src_ref, dst_ref, sem):
    @pl.when(n_pieces > 0)
    def _():
        rows = n_pieces * PIECE
        pltpu.make_async_copy(src_ref.at[pl.ds(0, rows)], dst_ref.at[pl.ds(0, rows)], sem).wait()


RING = 3


def _to_sorted_kernel(pstart_ref, gap_ref, tail_ref, tab_ref, tab_prev_ref, xs_ref, dst_ref, buf_ref, zeros_ref,
                      in_sem, out_sem, zsem):
    i = pl.program_id(0)
    tiles = pl.num_programs(0)

    def load(tile):
        slot = tile % RING
        return pltpu.make_async_copy(xs_ref.at[pl.ds(pl.multiple_of(tile * SORT_ROWS, SORT_ROWS), SORT_ROWS)],
                                     buf_ref.at[slot], in_sem.at[slot])

    def piece_copy(slot, local_row, sorted_row):
        return pltpu.make_async_copy(buf_ref.at[slot, pl.ds(local_row, PIECE)],
                                     dst_ref.at[pl.ds(sorted_row, PIECE)], out_sem.at[slot])

    @pl.when(i == 0)
    def _():
        load(0).start()

        @pl.when(tiles > 1)
        def _():
            load(1).start()

    @pl.when(i >= 1)
    def _():
        prev = (i - 1) % RING
        _wait_pieces(_tab(tab_prev_ref, 3, 0), buf_ref.at[prev], dst_ref, out_sem.at[prev])

    @pl.when(i + 2 < tiles)
    def _():
        load(i + 2).start()

    slot = i % RING
    load(i).wait()
    _for_each_piece(pstart_ref, tab_ref, lambda a, g, prio: piece_copy(slot, a, g).start(priority=prio))

    @pl.when(i == tiles - 1)
    def _():
        _wait_pieces(_tab(tab_ref, 3, 0), buf_ref.at[slot], dst_ref, out_sem.at[slot])

    @pl.when(i == 0)
    def _():
        zeros_ref[...] = jnp.zeros_like(zeros_ref)

        def zero_copy(row, span):
            r = pl.multiple_of(row, PIECE)
            return pltpu.make_async_copy(zeros_ref.at[pl.ds(0, span * PIECE)],
                                         dst_ref.at[pl.ds(r, span * PIECE)], zsem)

        def gaps(e, counts):
            row = gap_ref[0, e]
            g = gap_ref[1, e]
            new = []
            for span, cnt in zip(ZERO_SPANS, counts):
                hit = (g & span) != 0

                @pl.when(hit)
                def _():
                    zero_copy(row, span).start()

                row = row + jnp.where(hit, span * PIECE, 0)
                new.append(cnt + jnp.where(hit, 1, 0))
            return tuple(new)

        counts = lax.fori_loop(0, N_EXPERTS, gaps, (jnp.int32(0),) * len(ZERO_SPANS))
        for span, cnt in zip(ZERO_SPANS, counts):
            lax.fori_loop(0, cnt, lambda p, c: (zero_copy(0, span).wait(), c)[1], 0)

        blk = EXPERT_ROWS // PIECE

        def tail(b, c):
            zero_copy(tail_ref[0] + b * EXPERT_ROWS, blk).start()
            return c

        lax.fori_loop(0, tail_ref[1], tail, 0)
        lax.fori_loop(0, tail_ref[1], lambda p, c: (zero_copy(0, blk).wait(), c)[1], 0)


def _to_sorted(pstart, gaps, tail, tab, xs_local, n_rows):
    tiles = tab.shape[0] // TAB_SIZE
    return pl.pallas_call(
        _to_sorted_kernel,
        out_shape=jax.ShapeDtypeStruct((n_rows, PACKED), U32),
        grid_spec=pltpu.PrefetchScalarGridSpec(
            num_scalar_prefetch=3,
            grid=(tiles,),
            in_specs=[pl.BlockSpec((TAB_SIZE,), lambda i, *_: (i,), memory_space=pltpu.SMEM),
                      pl.BlockSpec((TAB_SIZE,), lambda i, *_: (jnp.maximum(i - 1, 0),), memory_space=pltpu.SMEM),
                      pl.BlockSpec(memory_space=pl.ANY)],
            out_specs=pl.BlockSpec(memory_space=pl.ANY),
            scratch_shapes=[pltpu.VMEM((RING, SORT_ROWS, PACKED), U32),
                            pltpu.VMEM((EXPERT_ROWS, PACKED), U32),
                            pltpu.SemaphoreType.DMA((RING,)), pltpu.SemaphoreType.DMA((RING,)),
                            pltpu.SemaphoreType.DMA]),
        compiler_params=pltpu.CompilerParams(dimension_semantics=("arbitrary",), vmem_limit_bytes=VMEM_LIMIT),
        name="moe_to_sorted",
    )(pstart, gaps, tail, tab, tab, xs_local)


def _experts_kernel(be_ref, nu_ref, xs_ref, wg_ref, wu_ref, wd_ref, ys_ref, xbuf_ref, sem, wgb_ref, wub_ref,
                    wdb_ref):
    j = pl.program_id(0)
    n_used = nu_ref[0]
    used = j < n_used
    rows = xbuf_ref.shape[1]

    def fetch(block):
        slot = block % RING
        return pltpu.make_async_copy(xs_ref.at[pl.ds(pl.multiple_of(block * rows, rows), rows)],
                                     xbuf_ref.at[slot], sem.at[slot])

    @pl.when(j == 0)
    def _():
        fetch(0).start()

        @pl.when(n_used > 1)
        def _():
            fetch(1).start()

    @pl.when(j + 2 < n_used)
    def _():
        fetch(j + 2).start()

    @pl.when(jnp.logical_or(j == 0, be_ref[j] != be_ref[jnp.maximum(j - 1, 0)]))
    def _():
        wgb_ref[...] = wg_ref[0, 0].astype(BF16)
        wub_ref[...] = wu_ref[0, 0].astype(BF16)
        wdb_ref[...] = wd_ref[0, 0].astype(BF16)

    @pl.when(used)
    def _():
        fetch(j).wait()
        x = _unpack_rows(xbuf_ref[j % RING])
        a = (_silu(_dot(x, wgb_ref[...])) * _dot(x, wub_ref[...])).astype(BF16)
        ys_ref[...] = _pack_rows(_dot(a, wdb_ref[...]))

    @pl.when(jnp.logical_not(used))
    def _():
        ys_ref[...] = jnp.zeros_like(ys_ref)


def _experts(block_e, n_used, xs, w_gate, w_up, w_down, layer):
    n_rows = xs.shape[0]
    d = w_gate.shape[2]
    r = EXPERT_ROWS
    return pl.pallas_call(
        _experts_kernel,
        out_shape=jax.ShapeDtypeStruct((n_rows, PACKED), U32),
        grid_spec=pltpu.PrefetchScalarGridSpec(
            num_scalar_prefetch=2,
            grid=(n_rows // r,),
            in_specs=[pl.BlockSpec(memory_space=pl.ANY),
                      pl.BlockSpec((1, 1, d, EXPERT_DIM), lambda j, be, nu: (layer, be[j], 0, 0)),
                      pl.BlockSpec((1, 1, d, EXPERT_DIM), lambda j, be, nu: (layer, be[j], 0, 0)),
                      pl.BlockSpec((1, 1, EXPERT_DIM, d), lambda j, be, nu: (layer, be[j], 0, 0))],
            out_specs=pl.BlockSpec((r, PACKED), lambda j, be, nu: (j, 0)),
            scratch_shapes=[pltpu.VMEM((RING, r, PACKED), U32), pltpu.SemaphoreType.DMA((RING,)),
                            pltpu.VMEM((d, EXPERT_DIM), BF16), pltpu.VMEM((d, EXPERT_DIM), BF16),
                            pltpu.VMEM((EXPERT_DIM, d), BF16)]),
        compiler_params=pltpu.CompilerParams(dimension_semantics=("arbitrary",), vmem_limit_bytes=VMEM_LIMIT),
        name="moe_experts",
    )(block_e, n_used, xs, w_gate, w_up, w_down)


def _combine_kernel(pstart_ref, tab_ref, tab_next_ref, r0_ref, pos_ref, wts_ref, mod_ref, g2_ref, b2_ref, ys_ref,
                    oc_ref, ol_ref, buf_ref, sem, *, nct):
    tm, d = r0_ref.shape
    i = pl.program_id(0)
    slot = i % 2

    def piece_copy(slot_idx, local_row, sorted_row):
        return pltpu.make_async_copy(ys_ref.at[pl.ds(sorted_row, PIECE)],
                                     buf_ref.at[slot_idx, pl.ds(local_row, PIECE)], sem.at[slot_idx])

    @pl.when(i == 0)
    def _():
        buf_ref[...] = jnp.zeros_like(buf_ref)
        _for_each_piece(pstart_ref, tab_ref, lambda a, g, prio: piece_copy(0, a, g).start(priority=prio))

    @pl.when(i + 1 < pl.num_programs(0))
    def _():
        _for_each_piece(pstart_ref, tab_next_ref,
                        lambda a, g, prio: piece_copy(1 - slot, a, g).start(priority=prio))

    _wait_pieces(_tab(tab_ref, 3, 0), ys_ref, buf_ref.at[slot], sem.at[slot])

    pos = pos_ref[...]
    wts = wts_ref[...]
    col = lax.broadcasted_iota(I32, (tm, SORT_ROWS), 1)
    weight = jnp.zeros((tm, SORT_ROWS), F32)
    for k in range(TOP_K):
        weight = jnp.where(col == pos[:, k:k + 1], wts[:, k:k + 1], weight)
    routed = _dot(weight.astype(BF16), _unpack_rows(buf_ref[slot]))
    gate2 = mod_ref[0][:, 5 * d:6 * d]
    out = _layer_norm(r0_ref[...] + gate2 * routed) * g2_ref[...] + b2_ref[...]
    @pl.when(i < nct)
    def _():
        oc_ref[...] = out

    @pl.when(i >= nct)
    def _():
        ol_ref[...] = out


def _combine(st, pstart, tab, r0, pos, wts, mod_l, ln2_g, ln2_b, ys_sorted):
    n, d = r0.shape
    tm = st.tm
    tok = lambda w: pl.BlockSpec((tm, w), lambda i, *_: (i, 0))
    last = st.tiles - 1
    out_shape = [jax.ShapeDtypeStruct((st.n_ctx, d), F32), jax.ShapeDtypeStruct((n - st.n_ctx, d), F32)]
    out_specs = [pl.BlockSpec((tm, d), lambda i, *_: (jnp.minimum(i, st.nct - 1), 0)),
                 pl.BlockSpec((tm, d), lambda i, *_: (jnp.maximum(i - st.nct, 0), 0))]
    return pl.pallas_call(
        functools.partial(_combine_kernel, nct=st.nct),
        out_shape=out_shape,
        grid_spec=pltpu.PrefetchScalarGridSpec(
            num_scalar_prefetch=1,
            grid=(st.tiles,),
            in_specs=[pl.BlockSpec((TAB_SIZE,), lambda i, *_: (i,), memory_space=pltpu.SMEM),
                      pl.BlockSpec((TAB_SIZE,), lambda i, *_: (jnp.minimum(i + 1, last),), memory_space=pltpu.SMEM),
                      tok(d), tok(LANES), tok(LANES),
                      pl.BlockSpec((1, 1, 6 * d), lambda i, *_: (st.mod_row(i), 0, 0)),
                      _full((1, d)), _full((1, d)),
                      pl.BlockSpec(memory_space=pl.ANY)],
            out_specs=out_specs,
            scratch_shapes=[pltpu.VMEM((2, SORT_ROWS, PACKED), U32), pltpu.SemaphoreType.DMA((2,))]),
        compiler_params=pltpu.CompilerParams(dimension_semantics=("arbitrary",), vmem_limit_bytes=VMEM_LIMIT),
        name="moe_combine",
    )(pstart, tab, tab, r0, pos, wts, mod_l, ln2_g, ln2_b, ys_sorted)


def _moe_routed(st, xs_local, tab, counts, lw, layer):
    r = EXPERT_ROWS
    n_blocks = -(-(st.tiles * (st.tm * TOP_K + N_EXPERTS * (PIECE - 1))) // r) + N_EXPERTS
    cnt = counts[0, :N_EXPERTS].astype(I32)
    padded = (cnt + r - 1) // r * r
    pend = jnp.cumsum(padded)
    pstart = (pend - padded).astype(I32)
    n_used = (pend[-1] // r).astype(I32)
    blocks = jnp.arange(n_blocks, dtype=I32)
    block_i = jnp.minimum(blocks, n_used - 1)
    block_e = jnp.sum((pend[None, :] <= (block_i * r)[:, None]).astype(I32), axis=1)
    block_e = jnp.minimum(block_e, N_EXPERTS - 1).astype(I32)
    gaps = jnp.stack([pstart + cnt, (padded - cnt) // PIECE]).astype(I32)
    tail = jnp.stack([n_used * r, n_blocks - n_used]).astype(I32)
    xs = _to_sorted(pstart, gaps, tail, tab, xs_local, n_blocks * r)
    ys = _experts(block_e, n_used.reshape(1), xs, lw['exp_w_gate'], lw['exp_w_up'], lw['exp_w_down'], layer)
    return pstart, ys


def _channel_dft():
    j = jnp.arange(FOURIER_DIM)
    same = (j[:, None] // FOURIER_GROUP_DIM) == (j[None, :] // FOURIER_GROUP_DIM)
    prod = ((j[:, None] % FOURIER_GROUP_DIM) * (j[None, :] % FOURIER_GROUP_DIM)) % FOURIER_GROUP_DIM
    ang = (2.0 * jnp.pi / FOURIER_GROUP_DIM) * prod.astype(F32)
    scale = FOURIER_GROUP_DIM ** -0.5
    return (jnp.where(same, jnp.cos(ang) * scale, 0.0).astype(BF16),
            jnp.where(same, jnp.sin(ang) * scale, 0.0).astype(BF16))


def _position_dft(n):
    m = 1 << (n.bit_length() // 2)
    assert n % m == 0
    k = jnp.arange(n, dtype=I32)[None, :]
    ang_hi = (2.0 * jnp.pi / n) * ((jnp.arange(n // m, dtype=I32)[:, None] * m * k) % n).astype(F32)
    ang_lo = (2.0 * jnp.pi / n) * ((jnp.arange(m, dtype=I32)[:, None] * k) % n).astype(F32)
    scale = n ** -0.5
    ch, sh = (jnp.cos(ang_hi) * scale)[:, None, :], (jnp.sin(ang_hi) * scale)[:, None, :]
    cl, sl = jnp.cos(ang_lo)[None, :, :], jnp.sin(ang_lo)[None, :, :]
    return ((ch * cl - sh * sl).reshape(n, n).astype(BF16), (sh * cl + ch * sl).reshape(n, n).astype(BF16))


def _rope_tables(s_ctx, n_lat):
    t = jnp.arange(n_lat)
    inv = ROPE_BASE ** (-jnp.arange(ROPE_HALF, dtype=F32) / ROPE_HALF)
    row_ang = (t // GRID_W).astype(F32)[:, None] * inv[None, :]
    col_ang = (t % GRID_W).astype(F32)[:, None] * inv[None, :]
    cos = jnp.concatenate([jnp.cos(row_ang)] * 2 + [jnp.cos(col_ang)] * 2, axis=1)
    sin = jnp.concatenate([-jnp.sin(row_ang), jnp.sin(row_ang), -jnp.sin(col_ang), jnp.sin(col_ang)], axis=1)
    reps = LANES // HEAD_DIM
    cos = jnp.concatenate([jnp.ones((s_ctx, LANES), F32), jnp.tile(cos, (1, reps))], axis=0)
    sin = jnp.concatenate([jnp.zeros((s_ctx, LANES), F32), jnp.tile(sin, (1, reps))], axis=0)
    return cos, sin


def kernel(x_prompt, x_sample, cache_k, cache_v, c, c_ctx, w_ada, b_ada, w_in, conv_w, conv_b, conv_ln_g, conv_ln_b, w_conv_out, attn_sink, w_attn_out, w_fourier_out, w_out, post_ln_g, post_ln_b, router_w, router_bias, exp_w_gate, exp_w_up, exp_w_down, shared_w_gate, shared_w_up, shared_w_down):
    b_ctx, s_ctx, d = x_prompt.shape
    b_lat, s_lat, _ = x_sample.shape
    depth = w_in.shape[0]
    assert b_lat < MOD_ROWS and d == D_MODEL
    st = _Stream(b_ctx, s_ctx, b_lat, s_lat)
    nc = st.n_ctx

    cond = jnp.concatenate([c, c_ctx[None, :], jnp.zeros((MOD_ROWS - b_lat - 1, d), F32)], axis=0)
    mod = _modulation(cond, w_ada, b_ada)

    cc, sc = _channel_dft()
    cn_ctx, sn_ctx = _position_dft(s_ctx)
    cn_lat, sn_lat = _position_dft(s_lat)
    rope_cos, rope_sin = _rope_tables(s_ctx, s_lat)

    y_ctx = x_prompt.reshape(nc, d)
    y_lat = x_sample.reshape(b_lat * s_lat, d)
    new_k, new_v = [], []
    for l in range(depth):
        rw = jnp.pad(router_w[l].T, ((0, LANES - N_EXPERTS), (0, 0)))
        rw_hi = rw.astype(BF16)
        row = lambda a: a.reshape(1, -1)
        lw = dict(
            w_gates=w_in[l][:, PRE_COLS:].astype(BF16),
            conv_w=conv_w[l], conv_b=row(conv_b[l]), conv_ln_g=row(conv_ln_g[l]), conv_ln_b=row(conv_ln_b[l]),
            w_conv_out=w_conv_out[l].astype(BF16), w_attn_out=w_attn_out[l].astype(BF16),
            w_fourier_out=w_fourier_out[l].astype(BF16), w_out=w_out[l].astype(BF16),
            ln1_g=row(post_ln_g[l, 0]), ln1_b=row(post_ln_b[l, 0]),
            router_hi=rw_hi, router_lo=(rw - rw_hi.astype(F32)).astype(BF16),
            router_bias=jnp.broadcast_to(
                jnp.pad(router_bias[l], (0, LANES - N_EXPERTS), constant_values=-jnp.inf)[:, None],
                (LANES, TOKEN_TILE)),
            exp_w_gate=exp_w_gate, exp_w_up=exp_w_up, exp_w_down=exp_w_down,
            shared_w_gate=shared_w_gate[l].astype(BF16), shared_w_up=shared_w_up[l].astype(BF16),
            shared_w_down=shared_w_down[l].astype(BF16))
        mod_l = mod[l].reshape(MOD_ROWS, 1, 6 * d)
        sink = attn_sink[l]

        z, q_c, q_l, k_c, k_l, v_c, v_l, uc_c, uc_l, us_c, us_l = _premix(
            st, y_ctx, y_lat, mod_l, w_in[l][:, :PRE_COLS].astype(BF16), cc, sc, rope_cos, rope_sin)
        new_k.append(k_c.reshape(b_ctx, s_ctx, N_KV_HEADS, HEAD_DIM))
        new_v.append(v_c.reshape(b_ctx, s_ctx, N_KV_HEADS, HEAD_DIM))
        attn_c = _ctx_attention(sink, q_c, k_c, v_c, b_ctx, s_ctx)
        attn_l = _lat_attention(sink, q_l, k_l, v_l, cache_k[:, l].reshape(-1, KV_DIM),
                                cache_v[:, l].reshape(-1, KV_DIM), b_lat, s_lat)
        four_c = _fourier_positions(cn_ctx, sn_ctx, uc_c, us_c, b_ctx)
        four_l = _fourier_positions(cn_lat, sn_lat, uc_l, us_l, b_lat)
        r0, xs_local, pos, wts, tab, counts = _postmix(st, y_ctx, y_lat, mod_l, z, attn_c, attn_l, four_c, four_l, lw)
        tab = tab.reshape(-1)
        pstart, ys_sorted = _moe_routed(st, xs_local, tab, counts, lw, l)
        y_ctx, y_lat = _combine(st, pstart, tab, r0, pos, wts, mod_l, row(post_ln_g[l, 1]), row(post_ln_b[l, 1]),
                                ys_sorted)

    return (y_ctx.reshape(b_ctx, s_ctx, d), y_lat.reshape(b_lat, s_lat, d),
            jnp.stack(new_k, axis=1), jnp.stack(new_v, axis=1))
```

```python
import functools

import jax
import jax.numpy as jnp
from jax import lax
from jax.experimental import pallas as pl
from jax.experimental.pallas import tpu as pltpu

F32 = jnp.float32
BF16 = jnp.bfloat16
I32 = jnp.int32
U32 = jnp.uint32

D_MODEL = 1024
DEPTH = 2
GRID_W = 64
CONV_DIM = 256
CONV_WIDTH = 31
CONV_PAD = CONV_WIDTH // 2
CONV_HALO = 16
N_HEADS = 8
N_KV_HEADS = 2
HEAD_DIM = 64
Q_PER_KV = N_HEADS // N_KV_HEADS
Q_DIM = N_HEADS * HEAD_DIM
KV_DIM = N_KV_HEADS * HEAD_DIM
WINDOW = 128
ATTN_BLOCK = 128
ATTN_SCALE = HEAD_DIM ** -0.5
ROPE_AXIS_DIM = HEAD_DIM // 2
ROPE_HALF = ROPE_AXIS_DIM // 2
ROPE_BASE = 10000.0
NEG_INF = -1e30
FOURIER_GROUPS = 4
FOURIER_GROUP_DIM = 64
FOURIER_DIM = FOURIER_GROUPS * FOURIER_GROUP_DIM
PRE_COLS = 2 * CONV_DIM + Q_DIM + 2 * KV_DIM + FOURIER_DIM
N_EXPERTS = 64
TOP_K = 6
EXPERT_DIM = 256
ROUTED_SCALE = 2.5
LN_EPS = 1e-6
DEEPNORM_ALPHA = (2 * DEPTH) ** 0.25

LANES = 128
SUBLANES = 8
MOD_ROWS = 16
TOKEN_TILE = 256
EXPERT_ROWS = 512
PIECE = SUBLANES
SORT_ROWS = -(-(TOKEN_TILE * TOP_K + N_EXPERTS * (PIECE - 1)) // 256) * 256
SORT_PIECES = SORT_ROWS // PIECE
TAB_COLS = LANES + SORT_PIECES
ATTN_STACK = Q_PER_KV
TAB_SIZE = SUBLANES * TAB_COLS
PIECE_UNROLL = 4
BLOCK_PIECES = EXPERT_ROWS // PIECE
SRC_BLOCK = 1024
assert SRC_BLOCK % BLOCK_PIECES == 0 and SRC_BLOCK // BLOCK_PIECES > 2
PACKED = D_MODEL // 2
VMEM_LIMIT = 56 * 1024 * 1024


def _layer_norm(x):
    mu = jnp.mean(x, axis=-1, keepdims=True)
    xc = x - mu
    var = jnp.mean(xc * xc, axis=-1, keepdims=True)
    return xc * lax.rsqrt(var + LN_EPS)


def _silu(x):
    return x * jax.nn.sigmoid(x)


def _dot(a, b):
    return jnp.dot(a, b, preferred_element_type=F32)


def _full(shape):
    return pl.BlockSpec(shape, lambda *_: (0,) * len(shape), pipeline_mode=pl.Buffered(1))


def _pack_rows(y, rounded=False):
    bits = lax.bitcast_convert_type(y if rounded else y.astype(BF16).astype(F32), U32)
    return (bits[:, :PACKED] >> 16) | bits[:, PACKED:]


def _unpack_rows(u):
    lo = lax.bitcast_convert_type(u << 16, F32)
    hi = lax.bitcast_convert_type(u & jnp.uint32(0xFFFF0000), F32)
    return jnp.concatenate([lo, hi], axis=1).astype(BF16)


class _Stream:
    def __init__(self, b_ctx, s_ctx, b_lat, s_lat):
        self.tm = TOKEN_TILE
        assert s_ctx % self.tm == 0 and s_lat % self.tm == 0
        self.b_ctx, self.s_ctx, self.b_lat, self.s_lat = b_ctx, s_ctx, b_lat, s_lat
        self.n_ctx = b_ctx * s_ctx
        self.n = self.n_ctx + b_lat * s_lat
        self.tpb_c = s_ctx // self.tm
        self.tpb_l = s_lat // self.tm
        self.nct = self.n_ctx // self.tm
        self.tiles = self.n // self.tm
        self.ctx_row = b_lat

    def mod_row(self, i):
        return jnp.where(i >= self.nct, (i - self.nct) // self.tpb_l, self.ctx_row)

    def tile_in_seq(self, i):
        return jnp.where(i >= self.nct, (i - self.nct) % self.tpb_l, i % self.tpb_c)

    def tiles_per_seq(self, i):
        return jnp.where(i >= self.nct, self.tpb_l, self.tpb_c)

    def rope_block(self, i):
        return jnp.where(i >= self.nct, self.tpb_c + (i - self.nct) % self.tpb_l, i % self.tpb_c)


def _mod_kernel(c_ref, w_ref, b_ref, o_ref):
    o_ref[0] = _dot(_silu(c_ref[...]), w_ref[0]) + b_ref[0]


def _modulation(cond, w_ada, b_ada):
    depth, d, cols = w_ada.shape
    cb = cols // 4
    return pl.pallas_call(
        _mod_kernel,
        out_shape=jax.ShapeDtypeStruct((depth, MOD_ROWS, cols), F32),
        grid=(depth, cols // cb),
        in_specs=[pl.BlockSpec((MOD_ROWS, d), lambda l, j: (0, 0)),
                  pl.BlockSpec((1, d, cb), lambda l, j: (l, 0, j)),
                  pl.BlockSpec((1, 1, cb), lambda l, j: (l, 0, j))],
        out_specs=pl.BlockSpec((1, MOD_ROWS, cb), lambda l, j: (l, 0, j)),
        compiler_params=pltpu.CompilerParams(vmem_limit_bytes=VMEM_LIMIT),
        name="modulation",
    )(cond, w_ada, b_ada.reshape(depth, 1, cols))


def _premix_kernel(xc_ref, xl_ref, mod_ref, w_ref, cc_ref, sc_ref, cos_ref, sin_ref, z_ref, *outs, nct):
    d = D_MODEL
    is_lat = pl.program_id(0) >= nct
    mod = mod_ref[0]
    h = _layer_norm(jnp.where(is_lat, xl_ref[...], xc_ref[...])) * (1.0 + mod[:, d:2 * d]) + mod[:, 0:d]
    p = _dot(h.astype(BF16), w_ref[...])
    c0 = CONV_DIM
    z_ref[...] = (p[:, 0:c0] * jax.nn.sigmoid(p[:, c0:2 * c0])).astype(z_ref.dtype)
    q0 = 2 * c0
    k0 = q0 + Q_DIM
    v0 = k0 + KV_DIM
    f0 = v0 + KV_DIM
    cos = cos_ref[...]
    sin = sin_ref[...]
    first = (lax.broadcasted_iota(I32, cos.shape, 1) % ROPE_AXIS_DIM) < ROPE_HALF

    def rot(t):
        sw = jnp.where(first, pltpu.roll(t, LANES - ROPE_HALF, 1), pltpu.roll(t, ROPE_HALF, 1))
        return t * cos + sw * sin

    q = p[:, q0:k0] * ATTN_SCALE
    q = jnp.concatenate([rot(q[:, j * LANES:(j + 1) * LANES]) for j in range(Q_DIM // LANES)], axis=1)
    uf = p[:, f0:f0 + FOURIER_DIM].astype(BF16)
    results = [q.astype(BF16), rot(p[:, k0:v0]), p[:, v0:f0],
               _dot(uf, cc_ref[...]).astype(BF16), _dot(uf, sc_ref[...]).astype(BF16)]

    @pl.when(jnp.logical_not(is_lat))
    def _():
        for r, o_ref in zip(results, outs[0::2]):
            o_ref[...] = r

    @pl.when(is_lat)
    def _():
        for r, o_ref in zip(results, outs[1::2]):
            o_ref[...] = r


def _premix(st, x_ctx, x_lat, mod_l, w_pre, cc, sc, rope_cos, rope_sin):
    d = x_ctx.shape[1]
    tm = st.tm
    tok = lambda w: pl.BlockSpec((tm, w), lambda i: (i, 0))
    ctx_tok = lambda w: pl.BlockSpec((tm, w), lambda i: (jnp.minimum(i, st.nct - 1), 0))
    lat_tok = lambda w: pl.BlockSpec((tm, w), lambda i: (jnp.maximum(i - st.nct, 0), 0))
    rope = pl.BlockSpec((tm, LANES), lambda i: (st.rope_block(i), 0))
    n_lat = st.n - st.n_ctx
    out_shape = [jax.ShapeDtypeStruct((st.n, CONV_DIM), BF16)]
    out_specs = [tok(CONV_DIM)]
    for w, dt in ((Q_DIM, BF16), (KV_DIM, F32), (KV_DIM, F32), (FOURIER_DIM, BF16), (FOURIER_DIM, BF16)):
        out_shape += [jax.ShapeDtypeStruct((st.n_ctx, w), dt), jax.ShapeDtypeStruct((n_lat, w), dt)]
        out_specs += [ctx_tok(w), lat_tok(w)]
    return pl.pallas_call(
        functools.partial(_premix_kernel, nct=st.nct),
        out_shape=out_shape,
        grid=(st.tiles,),
        in_specs=[ctx_tok(d), lat_tok(d), pl.BlockSpec((1, 1, 6 * d), lambda i: (st.mod_row(i), 0, 0)),
                  _full(w_pre.shape), _full(cc.shape), _full(sc.shape), rope, rope],
        out_specs=out_specs,
        compiler_params=pltpu.CompilerParams(dimension_semantics=("arbitrary",), vmem_limit_bytes=VMEM_LIMIT),
        name="premix",
    )(x_ctx, x_lat, mod_l, w_pre, cc, sc, rope_cos, rope_sin)


def _sink_softmax_pv(s, sink_col, v):
    m = jnp.maximum(jnp.max(s, axis=1, keepdims=True), sink_col)
    p = jnp.exp(s - m)
    denom = jnp.sum(p, axis=1, keepdims=True) + jnp.exp(sink_col - m)
    return _dot(p.astype(BF16), v) / denom


def _attend_heads(sink_ref, q_ref, keys, vals, bias, o_ref):
    rows = q_ref.shape[0]
    for first in range(0, N_HEADS, ATTN_STACK):
        h = first // Q_PER_KV
        heads = list(range(first, first + ATTN_STACK))
        q = jnp.concatenate([q_ref[:, hd * HEAD_DIM:(hd + 1) * HEAD_DIM] for hd in heads], axis=0)
        s = lax.dot_general(q, keys[:, h * HEAD_DIM:(h + 1) * HEAD_DIM], (((1,), (1,)), ((), ())),
                            preferred_element_type=F32)
        if bias is not None:
            s = s + bias
        sink_col = jnp.concatenate([jnp.full((rows, 1), sink_ref[hd], F32) for hd in heads], axis=0)
        o = _sink_softmax_pv(s, sink_col, vals[:, h * HEAD_DIM:(h + 1) * HEAD_DIM])
        for g, hd in enumerate(heads):
            o_ref[:, hd * HEAD_DIM:(hd + 1) * HEAD_DIM] = o[g * rows:(g + 1) * rows].astype(o_ref.dtype)


def _ctx_attn_kernel(sink_ref, q_ref, k_ref, v_ref, o_ref):
    _attend_heads(sink_ref, q_ref, k_ref[...].astype(BF16), v_ref[...].astype(BF16), None, o_ref)


def _ctx_attention(sink, q, k, v, batch, seq):
    return pl.pallas_call(
        _ctx_attn_kernel,
        out_shape=jax.ShapeDtypeStruct((batch * seq, Q_DIM), BF16),
        grid=(batch,),
        in_specs=[pl.BlockSpec(memory_space=pltpu.SMEM),
                  pl.BlockSpec((seq, Q_DIM), lambda i: (i, 0)),
                  pl.BlockSpec((seq, KV_DIM), lambda i: (i, 0)),
                  pl.BlockSpec((seq, KV_DIM), lambda i: (i, 0))],
        out_specs=pl.BlockSpec((seq, Q_DIM), lambda i: (i, 0)),
        compiler_params=pltpu.CompilerParams(vmem_limit_bytes=VMEM_LIMIT),
        name="attn_ctx",
    )(sink, q, k, v)


def _lat_attn_kernel(sink_ref, q_ref, k_ref, v_ref, kc_ref, vc_ref, bias_ref, o_ref):
    i = pl.program_id(1)
    nblk = pl.num_programs(1)
    blk = ATTN_BLOCK
    starts = [pl.multiple_of(jnp.maximum(i - 1, 0) * blk, blk),
              pl.multiple_of(i * blk, blk),
              pl.multiple_of(jnp.minimum(i + 1, nblk - 1) * blk, blk)]
    keys = jnp.concatenate([k_ref[pl.ds(st, blk), :] for st in starts] + [kc_ref[...]], axis=0).astype(BF16)
    vals = jnp.concatenate([v_ref[pl.ds(st, blk), :] for st in starts] + [vc_ref[...]], axis=0).astype(BF16)
    _attend_heads(sink_ref, q_ref, keys, vals, bias_ref[0], o_ref)


def _window_bias(n_ctx_keys):
    blk = ATTN_BLOCK
    r = jnp.arange(blk)[:, None]
    j = jnp.arange(3 * blk)[None, :]
    band = (j >= r) & (j <= r + 2 * WINDOW)
    has_prev = j >= blk
    has_next = j < 2 * blk
    variants = [band & has_prev, band, band & has_next]
    out = []
    for m in variants:
        full = jnp.concatenate([m, jnp.ones((blk, n_ctx_keys), bool)], axis=1)
        out.append(jnp.tile(jnp.where(full, 0.0, NEG_INF).astype(F32), (ATTN_STACK, 1)))
    return jnp.stack(out)


def _lat_attention(sink, q, k, v, kc, vc, batch, seq):
    blk = ATTN_BLOCK
    nblk = seq // blk
    assert nblk >= 2 and WINDOW == blk
    pc = kc.shape[0] // batch
    bias = _window_bias(pc)
    nkeys = 3 * blk + pc
    bias_map = lambda i, j: (jnp.where(j == 0, 0, jnp.where(j == nblk - 1, 2, 1)), 0, 0)
    return pl.pallas_call(
        _lat_attn_kernel,
        out_shape=jax.ShapeDtypeStruct((batch * seq, Q_DIM), BF16),
        grid=(batch, nblk),
        in_specs=[pl.BlockSpec(memory_space=pltpu.SMEM),
                  pl.BlockSpec((blk, Q_DIM), lambda i, j: (i * nblk + j, 0)),
                  pl.BlockSpec((seq, KV_DIM), lambda i, j: (i, 0)),
                  pl.BlockSpec((seq, KV_DIM), lambda i, j: (i, 0)),
                  pl.BlockSpec((pc, KV_DIM), lambda i, j: (i, 0)),
                  pl.BlockSpec((pc, KV_DIM), lambda i, j: (i, 0)),
                  pl.BlockSpec((1, ATTN_STACK * blk, nkeys), bias_map)],
        out_specs=pl.BlockSpec((blk, Q_DIM), lambda i, j: (i * nblk + j, 0)),
        compiler_params=pltpu.CompilerParams(vmem_limit_bytes=VMEM_LIMIT),
        name="attn_lat",
    )(sink, q, k, v, kc, vc, bias)


def _fourier_kernel(cn_ref, sn_ref, uc_ref, us_ref, o_ref):
    o_ref[...] = (_dot(cn_ref[...], uc_ref[...]) - _dot(sn_ref[...], us_ref[...])).astype(o_ref.dtype)


def _fourier_positions(cn, sn, uc, us, nb):
    seq = cn.shape[0]
    tf = min(512, seq)
    jt = seq // tf
    return pl.pallas_call(
        _fourier_kernel,
        out_shape=jax.ShapeDtypeStruct((nb * seq, FOURIER_DIM), BF16),
        grid=(jt, nb),
        in_specs=[pl.BlockSpec((tf, seq), lambda j, b: (j, 0)),
                  pl.BlockSpec((tf, seq), lambda j, b: (j, 0)),
                  pl.BlockSpec((seq, FOURIER_DIM), lambda j, b: (b, 0)),
                  pl.BlockSpec((seq, FOURIER_DIM), lambda j, b: (b, 0))],
        out_specs=pl.BlockSpec((tf, FOURIER_DIM), lambda j, b: (b * jt + j, 0)),
        compiler_params=pltpu.CompilerParams(vmem_limit_bytes=VMEM_LIMIT),
        name="fourier",
    )(cn, sn, uc, us)


def _postmix_kernel(xc_ref, xl_ref, mod_ref, z_ref, zp_ref, zn_ref, attn_c_ref, attn_l_ref, four_c_ref, four_l_ref,
                    wg_ref, cw_ref, cb_ref, cg_ref,
                    cbeta_ref, wco_ref, wao_ref, wfo_ref, wo_ref, g1_ref, b1_ref, rwh_ref, rwl_ref, rb_ref,
                    sg_ref, su_ref, sd_ref,
                    r0_ref, xs_ref, pos_ref, wts_ref, tab_ref, cnt_ref, zs_ref, carry_ref, *, st):
    d = D_MODEL
    i = pl.program_id(0)
    tm = xc_ref.shape[0]
    is_lat = i >= st.nct
    attn = jnp.where(is_lat, attn_l_ref[...], attn_c_ref[...])
    four = jnp.where(is_lat, four_l_ref[...], four_c_ref[...])
    x = jnp.where(is_lat, xl_ref[...], xc_ref[...])
    mod = mod_ref[0]
    shift1, scale1, gate1, shift2, scale2, gate2 = [mod[:, j * d:(j + 1) * d] for j in range(6)]
    h = (_layer_norm(x) * (1.0 + scale1) + shift1).astype(BF16)
    gates = jax.nn.sigmoid(_dot(h, wg_ref[...]))

    tib = st.tile_in_seq(i)
    zs_ref[0:CONV_HALO, :] = zp_ref[...].astype(F32) * jnp.where(tib != 0, 1.0, 0.0)
    zs_ref[CONV_HALO + tm:, :] = zn_ref[...].astype(F32) * jnp.where(tib != st.tiles_per_seq(i) - 1, 1.0, 0.0)
    zs_ref[CONV_HALO:CONV_HALO + tm, :] = z_ref[...].astype(F32)
    acc = jnp.broadcast_to(cb_ref[...], (tm, CONV_DIM))
    first_off = CONV_HALO - CONV_PAD
    window = zs_ref[...]
    rows = window.shape[0]
    for r in range(SUBLANES):
        offs = [o for o in range(first_off, first_off + CONV_WIDTH) if o % SUBLANES == r]
        shifted = window if r == 0 else pltpu.roll(window, rows - r, 0)
        for o in offs:
            acc = acc + cw_ref[o - first_off:o - first_off + 1, :] * shifted[o - r:o - r + tm]
    conv = _silu(_layer_norm(acc) * cg_ref[...] + cbeta_ref[...]).astype(BF16)

    merged = (gates[:, 0:d] * _dot(conv, wco_ref[...])
              + gates[:, d:2 * d] * _dot(attn, wao_ref[...])
              + gates[:, 2 * d:3 * d] * _dot(four, wfo_ref[...]))
    mix = _dot(merged.astype(BF16), wo_ref[...])
    x1 = _layer_norm(DEEPNORM_ALPHA * x + gate1 * mix) * g1_ref[...] + b1_ref[...]
    h2 = _layer_norm(x1) * (1.0 + scale2) + shift2

    h2_hi = h2.astype(BF16)
    h2_lo = (h2 - h2_hi.astype(F32)).astype(BF16)
    nt = (((1,), (1,)), ((), ()))
    logits_t = (lax.dot_general(rwh_ref[...], h2_hi, nt, preferred_element_type=F32)
                + lax.dot_general(rwh_ref[...], h2_lo, nt, preferred_element_type=F32)
                + lax.dot_general(rwl_ref[...], h2_hi, nt, preferred_element_type=F32))
    scores_t = jax.nn.sigmoid(logits_t)
    sel_t = scores_t + rb_ref[...]
    expert_t = lax.broadcasted_iota(I32, (LANES, tm), 0)
    onehots_t, top_s = [], []
    for _ in range(TOP_K):
        mx = jnp.max(sel_t, axis=0, keepdims=True)
        idx = jnp.min(jnp.where(sel_t == mx, expert_t, LANES), axis=0, keepdims=True)
        oh = expert_t == idx
        onehots_t.append(oh)
        top_s.append(jnp.sum(jnp.where(oh, scores_t, 0.0), axis=0, keepdims=True))
        sel_t = jnp.where(oh, -jnp.inf, sel_t)
    total = top_s[0]
    for s_k in top_s[1:]:
        total = total + s_k

    @pl.when(i == 0)
    def _():
        carry_ref[...] = jnp.zeros_like(carry_ref)

    chosen_t = jnp.zeros((LANES, tm), F32)
    for oh in onehots_t:
        chosen_t = jnp.where(oh, 1.0, chosen_t)
    chosen_tb = chosen_t.astype(BF16)
    tok_r = lax.broadcasted_iota(I32, (tm, tm), 0)
    tok_c = lax.broadcasted_iota(I32, (tm, tm), 1)
    prefix_t = _dot(chosen_tb, jnp.where(tok_r < tok_c, 1.0, 0.0).astype(BF16))
    cpad_t = jnp.floor((jnp.sum(chosen_t, axis=1, keepdims=True) + (PIECE - 1)) * (1.0 / PIECE)) * PIECE
    exp_r = lax.broadcasted_iota(I32, (LANES, LANES), 0)
    exp_c = lax.broadcasted_iota(I32, (LANES, LANES), 1)
    cstart_t = _dot(jnp.where(exp_c < exp_r, 1.0, 0.0).astype(BF16),
                    jnp.broadcast_to(cpad_t, (LANES, tm)).astype(BF16))

    where_t = prefix_t + cstart_t
    krow = lax.broadcasted_iota(I32, (LANES, tm), 0)
    pos_t = jnp.zeros((LANES, tm), F32)
    wts_t = jnp.zeros((LANES, tm), F32)
    for k, oh in enumerate(onehots_t):
        pos_t = jnp.where(krow == k, jnp.sum(jnp.where(oh, where_t, 0.0), axis=0, keepdims=True), pos_t)
        wts_t = jnp.where(krow == k, ROUTED_SCALE * top_s[k] / total, wts_t)
    pos_ref[...] = pos_t.T.astype(I32)
    wts_ref[...] = wts_t.T

    chosen = chosen_t.T
    count = jnp.sum(chosen, axis=0, keepdims=True)
    cpad = jnp.floor((count + (PIECE - 1)) * (1.0 / PIECE)) * PIECE
    er = lax.broadcasted_iota(I32, (LANES, LANES), 0)
    ec = lax.broadcasted_iota(I32, (LANES, LANES), 1)
    cstart = _dot(jnp.broadcast_to(cpad, (SUBLANES, LANES)), jnp.where(er < ec, 1.0, 0.0))[0:1]
    gbase = carry_ref[...]
    carry_ref[...] = gbase + cpad
    cnt_ref[...] = gbase + cpad
    pieces = jnp.sum(cpad, axis=1, keepdims=True) * (1.0 / PIECE)
    trow = lax.broadcasted_iota(I32, (SUBLANES, LANES), 0)
    per_expert = jnp.where(trow == 0, cstart, jnp.where(trow == 1, cpad, jnp.where(trow == 2, gbase, pieces)))
    piece_row = (lax.broadcasted_iota(I32, (SORT_PIECES, LANES), 0) * PIECE).astype(F32)
    owns = jnp.where((piece_row >= cstart) & (piece_row < cstart + cpad), 1.0, 0.0).astype(BF16)
    expert_id = lax.broadcasted_iota(I32, (SUBLANES, LANES), 1).astype(BF16)
    piece_expert = lax.dot_general(expert_id, owns, (((1,), (1,)), ((), ())), preferred_element_type=F32)
    tab_ref[...] = jnp.concatenate([per_expert, piece_expert], axis=1).astype(I32)

    slot_e = lax.broadcasted_iota(I32, (SORT_ROWS, LANES), 0).astype(F32)
    slot_expert = jnp.where((slot_e >= cstart) & (slot_e < cstart + cpad), 1.0, 0.0).astype(BF16)
    rank_t = jnp.where(chosen_t > 0.0, prefix_t, -1.0)
    looked = _dot(slot_expert, jnp.concatenate([rank_t.astype(BF16), cstart_t.astype(BF16)], axis=1))
    slot_t = lax.broadcasted_iota(I32, (SORT_ROWS, tm), 0).astype(F32)
    perm = jnp.where(looked[:, :tm] + looked[:, tm:] == slot_t, 1.0, 0.0).astype(BF16)
    h2b = h2.astype(BF16)
    xs_ref[...] = _pack_rows(_dot(perm, h2b), rounded=True)

    shared = _dot((_silu(_dot(h2b, sg_ref[...])) * _dot(h2b, su_ref[...])).astype(BF16), sd_ref[...])
    r0_ref[...] = DEEPNORM_ALPHA * x1 + gate2 * shared


def _postmix(st, x_ctx, x_lat, mod_l, z, attn_c, attn_l, four_c, four_l, lw):
    n, d = st.n, x_ctx.shape[1]
    tm = st.tm
    hb = tm // CONV_HALO
    last_halo = n // CONV_HALO - 1
    tok = lambda w: pl.BlockSpec((tm, w), lambda i: (i, 0))
    ctx_tok = lambda w: pl.BlockSpec((tm, w), lambda i: (jnp.minimum(i, st.nct - 1), 0))
    lat_tok = lambda w: pl.BlockSpec((tm, w), lambda i: (jnp.maximum(i - st.nct, 0), 0))
    in_specs = [ctx_tok(d), lat_tok(d), pl.BlockSpec((1, 1, 6 * d), lambda i: (st.mod_row(i), 0, 0)), tok(CONV_DIM),
                pl.BlockSpec((CONV_HALO, CONV_DIM), lambda i: (jnp.maximum(i * hb - 1, 0), 0)),
                pl.BlockSpec((CONV_HALO, CONV_DIM), lambda i: (jnp.minimum((i + 1) * hb, last_halo), 0)),
                ctx_tok(Q_DIM), lat_tok(Q_DIM), ctx_tok(FOURIER_DIM), lat_tok(FOURIER_DIM)]
    args = [x_ctx, x_lat, mod_l, z, z, z, attn_c, attn_l, four_c, four_l]
    weights = [lw['w_gates'], lw['conv_w'], lw['conv_b'], lw['conv_ln_g'], lw['conv_ln_b'], lw['w_conv_out'],
               lw['w_attn_out'], lw['w_fourier_out'], lw['w_out'], lw['ln1_g'], lw['ln1_b'],
               lw['router_hi'], lw['router_lo'], lw['router_bias'],
               lw['shared_w_gate'], lw['shared_w_up'], lw['shared_w_down']]
    in_specs += [_full(w.shape) for w in weights]
    args += weights
    return pl.pallas_call(
        functools.partial(_postmix_kernel, st=st),
        out_shape=[jax.ShapeDtypeStruct((n, d), F32),
                   jax.ShapeDtypeStruct((st.tiles * SORT_ROWS, PACKED), U32),
                   jax.ShapeDtypeStruct((n, LANES), I32),
                   jax.ShapeDtypeStruct((n, LANES), F32),
                   jax.ShapeDtypeStruct((st.tiles * SUBLANES, TAB_COLS), I32),
                   jax.ShapeDtypeStruct((1, LANES), F32)],
        grid=(st.tiles,),
        in_specs=in_specs,
        out_specs=[tok(d), pl.BlockSpec((SORT_ROWS, PACKED), lambda i: (i, 0)), tok(LANES), tok(LANES),
                   pl.BlockSpec((SUBLANES, TAB_COLS), lambda i: (i, 0)), _full((1, LANES))],
        scratch_shapes=[pltpu.VMEM((tm + 2 * CONV_HALO, CONV_DIM), F32),
                        pltpu.VMEM((1, LANES), F32)],
        compiler_params=pltpu.CompilerParams(dimension_semantics=("arbitrary",), vmem_limit_bytes=VMEM_LIMIT),
        name="postmix",
    )(*args)


RING = 3


def _tab(tab_ref, r, c):
    return tab_ref[r * TAB_COLS + c]


def _for_each_piece(pstart_ref, tab_ref, fn):
    def one(p, prio):
        e = _tab(tab_ref, 0, LANES + p)
        local_row = p * PIECE
        sorted_row = pstart_ref[e] + _tab(tab_ref, 2, e) + (local_row - _tab(tab_ref, 0, e))
        fn(pl.multiple_of(local_row, PIECE), pl.multiple_of(sorted_row, PIECE), prio)

    n = _tab(tab_ref, 3, 0)
    groups = n // PIECE_UNROLL

    def group(g, c):
        for u in range(PIECE_UNROLL):
            one(g * PIECE_UNROLL + u, u % 2)
        return c

    lax.fori_loop(0, groups, group, 0)
    lax.fori_loop(groups * PIECE_UNROLL, n, lambda p, c: (one(p, 0), c)[1], 0)


def _wait_pieces(n_pieces, src_ref, dst_ref, sem):
    @pl.when(n_pieces > 0)
    def _():
        rows = n_pieces * PIECE
        pltpu.make_async_copy(src_ref.at[pl.ds(0, rows)], dst_ref.at[pl.ds(0, rows)], sem).wait()


def _experts_kernel(be_ref, nu_ref, src_ref, xs_ref, wg_ref, wu_ref, wd_ref, ys_ref, xbuf_ref, sem, wgb_ref, wub_ref,
                    wdb_ref):
    j = pl.program_id(0)
    n_used = nu_ref[0]
    used = j < n_used

    def gather(block):
        slot = block % RING
        first = (block * BLOCK_PIECES) % SRC_BLOCK

        def group(g, c):
            for u in range(PIECE_UNROLL):
                p = g * PIECE_UNROLL + u
                src_row = pl.multiple_of(src_ref[first + p], PIECE)
                pltpu.make_async_copy(xs_ref.at[pl.ds(src_row, PIECE)],
                                      xbuf_ref.at[slot, pl.ds(pl.multiple_of(p * PIECE, PIECE), PIECE)],
                                      sem.at[slot]).start(priority=u % 2)
            return c

        lax.fori_loop(0, BLOCK_PIECES // PIECE_UNROLL, group, 0)

    @pl.when(j == 0)
    def _():
        gather(0)

        @pl.when(n_used > 1)
        def _():
            gather(1)

    @pl.when(j + 2 < n_used)
    def _():
        gather(j + 2)

    @pl.when(jnp.logical_or(j == 0, be_ref[j] != be_ref[jnp.maximum(j - 1, 0)]))
    def _():
        wgb_ref[...] = wg_ref[0, 0].astype(BF16)
        wub_ref[...] = wu_ref[0, 0].astype(BF16)
        wdb_ref[...] = wd_ref[0, 0].astype(BF16)

    @pl.when(used)
    def _():
        slot = j % RING
        pltpu.make_async_copy(xs_ref.at[pl.ds(0, EXPERT_ROWS)], xbuf_ref.at[slot], sem.at[slot]).wait()
        x = _unpack_rows(xbuf_ref[slot])
        a = (_silu(_dot(x, wgb_ref[...])) * _dot(x, wub_ref[...])).astype(BF16)
        ys_ref[...] = _pack_rows(_dot(a, wdb_ref[...]))

    @pl.when(jnp.logical_not(used))
    def _():
        ys_ref[...] = jnp.zeros_like(ys_ref)


def _experts(block_e, n_used, src_rows, xs_local, n_blocks, w_gate, w_up, w_down, layer):
    d = w_gate.shape[2]
    r = EXPERT_ROWS
    n_rows = n_blocks * r
    last_table = src_rows.shape[0] // SRC_BLOCK - 1
    ahead = lambda j, be, nu: (jnp.minimum((j + 2) * BLOCK_PIECES // SRC_BLOCK, last_table),)
    return pl.pallas_call(
        _experts_kernel,
        out_shape=jax.ShapeDtypeStruct((n_rows, PACKED), U32),
        grid_spec=pltpu.PrefetchScalarGridSpec(
            num_scalar_prefetch=2,
            grid=(n_blocks,),
            in_specs=[pl.BlockSpec((SRC_BLOCK,), ahead, memory_space=pltpu.SMEM),
                      pl.BlockSpec(memory_space=pl.ANY),
                      pl.BlockSpec((1, 1, d, EXPERT_DIM), lambda j, be, nu: (layer, be[j], 0, 0)),
                      pl.BlockSpec((1, 1, d, EXPERT_DIM), lambda j, be, nu: (layer, be[j], 0, 0)),
                      pl.BlockSpec((1, 1, EXPERT_DIM, d), lambda j, be, nu: (layer, be[j], 0, 0))],
            out_specs=pl.BlockSpec((r, PACKED), lambda j, be, nu: (j, 0)),
            scratch_shapes=[pltpu.VMEM((RING, r, PACKED), U32), pltpu.SemaphoreType.DMA((RING,)),
                            pltpu.VMEM((d, EXPERT_DIM), BF16), pltpu.VMEM((d, EXPERT_DIM), BF16),
                            pltpu.VMEM((EXPERT_DIM, d), BF16)]),
        compiler_params=pltpu.CompilerParams(dimension_semantics=("arbitrary",), vmem_limit_bytes=VMEM_LIMIT),
        name="moe_experts",
    )(block_e, n_used, src_rows, xs_local, w_gate, w_up, w_down)


def _combine_kernel(pstart_ref, tab_ref, tab_next_ref, r0_ref, pos_ref, wts_ref, mod_ref, g2_ref, b2_ref, ys_ref,
                    oc_ref, ol_ref, buf_ref, sem, *, nct):
    tm, d = r0_ref.shape
    i = pl.program_id(0)
    slot = i % 2

    def piece_copy(slot_idx, local_row, sorted_row):
        return pltpu.make_async_copy(ys_ref.at[pl.ds(sorted_row, PIECE)],
                                     buf_ref.at[slot_idx, pl.ds(local_row, PIECE)], sem.at[slot_idx])

    @pl.when(i == 0)
    def _():
        buf_ref[...] = jnp.zeros_like(buf_ref)
        _for_each_piece(pstart_ref, tab_ref, lambda a, g, prio: piece_copy(0, a, g).start(priority=prio))

    @pl.when(i + 1 < pl.num_programs(0))
    def _():
        _for_each_piece(pstart_ref, tab_next_ref,
                        lambda a, g, prio: piece_copy(1 - slot, a, g).start(priority=prio))

    _wait_pieces(_tab(tab_ref, 3, 0), ys_ref, buf_ref.at[slot], sem.at[slot])

    pos = pos_ref[...]
    wts = wts_ref[...]
    col = lax.broadcasted_iota(I32, (tm, SORT_ROWS), 1)
    weight = jnp.zeros((tm, SORT_ROWS), F32)
    for k in range(TOP_K):
        weight = jnp.where(col == pos[:, k:k + 1], wts[:, k:k + 1], weight)
    routed = _dot(weight.astype(BF16), _unpack_rows(buf_ref[slot]))
    gate2 = mod_ref[0][:, 5 * d:6 * d]
    out = _layer_norm(r0_ref[...] + gate2 * routed) * g2_ref[...] + b2_ref[...]
    @pl.when(i < nct)
    def _():
        oc_ref[...] = out

    @pl.when(i >= nct)
    def _():
        ol_ref[...] = out


def _combine(st, pstart, tab, r0, pos, wts, mod_l, ln2_g, ln2_b, ys_sorted):
    n, d = r0.shape
    tm = st.tm
    tok = lambda w: pl.BlockSpec((tm, w), lambda i, *_: (i, 0))
    last = st.tiles - 1
    out_shape = [jax.ShapeDtypeStruct((st.n_ctx, d), F32), jax.ShapeDtypeStruct((n - st.n_ctx, d), F32)]
    out_specs = [pl.BlockSpec((tm, d), lambda i, *_: (jnp.minimum(i, st.nct - 1), 0)),
                 pl.BlockSpec((tm, d), lambda i, *_: (jnp.maximum(i - st.nct, 0), 0))]
    return pl.pallas_call(
        functools.partial(_combine_kernel, nct=st.nct),
        out_shape=out_shape,
        grid_spec=pltpu.PrefetchScalarGridSpec(
            num_scalar_prefetch=1,
            grid=(st.tiles,),
            in_specs=[pl.BlockSpec((TAB_SIZE,), lambda i, *_: (i,), memory_space=pltpu.SMEM),
                      pl.BlockSpec((TAB_SIZE,), lambda i, *_: (jnp.minimum(i + 1, last),), memory_space=pltpu.SMEM),
                      tok(d), tok(LANES), tok(LANES),
                      pl.BlockSpec((1, 1, 6 * d), lambda i, *_: (st.mod_row(i), 0, 0)),
                      _full((1, d)), _full((1, d)),
                      pl.BlockSpec(memory_space=pl.ANY)],
            out_specs=out_specs,
            scratch_shapes=[pltpu.VMEM((2, SORT_ROWS, PACKED), U32), pltpu.SemaphoreType.DMA((2,))]),
        compiler_params=pltpu.CompilerParams(dimension_semantics=("arbitrary",), vmem_limit_bytes=VMEM_LIMIT),
        name="moe_combine",
    )(pstart, tab, tab, r0, pos, wts, mod_l, ln2_g, ln2_b, ys_sorted)


def _moe_routed(st, xs_local, tab, counts, lw, layer):
    r = EXPERT_ROWS
    n_blocks = -(-(st.tiles * (st.tm * TOP_K + N_EXPERTS * (PIECE - 1))) // r) + N_EXPERTS
    cnt = counts[0, :N_EXPERTS].astype(I32)
    padded = (cnt + r - 1) // r * r
    pend = jnp.cumsum(padded)
    pstart = (pend - padded).astype(I32)
    n_used = (pend[-1] // r).astype(I32)
    blocks = jnp.arange(n_blocks, dtype=I32)
    block_i = jnp.minimum(blocks, n_used - 1)
    block_e = jnp.sum((pend[None, :] <= (block_i * r)[:, None]).astype(I32), axis=1)
    block_e = jnp.minimum(block_e, N_EXPERTS - 1).astype(I32)
    t3 = tab.reshape(st.tiles, SUBLANES, TAB_COLS)
    piece_expert = t3[:, 0, LANES:]
    chunk_shift = jnp.pad(pstart, (0, LANES - N_EXPERTS))[None, :] + t3[:, 2, :LANES] - t3[:, 0, :LANES]
    local_row = jnp.arange(SORT_PIECES, dtype=I32)[None, :] * PIECE
    sorted_row = jnp.take_along_axis(chunk_shift, piece_expert, axis=1) + local_row
    n_entries = -(-(n_blocks * BLOCK_PIECES) // SRC_BLOCK) * SRC_BLOCK
    in_use = jnp.arange(SORT_PIECES, dtype=I32)[None, :] < t3[:, 3, 0:1]
    src_rows = jnp.full((n_entries,), SORT_ROWS - PIECE, I32).at[
        jnp.where(in_use, sorted_row // PIECE, n_entries).reshape(-1)].set(
        (jnp.arange(st.tiles, dtype=I32)[:, None] * SORT_ROWS + local_row).reshape(-1), mode='drop')
    ys = _experts(block_e, n_used.reshape(1), src_rows, xs_local, n_blocks,
                  lw['exp_w_gate'], lw['exp_w_up'], lw['exp_w_down'], layer)
    return pstart, ys


def _channel_dft():
    j = jnp.arange(FOURIER_DIM)
    same = (j[:, None] // FOURIER_GROUP_DIM) == (j[None, :] // FOURIER_GROUP_DIM)
    prod = ((j[:, None] % FOURIER_GROUP_DIM) * (j[None, :] % FOURIER_GROUP_DIM)) % FOURIER_GROUP_DIM
    ang = (2.0 * jnp.pi / FOURIER_GROUP_DIM) * prod.astype(F32)
    scale = FOURIER_GROUP_DIM ** -0.5
    return (jnp.where(same, jnp.cos(ang) * scale, 0.0).astype(BF16),
            jnp.where(same, jnp.sin(ang) * scale, 0.0).astype(BF16))


def _position_dft(n):
    m = 1 << (n.bit_length() // 2)
    assert n % m == 0
    k = jnp.arange(n, dtype=I32)[None, :]
    ang_hi = (2.0 * jnp.pi / n) * ((jnp.arange(n // m, dtype=I32)[:, None] * m * k) % n).astype(F32)
    ang_lo = (2.0 * jnp.pi / n) * ((jnp.arange(m, dtype=I32)[:, None] * k) % n).astype(F32)
    scale = n ** -0.5
    ch, sh = (jnp.cos(ang_hi) * scale)[:, None, :], (jnp.sin(ang_hi) * scale)[:, None, :]
    cl, sl = jnp.cos(ang_lo)[None, :, :], jnp.sin(ang_lo)[None, :, :]
    return ((ch * cl - sh * sl).reshape(n, n).astype(BF16), (sh * cl + ch * sl).reshape(n, n).astype(BF16))


def _rope_tables(s_ctx, n_lat):
    t = jnp.arange(n_lat)
    inv = ROPE_BASE ** (-jnp.arange(ROPE_HALF, dtype=F32) / ROPE_HALF)
    row_ang = (t // GRID_W).astype(F32)[:, None] * inv[None, :]
    col_ang = (t % GRID_W).astype(F32)[:, None] * inv[None, :]
    cos = jnp.concatenate([jnp.cos(row_ang)] * 2 + [jnp.cos(col_ang)] * 2, axis=1)
    sin = jnp.concatenate([-jnp.sin(row_ang), jnp.sin(row_ang), -jnp.sin(col_ang), jnp.sin(col_ang)], axis=1)
    reps = LANES // HEAD_DIM
    cos = jnp.concatenate([jnp.ones((s_ctx, LANES), F32), jnp.tile(cos, (1, reps))], axis=0)
    sin = jnp.concatenate([jnp.zeros((s_ctx, LANES), F32), jnp.tile(sin, (1, reps))], axis=0)
    return cos, sin


def kernel(x_prompt, x_sample, cache_k, cache_v, c, c_ctx, w_ada, b_ada, w_in, conv_w, conv_b, conv_ln_g, conv_ln_b, w_conv_out, attn_sink, w_attn_out, w_fourier_out, w_out, post_ln_g, post_ln_b, router_w, router_bias, exp_w_gate, exp_w_up, exp_w_down, shared_w_gate, shared_w_up, shared_w_down):
    b_ctx, s_ctx, d = x_prompt.shape
    b_lat, s_lat, _ = x_sample.shape
    depth = w_in.shape[0]
    assert b_lat < MOD_ROWS and d == D_MODEL
    st = _Stream(b_ctx, s_ctx, b_lat, s_lat)
    nc = st.n_ctx

    cond = jnp.concatenate([c, c_ctx[None, :], jnp.zeros((MOD_ROWS - b_lat - 1, d), F32)], axis=0)
    mod = _modulation(cond, w_ada, b_ada)

    cc, sc = _channel_dft()
    cn_ctx, sn_ctx = _position_dft(s_ctx)
    cn_lat, sn_lat = _position_dft(s_lat)
    rope_cos, rope_sin = _rope_tables(s_ctx, s_lat)

    y_ctx = x_prompt.reshape(nc, d)
    y_lat = x_sample.reshape(b_lat * s_lat, d)
    new_k, new_v = [], []
    for l in range(depth):
        rw = jnp.pad(router_w[l].T, ((0, LANES - N_EXPERTS), (0, 0)))
        rw_hi = rw.astype(BF16)
        row = lambda a: a.reshape(1, -1)
        lw = dict(
            w_gates=w_in[l][:, PRE_COLS:].astype(BF16),
            conv_w=conv_w[l], conv_b=row(conv_b[l]), conv_ln_g=row(conv_ln_g[l]), conv_ln_b=row(conv_ln_b[l]),
            w_conv_out=w_conv_out[l].astype(BF16), w_attn_out=w_attn_out[l].astype(BF16),
            w_fourier_out=w_fourier_out[l].astype(BF16), w_out=w_out[l].astype(BF16),
            ln1_g=row(post_ln_g[l, 0]), ln1_b=row(post_ln_b[l, 0]),
            router_hi=rw_hi, router_lo=(rw - rw_hi.astype(F32)).astype(BF16),
            router_bias=jnp.broadcast_to(
                jnp.pad(router_bias[l], (0, LANES - N_EXPERTS), constant_values=-jnp.inf)[:, None],
                (LANES, TOKEN_TILE)),
            exp_w_gate=exp_w_gate, exp_w_up=exp_w_up, exp_w_down=exp_w_down,
            shared_w_gate=shared_w_gate[l].astype(BF16), shared_w_up=shared_w_up[l].astype(BF16),
            shared_w_down=shared_w_down[l].astype(BF16))
        mod_l = mod[l].reshape(MOD_ROWS, 1, 6 * d)
        sink = attn_sink[l]

        z, q_c, q_l, k_c, k_l, v_c, v_l, uc_c, uc_l, us_c, us_l = _premix(
            st, y_ctx, y_lat, mod_l, w_in[l][:, :PRE_COLS].astype(BF16), cc, sc, rope_cos, rope_sin)
        new_k.append(k_c.reshape(b_ctx, s_ctx, N_KV_HEADS, HEAD_DIM))
        new_v.append(v_c.reshape(b_ctx, s_ctx, N_KV_HEADS, HEAD_DIM))
        attn_c = _ctx_attention(sink, q_c, k_c, v_c, b_ctx, s_ctx)
        attn_l = _lat_attention(sink, q_l, k_l, v_l, cache_k[:, l].reshape(-1, KV_DIM),
                                cache_v[:, l].reshape(-1, KV_DIM), b_lat, s_lat)
        four_c = _fourier_positions(cn_ctx, sn_ctx, uc_c, us_c, b_ctx)
        four_l = _fourier_positions(cn_lat, sn_lat, uc_l, us_l, b_lat)
        r0, xs_local, pos, wts, tab, counts = _postmix(st, y_ctx, y_lat, mod_l, z, attn_c, attn_l, four_c, four_l, lw)
        tab = tab.reshape(-1)
        pstart, ys_sorted = _moe_routed(st, xs_local, tab, counts, lw, l)
        y_ctx, y_lat = _combine(st, pstart, tab, r0, pos, wts, mod_l, row(post_ln_g[l, 1]), row(post_ln_b[l, 1]),
                                ys_sorted)

    return (y_ctx.reshape(b_ctx, s_ctx, d), y_lat.reshape(b_lat, s_lat, d),
            jnp.stack(new_k, axis=1), jnp.stack(new_v, axis=1))
```

```python
import functools

import jax
import jax.numpy as jnp
from jax import lax
from jax.experimental import pallas as pl
from jax.experimental.pallas import tpu as pltpu

F32 = jnp.float32
BF16 = jnp.bfloat16
I32 = jnp.int32
U32 = jnp.uint32

D_MODEL = 1024
DEPTH = 2
GRID_W = 64
CONV_DIM = 256
CONV_WIDTH = 31
CONV_PAD = CONV_WIDTH // 2
CONV_HALO = 16
N_HEADS = 8
N_KV_HEADS = 2
HEAD_DIM = 64
Q_PER_KV = N_HEADS // N_KV_HEADS
Q_DIM = N_HEADS * HEAD_DIM
KV_DIM = N_KV_HEADS * HEAD_DIM
WINDOW = 128
ATTN_BLOCK = 128
ATTN_SCALE = HEAD_DIM ** -0.5
ROPE_AXIS_DIM = HEAD_DIM // 2
ROPE_HALF = ROPE_AXIS_DIM // 2
ROPE_BASE = 10000.0
NEG_INF = -1e30
FOURIER_GROUPS = 4
FOURIER_GROUP_DIM = 64
FOURIER_DIM = FOURIER_GROUPS * FOURIER_GROUP_DIM
PRE_COLS = 2 * CONV_DIM + Q_DIM + 2 * KV_DIM + FOURIER_DIM
N_EXPERTS = 64
TOP_K = 6
EXPERT_DIM = 256
ROUTED_SCALE = 2.5
LN_EPS = 1e-6
DEEPNORM_ALPHA = (2 * DEPTH) ** 0.25

LANES = 128
SUBLANES = 8
MOD_ROWS = 16
TOKEN_TILE = 256
EXPERT_ROWS = 512
PIECE = SUBLANES
SORT_ROWS = -(-(TOKEN_TILE * TOP_K + N_EXPERTS * (PIECE - 1)) // 256) * 256
SORT_PIECES = SORT_ROWS // PIECE
TAB_COLS = LANES + SORT_PIECES
ATTN_STACK = Q_PER_KV
TAB_SIZE = SUBLANES * TAB_COLS
PIECE_UNROLL = 4
BLOCK_PIECES = EXPERT_ROWS // PIECE
SRC_BLOCK = 1024
assert SRC_BLOCK % BLOCK_PIECES == 0 and SRC_BLOCK // BLOCK_PIECES > 2
PACKED = D_MODEL // 2
VMEM_LIMIT = 56 * 1024 * 1024


def _layer_norm(x):
    mu = jnp.mean(x, axis=-1, keepdims=True)
    xc = x - mu
    var = jnp.mean(xc * xc, axis=-1, keepdims=True)
    return xc * lax.rsqrt(var + LN_EPS)


def _silu(x):
    return x * jax.nn.sigmoid(x)


def _dot(a, b):
    return jnp.dot(a, b, preferred_element_type=F32)


def _full(shape):
    return pl.BlockSpec(shape, lambda *_: (0,) * len(shape), pipeline_mode=pl.Buffered(1))


def _pack_rows(y, rounded=False):
    bits = lax.bitcast_convert_type(y if rounded else y.astype(BF16).astype(F32), U32)
    return (bits[:, :PACKED] >> 16) | bits[:, PACKED:]


def _unpack_rows(u):
    lo = lax.bitcast_convert_type(u << 16, F32)
    hi = lax.bitcast_convert_type(u & jnp.uint32(0xFFFF0000), F32)
    return jnp.concatenate([lo, hi], axis=1).astype(BF16)


class _Stream:
    def __init__(self, b_ctx, s_ctx, b_lat, s_lat):
        self.tm = TOKEN_TILE
        assert s_ctx % self.tm == 0 and s_lat % self.tm == 0
        self.b_ctx, self.s_ctx, self.b_lat, self.s_lat = b_ctx, s_ctx, b_lat, s_lat
        self.n_ctx = b_ctx * s_ctx
        self.n = self.n_ctx + b_lat * s_lat
        self.tpb_c = s_ctx // self.tm
        self.tpb_l = s_lat // self.tm
        self.nct = self.n_ctx // self.tm
        self.tiles = self.n // self.tm
        self.ctx_row = b_lat

    def mod_row(self, i):
        return jnp.where(i >= self.nct, (i - self.nct) // self.tpb_l, self.ctx_row)

    def tile_in_seq(self, i):
        return jnp.where(i >= self.nct, (i - self.nct) % self.tpb_l, i % self.tpb_c)

    def tiles_per_seq(self, i):
        return jnp.where(i >= self.nct, self.tpb_l, self.tpb_c)

    def rope_block(self, i):
        return jnp.where(i >= self.nct, self.tpb_c + (i - self.nct) % self.tpb_l, i % self.tpb_c)


def _mod_kernel(c_ref, w_ref, b_ref, o_ref):
    o_ref[0] = _dot(_silu(c_ref[...]), w_ref[0]) + b_ref[0]


def _modulation(cond, w_ada, b_ada):
    depth, d, cols = w_ada.shape
    cb = cols // 4
    return pl.pallas_call(
        _mod_kernel,
        out_shape=jax.ShapeDtypeStruct((depth, MOD_ROWS, cols), F32),
        grid=(depth, cols // cb),
        in_specs=[pl.BlockSpec((MOD_ROWS, d), lambda l, j: (0, 0)),
                  pl.BlockSpec((1, d, cb), lambda l, j: (l, 0, j)),
                  pl.BlockSpec((1, 1, cb), lambda l, j: (l, 0, j))],
        out_specs=pl.BlockSpec((1, MOD_ROWS, cb), lambda l, j: (l, 0, j)),
        compiler_params=pltpu.CompilerParams(vmem_limit_bytes=VMEM_LIMIT),
        name="modulation",
    )(cond, w_ada, b_ada.reshape(depth, 1, cols))


def _premix_kernel(xc_ref, xl_ref, mod_ref, w_ref, cc_ref, sc_ref, cos_ref, sin_ref, z_ref, *outs, nct):
    d = D_MODEL
    is_lat = pl.program_id(0) >= nct
    mod = mod_ref[0]
    h = _layer_norm(jnp.where(is_lat, xl_ref[...], xc_ref[...])) * (1.0 + mod[:, d:2 * d]) + mod[:, 0:d]
    p = _dot(h.astype(BF16), w_ref[...])
    c0 = CONV_DIM
    z_ref[...] = (p[:, 0:c0] * jax.nn.sigmoid(p[:, c0:2 * c0])).astype(z_ref.dtype)
    q0 = 2 * c0
    k0 = q0 + Q_DIM
    v0 = k0 + KV_DIM
    f0 = v0 + KV_DIM
    cos = cos_ref[...]
    sin = sin_ref[...]
    first = (lax.broadcasted_iota(I32, cos.shape, 1) % ROPE_AXIS_DIM) < ROPE_HALF

    def rot(t):
        sw = jnp.where(first, pltpu.roll(t, LANES - ROPE_HALF, 1), pltpu.roll(t, ROPE_HALF, 1))
        return t * cos + sw * sin

    q = p[:, q0:k0] * ATTN_SCALE
    q = jnp.concatenate([rot(q[:, j * LANES:(j + 1) * LANES]) for j in range(Q_DIM // LANES)], axis=1)
    uf = p[:, f0:f0 + FOURIER_DIM].astype(BF16)
    results = [q.astype(BF16), rot(p[:, k0:v0]), p[:, v0:f0],
               _dot(uf, cc_ref[...]).astype(BF16), _dot(uf, sc_ref[...]).astype(BF16)]

    @pl.when(jnp.logical_not(is_lat))
    def _():
        for r, o_ref in zip(results, outs[0::2]):
            o_ref[...] = r

    @pl.when(is_lat)
    def _():
        for r, o_ref in zip(results, outs[1::2]):
            o_ref[...] = r


def _premix(st, x_ctx, x_lat, mod_l, w_pre, cc, sc, rope_cos, rope_sin):
    d = x_ctx.shape[1]
    tm = st.tm
    tok = lambda w: pl.BlockSpec((tm, w), lambda i: (i, 0))
    ctx_tok = lambda w: pl.BlockSpec((tm, w), lambda i: (jnp.minimum(i, st.nct - 1), 0))
    lat_tok = lambda w: pl.BlockSpec((tm, w), lambda i: (jnp.maximum(i - st.nct, 0), 0))
    rope = pl.BlockSpec((tm, LANES), lambda i: (st.rope_block(i), 0))
    n_lat = st.n - st.n_ctx
    out_shape = [jax.ShapeDtypeStruct((st.n, CONV_DIM), BF16)]
    out_specs = [tok(CONV_DIM)]
    for w, dt in ((Q_DIM, BF16), (KV_DIM, F32), (KV_DIM, F32), (FOURIER_DIM, BF16), (FOURIER_DIM, BF16)):
        out_shape += [jax.ShapeDtypeStruct((st.n_ctx, w), dt), jax.ShapeDtypeStruct((n_lat, w), dt)]
        out_specs += [ctx_tok(w), lat_tok(w)]
    return pl.pallas_call(
        functools.partial(_premix_kernel, nct=st.nct),
        out_shape=out_shape,
        grid=(st.tiles,),
        in_specs=[ctx_tok(d), lat_tok(d), pl.BlockSpec((1, 1, 6 * d), lambda i: (st.mod_row(i), 0, 0)),
                  _full(w_pre.shape), _full(cc.shape), _full(sc.shape), rope, rope],
        out_specs=out_specs,
        compiler_params=pltpu.CompilerParams(dimension_semantics=("arbitrary",), vmem_limit_bytes=VMEM_LIMIT),
        name="premix",
    )(x_ctx, x_lat, mod_l, w_pre, cc, sc, rope_cos, rope_sin)


def _sink_softmax_pv(s, sink_col, v):
    m = jnp.maximum(jnp.max(s, axis=1, keepdims=True), sink_col)
    p = jnp.exp(s - m)
    denom = jnp.sum(p, axis=1, keepdims=True) + jnp.exp(sink_col - m)
    return _dot(p.astype(BF16), v) / denom


def _attend_heads(sink_ref, q_ref, keys, vals, bias, o_ref):
    rows = q_ref.shape[0]
    for first in range(0, N_HEADS, ATTN_STACK):
        h = first // Q_PER_KV
        heads = list(range(first, first + ATTN_STACK))
        q = jnp.concatenate([q_ref[:, hd * HEAD_DIM:(hd + 1) * HEAD_DIM] for hd in heads], axis=0)
        s = lax.dot_general(q, keys[:, h * HEAD_DIM:(h + 1) * HEAD_DIM], (((1,), (1,)), ((), ())),
                            preferred_element_type=F32)
        if bias is not None:
            s = s + bias
        sink_col = jnp.concatenate([jnp.full((rows, 1), sink_ref[hd], F32) for hd in heads], axis=0)
        o = _sink_softmax_pv(s, sink_col, vals[:, h * HEAD_DIM:(h + 1) * HEAD_DIM])
        for g, hd in enumerate(heads):
            o_ref[:, hd * HEAD_DIM:(hd + 1) * HEAD_DIM] = o[g * rows:(g + 1) * rows].astype(o_ref.dtype)


def _ctx_attn_kernel(sink_ref, q_ref, k_ref, v_ref, o_ref):
    _attend_heads(sink_ref, q_ref, k_ref[...].astype(BF16), v_ref[...].astype(BF16), None, o_ref)


def _ctx_attention(sink, q, k, v, batch, seq):
    return pl.pallas_call(
        _ctx_attn_kernel,
        out_shape=jax.ShapeDtypeStruct((batch * seq, Q_DIM), BF16),
        grid=(batch,),
        in_specs=[pl.BlockSpec(memory_space=pltpu.SMEM),
                  pl.BlockSpec((seq, Q_DIM), lambda i: (i, 0)),
                  pl.BlockSpec((seq, KV_DIM), lambda i: (i, 0)),
                  pl.BlockSpec((seq, KV_DIM), lambda i: (i, 0))],
        out_specs=pl.BlockSpec((seq, Q_DIM), lambda i: (i, 0)),
        compiler_params=pltpu.CompilerParams(vmem_limit_bytes=VMEM_LIMIT),
        name="attn_ctx",
    )(sink, q, k, v)


def _lat_attn_kernel(sink_ref, q_ref, k_ref, v_ref, kc_ref, vc_ref, bias_ref, o_ref):
    i = pl.program_id(1)
    nblk = pl.num_programs(1)
    blk = ATTN_BLOCK
    starts = [pl.multiple_of(jnp.maximum(i - 1, 0) * blk, blk),
              pl.multiple_of(i * blk, blk),
              pl.multiple_of(jnp.minimum(i + 1, nblk - 1) * blk, blk)]
    keys = jnp.concatenate([k_ref[pl.ds(st, blk), :] for st in starts] + [kc_ref[...]], axis=0).astype(BF16)
    vals = jnp.concatenate([v_ref[pl.ds(st, blk), :] for st in starts] + [vc_ref[...]], axis=0).astype(BF16)
    _attend_heads(sink_ref, q_ref, keys, vals, bias_ref[0], o_ref)


def _window_bias(n_ctx_keys):
    blk = ATTN_BLOCK
    r = jnp.arange(blk)[:, None]
    j = jnp.arange(3 * blk)[None, :]
    band = (j >= r) & (j <= r + 2 * WINDOW)
    has_prev = j >= blk
    has_next = j < 2 * blk
    variants = [band & has_prev, band, band & has_next]
    out = []
    for m in variants:
        full = jnp.concatenate([m, jnp.ones((blk, n_ctx_keys), bool)], axis=1)
        out.append(jnp.tile(jnp.where(full, 0.0, NEG_INF).astype(F32), (ATTN_STACK, 1)))
    return jnp.stack(out)


def _lat_attention(sink, q, k, v, kc, vc, batch, seq):
    blk = ATTN_BLOCK
    nblk = seq // blk
    assert nblk >= 2 and WINDOW == blk
    pc = kc.shape[0] // batch
    bias = _window_bias(pc)
    nkeys = 3 * blk + pc
    bias_map = lambda i, j: (jnp.where(j == 0, 0, jnp.where(j == nblk - 1, 2, 1)), 0, 0)
    return pl.pallas_call(
        _lat_attn_kernel,
        out_shape=jax.ShapeDtypeStruct((batch * seq, Q_DIM), BF16),
        grid=(batch, nblk),
        in_specs=[pl.BlockSpec(memory_space=pltpu.SMEM),
                  pl.BlockSpec((blk, Q_DIM), lambda i, j: (i * nblk + j, 0)),
                  pl.BlockSpec((seq, KV_DIM), lambda i, j: (i, 0)),
                  pl.BlockSpec((seq, KV_DIM), lambda i, j: (i, 0)),
                  pl.BlockSpec((pc, KV_DIM), lambda i, j: (i, 0)),
                  pl.BlockSpec((pc, KV_DIM), lambda i, j: (i, 0)),
                  pl.BlockSpec((1, ATTN_STACK * blk, nkeys), bias_map)],
        out_specs=pl.BlockSpec((blk, Q_DIM), lambda i, j: (i * nblk + j, 0)),
        compiler_params=pltpu.CompilerParams(vmem_limit_bytes=VMEM_LIMIT),
        name="attn_lat",
    )(sink, q, k, v, kc, vc, bias)


def _fourier_kernel(cn_ref, sn_ref, uc_ref, us_ref, o_ref):
    o_ref[...] = (_dot(cn_ref[...], uc_ref[...]) - _dot(sn_ref[...], us_ref[...])).astype(o_ref.dtype)


def _fourier_positions(cn, sn, uc, us, nb):
    seq = cn.shape[0]
    tf = min(512, seq)
    jt = seq // tf
    return pl.pallas_call(
        _fourier_kernel,
        out_shape=jax.ShapeDtypeStruct((nb * seq, FOURIER_DIM), BF16),
        grid=(jt, nb),
        in_specs=[pl.BlockSpec((tf, seq), lambda j, b: (j, 0)),
                  pl.BlockSpec((tf, seq), lambda j, b: (j, 0)),
                  pl.BlockSpec((seq, FOURIER_DIM), lambda j, b: (b, 0)),
                  pl.BlockSpec((seq, FOURIER_DIM), lambda j, b: (b, 0))],
        out_specs=pl.BlockSpec((tf, FOURIER_DIM), lambda j, b: (b * jt + j, 0)),
        compiler_params=pltpu.CompilerParams(vmem_limit_bytes=VMEM_LIMIT),
        name="fourier",
    )(cn, sn, uc, us)


def _postmix_kernel(xc_ref, xl_ref, mod_ref, z_ref, zp_ref, zn_ref, attn_c_ref, attn_l_ref, four_c_ref, four_l_ref,
                    wg_ref, cw_ref, cb_ref, cg_ref,
                    cbeta_ref, wco_ref, wao_ref, wfo_ref, wo_ref, g1_ref, b1_ref, rwh_ref, rwl_ref, rb_ref,
                    sg_ref, su_ref, sd_ref,
                    r0_ref, xs_ref, pos_ref, wts_ref, tab_ref, cnt_ref, zs_ref, carry_ref, *, st):
    d = D_MODEL
    i = pl.program_id(0)
    tm = xc_ref.shape[0]
    is_lat = i >= st.nct
    attn = jnp.where(is_lat, attn_l_ref[...], attn_c_ref[...])
    four = jnp.where(is_lat, four_l_ref[...], four_c_ref[...])
    x = jnp.where(is_lat, xl_ref[...], xc_ref[...])
    mod = mod_ref[0]
    shift1, scale1, gate1, shift2, scale2, gate2 = [mod[:, j * d:(j + 1) * d] for j in range(6)]
    h = (_layer_norm(x) * (1.0 + scale1) + shift1).astype(BF16)
    gates = jax.nn.sigmoid(_dot(h, wg_ref[...]))

    tib = st.tile_in_seq(i)
    zs_ref[0:CONV_HALO, :] = zp_ref[...].astype(F32) * jnp.where(tib != 0, 1.0, 0.0)
    zs_ref[CONV_HALO + tm:, :] = zn_ref[...].astype(F32) * jnp.where(tib != st.tiles_per_seq(i) - 1, 1.0, 0.0)
    zs_ref[CONV_HALO:CONV_HALO + tm, :] = z_ref[...].astype(F32)
    acc = jnp.broadcast_to(cb_ref[...], (tm, CONV_DIM))
    first_off = CONV_HALO - CONV_PAD
    window = zs_ref[...]
    rows = window.shape[0]
    for r in range(SUBLANES):
        offs = [o for o in range(first_off, first_off + CONV_WIDTH) if o % SUBLANES == r]
        shifted = window if r == 0 else pltpu.roll(window, rows - r, 0)
        for o in offs:
            acc = acc + cw_ref[o - first_off:o - first_off + 1, :] * shifted[o - r:o - r + tm]
    conv = _silu(_layer_norm(acc) * cg_ref[...] + cbeta_ref[...]).astype(BF16)

    merged = (gates[:, 0:d] * _dot(conv, wco_ref[...])
              + gates[:, d:2 * d] * _dot(attn, wao_ref[...])
              + gates[:, 2 * d:3 * d] * _dot(four, wfo_ref[...]))
    mix = _dot(merged.astype(BF16), wo_ref[...])
    x1 = _layer_norm(DEEPNORM_ALPHA * x + gate1 * mix) * g1_ref[...] + b1_ref[...]
    h2 = _layer_norm(x1) * (1.0 + scale2) + shift2

    h2_hi = h2.astype(BF16)
    h2_lo = (h2 - h2_hi.astype(F32)).astype(BF16)
    nt = (((1,), (1,)), ((), ()))
    logits_t = (lax.dot_general(rwh_ref[...], h2_hi, nt, preferred_element_type=F32)
                + lax.dot_general(rwh_ref[...], h2_lo, nt, preferred_element_type=F32)
                + lax.dot_general(rwl_ref[...], h2_hi, nt, preferred_element_type=F32))
    scores_t = jax.nn.sigmoid(logits_t)
    sel_t = scores_t + rb_ref[...]
    expert_t = lax.broadcasted_iota(I32, (LANES, tm), 0)
    onehots_t, top_s = [], []
    for _ in range(TOP_K):
        mx = jnp.max(sel_t, axis=0, keepdims=True)
        idx = jnp.min(jnp.where(sel_t == mx, expert_t, LANES), axis=0, keepdims=True)
        oh = expert_t == idx
        onehots_t.append(oh)
        top_s.append(jnp.sum(jnp.where(oh, scores_t, 0.0), axis=0, keepdims=True))
        sel_t = jnp.where(oh, -jnp.inf, sel_t)
    total = top_s[0]
    for s_k in top_s[1:]:
        total = total + s_k

    @pl.when(i == 0)
    def _():
        carry_ref[...] = jnp.zeros_like(carry_ref)

    chosen_t = jnp.zeros((LANES, tm), F32)
    for oh in onehots_t:
        chosen_t = jnp.where(oh, 1.0, chosen_t)
    chosen_tb = chosen_t.astype(BF16)
    tok_r = lax.broadcasted_iota(I32, (tm, tm), 0)
    tok_c = lax.broadcasted_iota(I32, (tm, tm), 1)
    prefix_t = _dot(chosen_tb, jnp.where(tok_r < tok_c, 1.0, 0.0).astype(BF16))
    cpad_t = jnp.floor((jnp.sum(chosen_t, axis=1, keepdims=True) + (PIECE - 1)) * (1.0 / PIECE)) * PIECE
    exp_r = lax.broadcasted_iota(I32, (LANES, LANES), 0)
    exp_c = lax.broadcasted_iota(I32, (LANES, LANES), 1)
    cstart_t = _dot(jnp.where(exp_c < exp_r, 1.0, 0.0).astype(BF16),
                    jnp.broadcast_to(cpad_t, (LANES, tm)).astype(BF16))

    where_t = prefix_t + cstart_t
    krow = lax.broadcasted_iota(I32, (LANES, tm), 0)
    pos_t = jnp.zeros((LANES, tm), F32)
    wts_t = jnp.zeros((LANES, tm), F32)
    for k, oh in enumerate(onehots_t):
        pos_t = jnp.where(krow == k, jnp.sum(jnp.where(oh, where_t, 0.0), axis=0, keepdims=True), pos_t)
        wts_t = jnp.where(krow == k, ROUTED_SCALE * top_s[k] / total, wts_t)
    pos_ref[...] = pos_t.T.astype(I32)
    wts_ref[...] = wts_t.T

    chosen = chosen_t.T
    count = jnp.sum(chosen, axis=0, keepdims=True)
    cpad = jnp.floor((count + (PIECE - 1)) * (1.0 / PIECE)) * PIECE
    er = lax.broadcasted_iota(I32, (LANES, LANES), 0)
    ec = lax.broadcasted_iota(I32, (LANES, LANES), 1)
    cstart = _dot(jnp.broadcast_to(cpad, (SUBLANES, LANES)), jnp.where(er < ec, 1.0, 0.0))[0:1]
    gbase = carry_ref[...]
    carry_ref[...] = gbase + cpad
    cnt_ref[...] = gbase + cpad
    pieces = jnp.sum(cpad, axis=1, keepdims=True) * (1.0 / PIECE)
    trow = lax.broadcasted_iota(I32, (SUBLANES, LANES), 0)
    per_expert = jnp.where(trow == 0, cstart, jnp.where(trow == 1, cpad, jnp.where(trow == 2, gbase, pieces)))
    piece_row = (lax.broadcasted_iota(I32, (SORT_PIECES, LANES), 0) * PIECE).astype(F32)
    owns = jnp.where((piece_row >= cstart) & (piece_row < cstart + cpad), 1.0, 0.0).astype(BF16)
    expert_id = lax.broadcasted_iota(I32, (SUBLANES, LANES), 1).astype(BF16)
    piece_expert = lax.dot_general(expert_id, owns, (((1,), (1,)), ((), ())), preferred_element_type=F32)
    tab_ref[...] = jnp.concatenate([per_expert, piece_expert], axis=1).astype(I32)

    slot_e = lax.broadcasted_iota(I32, (SORT_ROWS, LANES), 0).astype(F32)
    slot_expert = jnp.where((slot_e >= cstart) & (slot_e < cstart + cpad), 1.0, 0.0).astype(BF16)
    rank_t = jnp.where(chosen_t > 0.0, prefix_t, -1.0)
    looked = _dot(slot_expert, jnp.concatenate([rank_t.astype(BF16), cstart_t.astype(BF16)], axis=1))
    slot_t = lax.broadcasted_iota(I32, (SORT_ROWS, tm), 0).astype(F32)
    perm = jnp.where(looked[:, :tm] + looked[:, tm:] == slot_t, 1.0, 0.0).astype(BF16)
    h2b = h2.astype(BF16)
    xs_ref[...] = _pack_rows(_dot(perm, h2b), rounded=True)

    shared = _dot((_silu(_dot(h2b, sg_ref[...])) * _dot(h2b, su_ref[...])).astype(BF16), sd_ref[...])
    r0_ref[...] = DEEPNORM_ALPHA * x1 + gate2 * shared


def _postmix(st, x_ctx, x_lat, mod_l, z, attn_c, attn_l, four_c, four_l, lw):
    n, d = st.n, x_ctx.shape[1]
    tm = st.tm
    hb = tm // CONV_HALO
    last_halo = n // CONV_HALO - 1
    tok = lambda w: pl.BlockSpec((tm, w), lambda i: (i, 0))
    ctx_tok = lambda w: pl.BlockSpec((tm, w), lambda i: (jnp.minimum(i, st.nct - 1), 0))
    lat_tok = lambda w: pl.BlockSpec((tm, w), lambda i: (jnp.maximum(i - st.nct, 0), 0))
    in_specs = [ctx_tok(d), lat_tok(d), pl.BlockSpec((1, 1, 6 * d), lambda i: (st.mod_row(i), 0, 0)), tok(CONV_DIM),
                pl.BlockSpec((CONV_HALO, CONV_DIM), lambda i: (jnp.maximum(i * hb - 1, 0), 0)),
                pl.BlockSpec((CONV_HALO, CONV_DIM), lambda i: (jnp.minimum((i + 1) * hb, last_halo), 0)),
                ctx_tok(Q_DIM), lat_tok(Q_DIM), ctx_tok(FOURIER_DIM), lat_tok(FOURIER_DIM)]
    args = [x_ctx, x_lat, mod_l, z, z, z, attn_c, attn_l, four_c, four_l]
    weights = [lw['w_gates'], lw['conv_w'], lw['conv_b'], lw['conv_ln_g'], lw['conv_ln_b'], lw['w_conv_out'],
               lw['w_attn_out'], lw['w_fourier_out'], lw['w_out'], lw['ln1_g'], lw['ln1_b'],
               lw['router_hi'], lw['router_lo'], lw['router_bias'],
               lw['shared_w_gate'], lw['shared_w_up'], lw['shared_w_down']]
    in_specs += [_full(w.shape) for w in weights]
    args += weights
    return pl.pallas_call(
        functools.partial(_postmix_kernel, st=st),
        out_shape=[jax.ShapeDtypeStruct((n, d), F32),
                   jax.ShapeDtypeStruct((st.tiles * SORT_ROWS, PACKED), U32),
                   jax.ShapeDtypeStruct((n, LANES), I32),
                   jax.ShapeDtypeStruct((n, LANES), F32),
                   jax.ShapeDtypeStruct((st.tiles * SUBLANES, TAB_COLS), I32),
                   jax.ShapeDtypeStruct((1, LANES), F32)],
        grid=(st.tiles,),
        in_specs=in_specs,
        out_specs=[tok(d), pl.BlockSpec((SORT_ROWS, PACKED), lambda i: (i, 0)), tok(LANES), tok(LANES),
                   pl.BlockSpec((SUBLANES, TAB_COLS), lambda i: (i, 0)), _full((1, LANES))],
        scratch_shapes=[pltpu.VMEM((tm + 2 * CONV_HALO, CONV_DIM), F32),
                        pltpu.VMEM((1, LANES), F32)],
        compiler_params=pltpu.CompilerParams(dimension_semantics=("arbitrary",), vmem_limit_bytes=VMEM_LIMIT),
        name="postmix",
    )(*args)


RING = 3


def _tab(tab_ref, r, c):
    return tab_ref[r * TAB_COLS + c]


def _for_each_piece(pstart_ref, tab_ref, fn):
    def one(p, prio):
        e = _tab(tab_ref, 0, LANES + p)
        local_row = p * PIECE
        sorted_row = pstart_ref[e] + _tab(tab_ref, 2, e) + (local_row - _tab(tab_ref, 0, e))
        fn(pl.multiple_of(local_row, PIECE), pl.multiple_of(sorted_row, PIECE), prio)

    n = _tab(tab_ref, 3, 0)
    groups = n // PIECE_UNROLL

    def group(g, c):
        for u in range(PIECE_UNROLL):
            one(g * PIECE_UNROLL + u, u % 2)
        return c

    lax.fori_loop(0, groups, group, 0)
    lax.fori_loop(groups * PIECE_UNROLL, n, lambda p, c: (one(p, 0), c)[1], 0)


def _wait_pieces(n_pieces, src_ref, dst_ref, sem):
    @pl.when(n_pieces > 0)
    def _():
        rows = n_pieces * PIECE
        pltpu.make_async_copy(src_ref.at[pl.ds(0, rows)], dst_ref.at[pl.ds(0, rows)], sem).wait()


def _experts_kernel(be_ref, nu_ref, src_ref, xs_ref, wg_ref, wu_ref, wd_ref, ys_ref, xbuf_ref, sem, wgb_ref, wub_ref,
                    wdb_ref):
    j = pl.program_id(0)
    n_used = nu_ref[0]
    used = j < n_used

    def gather(block):
        slot = block % RING
        first = (block * BLOCK_PIECES) % SRC_BLOCK

        def group(g, c):
            for u in range(PIECE_UNROLL):
                p = g * PIECE_UNROLL + u
                src_row = pl.multiple_of(src_ref[first + p], PIECE)
                pltpu.make_async_copy(xs_ref.at[pl.ds(src_row, PIECE)],
                                      xbuf_ref.at[slot, pl.ds(pl.multiple_of(p * PIECE, PIECE), PIECE)],
                                      sem.at[slot]).start(priority=u % 2)
            return c

        lax.fori_loop(0, BLOCK_PIECES // PIECE_UNROLL, group, 0)

    @pl.when(j == 0)
    def _():
        gather(0)

        @pl.when(n_used > 1)
        def _():
            gather(1)

    @pl.when(j + 2 < n_used)
    def _():
        gather(j + 2)

    @pl.when(jnp.logical_or(j == 0, be_ref[j] != be_ref[jnp.maximum(j - 1, 0)]))
    def _():
        wgb_ref[...] = wg_ref[0, 0].astype(BF16)
        wub_ref[...] = wu_ref[0, 0].astype(BF16)
        wdb_ref[...] = wd_ref[0, 0].astype(BF16)

    @pl.when(used)
    def _():
        slot = j % RING
        pltpu.make_async_copy(xs_ref.at[pl.ds(0, EXPERT_ROWS)], xbuf_ref.at[slot], sem.at[slot]).wait()
        x = _unpack_rows(xbuf_ref[slot])
        a = (_silu(_dot(x, wgb_ref[...])) * _dot(x, wub_ref[...])).astype(BF16)
        ys_ref[...] = _pack_rows(_dot(a, wdb_ref[...]))

    @pl.when(jnp.logical_not(used))
    def _():
        ys_ref[...] = jnp.zeros_like(ys_ref)


def _experts(block_e, n_used, src_rows, xs_local, n_blocks, w_gate, w_up, w_down, layer):
    d = w_gate.shape[2]
    r = EXPERT_ROWS
    n_rows = n_blocks * r
    last_table = src_rows.shape[0] // SRC_BLOCK - 1
    ahead = lambda j, be, nu: (jnp.minimum((j + 2) * BLOCK_PIECES // SRC_BLOCK, last_table),)
    return pl.pallas_call(
        _experts_kernel,
        out_shape=jax.ShapeDtypeStruct((n_rows, PACKED), U32),
        grid_spec=pltpu.PrefetchScalarGridSpec(
            num_scalar_prefetch=2,
            grid=(n_blocks,),
            in_specs=[pl.BlockSpec((SRC_BLOCK,), ahead, memory_space=pltpu.SMEM),
                      pl.BlockSpec(memory_space=pl.ANY),
                      pl.BlockSpec((1, 1, d, EXPERT_DIM), lambda j, be, nu: (layer, be[j], 0, 0)),
                      pl.BlockSpec((1, 1, d, EXPERT_DIM), lambda j, be, nu: (layer, be[j], 0, 0)),
                      pl.BlockSpec((1, 1, EXPERT_DIM, d), lambda j, be, nu: (layer, be[j], 0, 0))],
            out_specs=pl.BlockSpec((r, PACKED), lambda j, be, nu: (j, 0)),
            scratch_shapes=[pltpu.VMEM((RING, r, PACKED), U32), pltpu.SemaphoreType.DMA((RING,)),
                            pltpu.VMEM((d, EXPERT_DIM), BF16), pltpu.VMEM((d, EXPERT_DIM), BF16),
                            pltpu.VMEM((EXPERT_DIM, d), BF16)]),
        compiler_params=pltpu.CompilerParams(dimension_semantics=("arbitrary",), vmem_limit_bytes=VMEM_LIMIT),
        name="moe_experts",
    )(block_e, n_used, src_rows, xs_local, w_gate, w_up, w_down)


def _combine_kernel(pstart_ref, tab_ref, tab_next_ref, r0_ref, pos_ref, wts_ref, mod_ref, g2_ref, b2_ref, ys_ref,
                    oc_ref, ol_ref, buf_ref, sem, *, nct):
    tm, d = r0_ref.shape
    i = pl.program_id(0)
    slot = i % 2

    def piece_copy(slot_idx, local_row, sorted_row):
        return pltpu.make_async_copy(ys_ref.at[pl.ds(sorted_row, PIECE)],
                                     buf_ref.at[slot_idx, pl.ds(local_row, PIECE)], sem.at[slot_idx])

    @pl.when(i == 0)
    def _():
        buf_ref[...] = jnp.zeros_like(buf_ref)
        _for_each_piece(pstart_ref, tab_ref, lambda a, g, prio: piece_copy(0, a, g).start(priority=prio))

    @pl.when(i + 1 < pl.num_programs(0))
    def _():
        _for_each_piece(pstart_ref, tab_next_ref,
                        lambda a, g, prio: piece_copy(1 - slot, a, g).start(priority=prio))

    _wait_pieces(_tab(tab_ref, 3, 0), ys_ref, buf_ref.at[slot], sem.at[slot])

    pos = pos_ref[...]
    wts = wts_ref[...]
    col = lax.broadcasted_iota(I32, (tm, SORT_ROWS), 1)
    weight = jnp.zeros((tm, SORT_ROWS), F32)
    for k in range(TOP_K):
        weight = jnp.where(col == pos[:, k:k + 1], wts[:, k:k + 1], weight)
    routed = _dot(weight.astype(BF16), _unpack_rows(buf_ref[slot]))
    gate2 = mod_ref[0][:, 5 * d:6 * d]
    out = _layer_norm(r0_ref[...] + gate2 * routed) * g2_ref[...] + b2_ref[...]
    @pl.when(i < nct)
    def _():
        oc_ref[...] = out

    @pl.when(i >= nct)
    def _():
        ol_ref[...] = out


def _combine(st, pstart, tab, r0, pos, wts, mod_l, ln2_g, ln2_b, ys_sorted):
    n, d = r0.shape
    tm = st.tm
    tok = lambda w: pl.BlockSpec((tm, w), lambda i, *_: (i, 0))
    last = st.tiles - 1
    out_shape = [jax.ShapeDtypeStruct((st.n_ctx, d), F32), jax.ShapeDtypeStruct((n - st.n_ctx, d), F32)]
    out_specs = [pl.BlockSpec((tm, d), lambda i, *_: (jnp.minimum(i, st.nct - 1), 0)),
                 pl.BlockSpec((tm, d), lambda i, *_: (jnp.maximum(i - st.nct, 0), 0))]
    return pl.pallas_call(
        functools.partial(_combine_kernel, nct=st.nct),
        out_shape=out_shape,
        grid_spec=pltpu.PrefetchScalarGridSpec(
            num_scalar_prefetch=1,
            grid=(st.tiles,),
            in_specs=[pl.BlockSpec((TAB_SIZE,), lambda i, *_: (i,), memory_space=pltpu.SMEM),
                      pl.BlockSpec((TAB_SIZE,), lambda i, *_: (jnp.minimum(i + 1, last),), memory_space=pltpu.SMEM),
                      tok(d), tok(LANES), tok(LANES),
                      pl.BlockSpec((1, 1, 6 * d), lambda i, *_: (st.mod_row(i), 0, 0)),
                      _full((1, d)), _full((1, d)),
                      pl.BlockSpec(memory_space=pl.ANY)],
            out_specs=out_specs,
            scratch_shapes=[pltpu.VMEM((2, SORT_ROWS, PACKED), U32), pltpu.SemaphoreType.DMA((2,))]),
        compiler_params=pltpu.CompilerParams(dimension_semantics=("arbitrary",), vmem_limit_bytes=VMEM_LIMIT),
        name="moe_combine",
    )(pstart, tab, tab, r0, pos, wts, mod_l, ln2_g, ln2_b, ys_sorted)


def _moe_routed(st, xs_local, tab, counts, lw, layer):
    r = EXPERT_ROWS
    n_blocks = -(-(st.tiles * (st.tm * TOP_K + N_EXPERTS * (PIECE - 1))) // r) + N_EXPERTS
    cnt = counts[0, :N_EXPERTS].astype(I32)
    padded = (cnt + r - 1) // r * r
    pend = jnp.cumsum(padded)
    pstart = (pend - padded).astype(I32)
    n_used = (pend[-1] // r).astype(I32)
    blocks = jnp.arange(n_blocks, dtype=I32)
    block_i = jnp.minimum(blocks, n_used - 1)
    block_e = jnp.sum((pend[None, :] <= (block_i * r)[:, None]).astype(I32), axis=1)
    block_e = jnp.minimum(block_e, N_EXPERTS - 1).astype(I32)
    t3 = tab.reshape(st.tiles, SUBLANES, TAB_COLS)
    cstart_et = t3[:, 0, :N_EXPERTS].T.astype(F32)
    gbase_et = t3[:, 2, :N_EXPERTS].T.astype(F32)
    gend_et = gbase_et + t3[:, 1, :N_EXPERTS].T.astype(F32)
    n_entries = -(-(n_blocks * BLOCK_PIECES) // SRC_BLOCK) * SRC_BLOCK
    entry_is = (block_e[:, None] == jnp.arange(N_EXPERTS, dtype=I32)[None, :]).astype(F32)
    entry_is = jnp.broadcast_to(entry_is[:, None, :], (n_blocks, BLOCK_PIECES, N_EXPERTS)).reshape(-1, N_EXPERTS)
    entry_is = jnp.pad(entry_is, ((0, n_entries - n_blocks * BLOCK_PIECES), (0, 0)))
    exact = lax.Precision.HIGHEST
    row_in_expert = (jnp.arange(n_entries, dtype=I32) * PIECE).astype(F32) - jnp.dot(
        entry_is, pstart.astype(F32), precision=exact)
    ends = jnp.dot(entry_is, gend_et, precision=exact)
    tile = jnp.sum((ends <= row_in_expert[:, None]).astype(I32), axis=1)
    in_use = row_in_expert < ends[:, -1]
    shift = jnp.dot(entry_is, cstart_et - gbase_et, precision=exact)
    shift = jnp.sum(jnp.where(jnp.arange(st.tiles, dtype=I32)[None, :] == tile[:, None], shift, 0.0), axis=1)
    src_rows = jnp.where(in_use, tile * SORT_ROWS + (shift + row_in_expert).astype(I32), SORT_ROWS - PIECE)
    ys = _experts(block_e, n_used.reshape(1), src_rows, xs_local, n_blocks,
                  lw['exp_w_gate'], lw['exp_w_up'], lw['exp_w_down'], layer)
    return pstart, ys


def _channel_dft():
    j = jnp.arange(FOURIER_DIM)
    same = (j[:, None] // FOURIER_GROUP_DIM) == (j[None, :] // FOURIER_GROUP_DIM)
    prod = ((j[:, None] % FOURIER_GROUP_DIM) * (j[None, :] % FOURIER_GROUP_DIM)) % FOURIER_GROUP_DIM
    ang = (2.0 * jnp.pi / FOURIER_GROUP_DIM) * prod.astype(F32)
    scale = FOURIER_GROUP_DIM ** -0.5
    return (jnp.where(same, jnp.cos(ang) * scale, 0.0).astype(BF16),
            jnp.where(same, jnp.sin(ang) * scale, 0.0).astype(BF16))


def _position_dft(n):
    m = 1 << (n.bit_length() // 2)
    assert n % m == 0
    k = jnp.arange(n, dtype=I32)[None, :]
    ang_hi = (2.0 * jnp.pi / n) * ((jnp.arange(n // m, dtype=I32)[:, None] * m * k) % n).astype(F32)
    ang_lo = (2.0 * jnp.pi / n) * ((jnp.arange(m, dtype=I32)[:, None] * k) % n).astype(F32)
    scale = n ** -0.5
    ch, sh = (jnp.cos(ang_hi) * scale)[:, None, :], (jnp.sin(ang_hi) * scale)[:, None, :]
    cl, sl = jnp.cos(ang_lo)[None, :, :], jnp.sin(ang_lo)[None, :, :]
    return ((ch * cl - sh * sl).reshape(n, n).astype(BF16), (sh * cl + ch * sl).reshape(n, n).astype(BF16))


def _rope_tables(s_ctx, n_lat):
    t = jnp.arange(n_lat)
    inv = ROPE_BASE ** (-jnp.arange(ROPE_HALF, dtype=F32) / ROPE_HALF)
    row_ang = (t // GRID_W).astype(F32)[:, None] * inv[None, :]
    col_ang = (t % GRID_W).astype(F32)[:, None] * inv[None, :]
    cos = jnp.concatenate([jnp.cos(row_ang)] * 2 + [jnp.cos(col_ang)] * 2, axis=1)
    sin = jnp.concatenate([-jnp.sin(row_ang), jnp.sin(row_ang), -jnp.sin(col_ang), jnp.sin(col_ang)], axis=1)
    reps = LANES // HEAD_DIM
    cos = jnp.concatenate([jnp.ones((s_ctx, LANES), F32), jnp.tile(cos, (1, reps))], axis=0)
    sin = jnp.concatenate([jnp.zeros((s_ctx, LANES), F32), jnp.tile(sin, (1, reps))], axis=0)
    return cos, sin


def kernel(x_prompt, x_sample, cache_k, cache_v, c, c_ctx, w_ada, b_ada, w_in, conv_w, conv_b, conv_ln_g, conv_ln_b, w_conv_out, attn_sink, w_attn_out, w_fourier_out, w_out, post_ln_g, post_ln_b, router_w, router_bias, exp_w_gate, exp_w_up, exp_w_down, shared_w_gate, shared_w_up, shared_w_down):
    b_ctx, s_ctx, d = x_prompt.shape
    b_lat, s_lat, _ = x_sample.shape
    depth = w_in.shape[0]
    assert b_lat < MOD_ROWS and d == D_MODEL
    st = _Stream(b_ctx, s_ctx, b_lat, s_lat)
    nc = st.n_ctx

    cond = jnp.concatenate([c, c_ctx[None, :], jnp.zeros((MOD_ROWS - b_lat - 1, d), F32)], axis=0)
    mod = _modulation(cond, w_ada, b_ada)

    cc, sc = _channel_dft()
    cn_ctx, sn_ctx = _position_dft(s_ctx)
    cn_lat, sn_lat = _position_dft(s_lat)
    rope_cos, rope_sin = _rope_tables(s_ctx, s_lat)

    y_ctx = x_prompt.reshape(nc, d)
    y_lat = x_sample.reshape(b_lat * s_lat, d)
    new_k, new_v = [], []
    for l in range(depth):
        rw = jnp.pad(router_w[l].T, ((0, LANES - N_EXPERTS), (0, 0)))
        rw_hi = rw.astype(BF16)
        row = lambda a: a.reshape(1, -1)
        lw = dict(
            w_gates=w_in[l][:, PRE_COLS:].astype(BF16),
            conv_w=conv_w[l], conv_b=row(conv_b[l]), conv_ln_g=row(conv_ln_g[l]), conv_ln_b=row(conv_ln_b[l]),
            w_conv_out=w_conv_out[l].astype(BF16), w_attn_out=w_attn_out[l].astype(BF16),
            w_fourier_out=w_fourier_out[l].astype(BF16), w_out=w_out[l].astype(BF16),
            ln1_g=row(post_ln_g[l, 0]), ln1_b=row(post_ln_b[l, 0]),
            router_hi=rw_hi, router_lo=(rw - rw_hi.astype(F32)).astype(BF16),
            router_bias=jnp.broadcast_to(
                jnp.pad(router_bias[l], (0, LANES - N_EXPERTS), constant_values=-jnp.inf)[:, None],
                (LANES, TOKEN_TILE)),
            exp_w_gate=exp_w_gate, exp_w_up=exp_w_up, exp_w_down=exp_w_down,
            shared_w_gate=shared_w_gate[l].astype(BF16), shared_w_up=shared_w_up[l].astype(BF16),
            shared_w_down=shared_w_down[l].astype(BF16))
        mod_l = mod[l].reshape(MOD_ROWS, 1, 6 * d)
        sink = attn_sink[l]

        z, q_c, q_l, k_c, k_l, v_c, v_l, uc_c, uc_l, us_c, us_l = _premix(
            st, y_ctx, y_lat, mod_l, w_in[l][:, :PRE_COLS].astype(BF16), cc, sc, rope_cos, rope_sin)
        new_k.append(k_c.reshape(b_ctx, s_ctx, N_KV_HEADS, HEAD_DIM))
        new_v.append(v_c.reshape(b_ctx, s_ctx, N_KV_HEADS, HEAD_DIM))
        attn_c = _ctx_attention(sink, q_c, k_c, v_c, b_ctx, s_ctx)
        attn_l = _lat_attention(sink, q_l, k_l, v_l, cache_k[:, l].reshape(-1, KV_DIM),
                                cache_v[:, l].reshape(-1, KV_DIM), b_lat, s_lat)
        four_c = _fourier_positions(cn_ctx, sn_ctx, uc_c, us_c, b_ctx)
        four_l = _fourier_positions(cn_lat, sn_lat, uc_l, us_l, b_lat)
        r0, xs_local, pos, wts, tab, counts = _postmix(st, y_ctx, y_lat, mod_l, z, attn_c, attn_l, four_c, four_l, lw)
        tab = tab.reshape(-1)
        pstart, ys_sorted = _moe_routed(st, xs_local, tab, counts, lw, l)
        y_ctx, y_lat = _combine(st, pstart, tab, r0, pos, wts, mod_l, row(post_ln_g[l, 1]), row(post_ln_b[l, 1]),
                                ys_sorted)

    return (y_ctx.reshape(b_ctx, s_ctx, d), y_lat.reshape(b_lat, s_lat, d),
            jnp.stack(new_k, axis=1), jnp.stack(new_v, axis=1))
```

```python
import functools

import jax
import jax.numpy as jnp
from jax import lax
from jax.experimental import pallas as pl
from jax.experimental.pallas import tpu as pltpu

F32 = jnp.float32
BF16 = jnp.bfloat16
I32 = jnp.int32
U32 = jnp.uint32

D_MODEL = 1024
DEPTH = 2
GRID_W = 64
CONV_DIM = 256
CONV_WIDTH = 31
CONV_PAD = CONV_WIDTH // 2
CONV_HALO = 16
N_HEADS = 8
N_KV_HEADS = 2
HEAD_DIM = 64
Q_PER_KV = N_HEADS // N_KV_HEADS
Q_DIM = N_HEADS * HEAD_DIM
KV_DIM = N_KV_HEADS * HEAD_DIM
WINDOW = 128
ATTN_BLOCK = 128
ATTN_SCALE = HEAD_DIM ** -0.5
ROPE_AXIS_DIM = HEAD_DIM // 2
ROPE_HALF = ROPE_AXIS_DIM // 2
ROPE_BASE = 10000.0
NEG_INF = -1e30
FOURIER_GROUPS = 4
FOURIER_GROUP_DIM = 64
FOURIER_DIM = FOURIER_GROUPS * FOURIER_GROUP_DIM
PRE_COLS = 2 * CONV_DIM + Q_DIM + 2 * KV_DIM + FOURIER_DIM
N_EXPERTS = 64
TOP_K = 6
EXPERT_DIM = 256
ROUTED_SCALE = 2.5
LN_EPS = 1e-6
DEEPNORM_ALPHA = (2 * DEPTH) ** 0.25

LANES = 128
SUBLANES = 8
MOD_ROWS = 16
TOKEN_TILE = 256
EXPERT_ROWS = 512
PIECE = SUBLANES
SORT_ROWS = -(-(TOKEN_TILE * TOP_K + N_EXPERTS * (PIECE - 1)) // 256) * 256
SORT_PIECES = SORT_ROWS // PIECE
TAB_COLS = LANES + SORT_PIECES
ATTN_STACK = Q_PER_KV
ATTN_SUB = 4
TAB_SIZE = SUBLANES * TAB_COLS
PIECE_UNROLL = 4
BLOCK_PIECES = EXPERT_ROWS // PIECE
SRC_BLOCK = 1024
assert SRC_BLOCK % BLOCK_PIECES == 0 and SRC_BLOCK // BLOCK_PIECES > 2
PACKED = D_MODEL // 2
VMEM_LIMIT = 56 * 1024 * 1024


def _layer_norm(x):
    mu = jnp.mean(x, axis=-1, keepdims=True)
    xc = x - mu
    var = jnp.mean(xc * xc, axis=-1, keepdims=True)
    return xc * lax.rsqrt(var + LN_EPS)


def _silu(x):
    return x * jax.nn.sigmoid(x)


def _dot(a, b):
    return jnp.dot(a, b, preferred_element_type=F32)


def _full(shape):
    return pl.BlockSpec(shape, lambda *_: (0,) * len(shape), pipeline_mode=pl.Buffered(1))


def _pack_rows(y, rounded=False):
    bits = lax.bitcast_convert_type(y if rounded else y.astype(BF16).astype(F32), U32)
    return (bits[:, :PACKED] >> 16) | bits[:, PACKED:]


def _unpack_rows(u):
    lo = lax.bitcast_convert_type(u << 16, F32)
    hi = lax.bitcast_convert_type(u & jnp.uint32(0xFFFF0000), F32)
    return jnp.concatenate([lo, hi], axis=1).astype(BF16)


class _Stream:
    def __init__(self, b_ctx, s_ctx, b_lat, s_lat):
        self.tm = TOKEN_TILE
        assert s_ctx % self.tm == 0 and s_lat % self.tm == 0
        self.b_ctx, self.s_ctx, self.b_lat, self.s_lat = b_ctx, s_ctx, b_lat, s_lat
        self.n_ctx = b_ctx * s_ctx
        self.n = self.n_ctx + b_lat * s_lat
        self.tpb_c = s_ctx // self.tm
        self.tpb_l = s_lat // self.tm
        self.nct = self.n_ctx // self.tm
        self.tiles = self.n // self.tm
        self.ctx_row = b_lat

    def mod_row(self, i):
        return jnp.where(i >= self.nct, (i - self.nct) // self.tpb_l, self.ctx_row)

    def tile_in_seq(self, i):
        return jnp.where(i >= self.nct, (i - self.nct) % self.tpb_l, i % self.tpb_c)

    def tiles_per_seq(self, i):
        return jnp.where(i >= self.nct, self.tpb_l, self.tpb_c)

    def rope_block(self, i):
        return jnp.where(i >= self.nct, self.tpb_c + (i - self.nct) % self.tpb_l, i % self.tpb_c)


def _mod_kernel(c_ref, w_ref, b_ref, o_ref):
    o_ref[0] = _dot(_silu(c_ref[...]), w_ref[0]) + b_ref[0]


def _modulation(cond, w_ada, b_ada):
    depth, d, cols = w_ada.shape
    cb = cols // 4
    return pl.pallas_call(
        _mod_kernel,
        out_shape=jax.ShapeDtypeStruct((depth, MOD_ROWS, cols), F32),
        grid=(depth, cols // cb),
        in_specs=[pl.BlockSpec((MOD_ROWS, d), lambda l, j: (0, 0)),
                  pl.BlockSpec((1, d, cb), lambda l, j: (l, 0, j)),
                  pl.BlockSpec((1, 1, cb), lambda l, j: (l, 0, j))],
        out_specs=pl.BlockSpec((1, MOD_ROWS, cb), lambda l, j: (l, 0, j)),
        compiler_params=pltpu.CompilerParams(vmem_limit_bytes=VMEM_LIMIT),
        name="modulation",
    )(cond, w_ada, b_ada.reshape(depth, 1, cols))


def _premix_kernel(xc_ref, xl_ref, mod_ref, w_ref, cc_ref, sc_ref, cos_ref, sin_ref, z_ref, *outs, nct):
    d = D_MODEL
    is_lat = pl.program_id(0) >= nct
    mod = mod_ref[0]
    h = _layer_norm(jnp.where(is_lat, xl_ref[...], xc_ref[...])) * (1.0 + mod[:, d:2 * d]) + mod[:, 0:d]
    p = _dot(h.astype(BF16), w_ref[...])
    c0 = CONV_DIM
    z_ref[...] = (p[:, 0:c0] * jax.nn.sigmoid(p[:, c0:2 * c0])).astype(z_ref.dtype)
    q0 = 2 * c0
    k0 = q0 + Q_DIM
    v0 = k0 + KV_DIM
    f0 = v0 + KV_DIM
    cos = cos_ref[...]
    sin = sin_ref[...]
    first = (lax.broadcasted_iota(I32, cos.shape, 1) % ROPE_AXIS_DIM) < ROPE_HALF

    def rot(t):
        sw = jnp.where(first, pltpu.roll(t, LANES - ROPE_HALF, 1), pltpu.roll(t, ROPE_HALF, 1))
        return t * cos + sw * sin

    q = p[:, q0:k0] * ATTN_SCALE
    q = jnp.concatenate([rot(q[:, j * LANES:(j + 1) * LANES]) for j in range(Q_DIM // LANES)], axis=1)
    uf = p[:, f0:f0 + FOURIER_DIM].astype(BF16)
    results = [q.astype(BF16), rot(p[:, k0:v0]), p[:, v0:f0],
               _dot(uf, cc_ref[...]).astype(BF16), _dot(uf, sc_ref[...]).astype(BF16)]

    @pl.when(jnp.logical_not(is_lat))
    def _():
        for r, o_ref in zip(results, outs[0::2]):
            o_ref[...] = r

    @pl.when(is_lat)
    def _():
        for r, o_ref in zip(results, outs[1::2]):
            o_ref[...] = r


def _premix(st, x_ctx, x_lat, mod_l, w_pre, cc, sc, rope_cos, rope_sin):
    d = x_ctx.shape[1]
    tm = st.tm
    tok = lambda w: pl.BlockSpec((tm, w), lambda i: (i, 0))
    ctx_tok = lambda w: pl.BlockSpec((tm, w), lambda i: (jnp.minimum(i, st.nct - 1), 0))
    lat_tok = lambda w: pl.BlockSpec((tm, w), lambda i: (jnp.maximum(i - st.nct, 0), 0))
    rope = pl.BlockSpec((tm, LANES), lambda i: (st.rope_block(i), 0))
    n_lat = st.n - st.n_ctx
    out_shape = [jax.ShapeDtypeStruct((st.n, CONV_DIM), BF16)]
    out_specs = [tok(CONV_DIM)]
    for w, dt in ((Q_DIM, BF16), (KV_DIM, F32), (KV_DIM, F32), (FOURIER_DIM, BF16), (FOURIER_DIM, BF16)):
        out_shape += [jax.ShapeDtypeStruct((st.n_ctx, w), dt), jax.ShapeDtypeStruct((n_lat, w), dt)]
        out_specs += [ctx_tok(w), lat_tok(w)]
    return pl.pallas_call(
        functools.partial(_premix_kernel, nct=st.nct),
        out_shape=out_shape,
        grid=(st.tiles,),
        in_specs=[ctx_tok(d), lat_tok(d), pl.BlockSpec((1, 1, 6 * d), lambda i: (st.mod_row(i), 0, 0)),
                  _full(w_pre.shape), _full(cc.shape), _full(sc.shape), rope, rope],
        out_specs=out_specs,
        compiler_params=pltpu.CompilerParams(dimension_semantics=("arbitrary",), vmem_limit_bytes=VMEM_LIMIT),
        name="premix",
    )(x_ctx, x_lat, mod_l, w_pre, cc, sc, rope_cos, rope_sin)


def _sink_softmax_pv(s, sink_col, v):
    m = jnp.maximum(jnp.max(s, axis=1, keepdims=True), sink_col)
    p = jnp.exp(s - m)
    denom = jnp.sum(p, axis=1, keepdims=True) + jnp.exp(sink_col - m)
    return _dot(p.astype(BF16), v) / denom


def _attend_heads(sink_ref, q_ref, keys, vals, bias, o_ref):
    rows = q_ref.shape[0]
    for first in range(0, N_HEADS, ATTN_STACK):
        h = first // Q_PER_KV
        heads = list(range(first, first + ATTN_STACK))
        q = jnp.concatenate([q_ref[:, hd * HEAD_DIM:(hd + 1) * HEAD_DIM] for hd in heads], axis=0)
        s = lax.dot_general(q, keys[:, h * HEAD_DIM:(h + 1) * HEAD_DIM], (((1,), (1,)), ((), ())),
                            preferred_element_type=F32)
        if bias is not None:
            s = s + bias
        sink_col = jnp.concatenate([jnp.full((rows, 1), sink_ref[hd], F32) for hd in heads], axis=0)
        o = _sink_softmax_pv(s, sink_col, vals[:, h * HEAD_DIM:(h + 1) * HEAD_DIM])
        for g, hd in enumerate(heads):
            o_ref[:, hd * HEAD_DIM:(hd + 1) * HEAD_DIM] = o[g * rows:(g + 1) * rows].astype(o_ref.dtype)


def _ctx_attn_kernel(sink_ref, q_ref, k_ref, v_ref, o_ref):
    _attend_heads(sink_ref, q_ref, k_ref[...].astype(BF16), v_ref[...].astype(BF16), None, o_ref)


def _ctx_attention(sink, q, k, v, batch, seq):
    return pl.pallas_call(
        _ctx_attn_kernel,
        out_shape=jax.ShapeDtypeStruct((batch * seq, Q_DIM), BF16),
        grid=(batch,),
        in_specs=[pl.BlockSpec(memory_space=pltpu.SMEM),
                  pl.BlockSpec((seq, Q_DIM), lambda i: (i, 0)),
                  pl.BlockSpec((seq, KV_DIM), lambda i: (i, 0)),
                  pl.BlockSpec((seq, KV_DIM), lambda i: (i, 0))],
        out_specs=pl.BlockSpec((seq, Q_DIM), lambda i: (i, 0)),
        compiler_params=pltpu.CompilerParams(vmem_limit_bytes=VMEM_LIMIT),
        name="attn_ctx",
    )(sink, q, k, v)


def _lat_attn_kernel(sink_ref, q_ref, k_ref, v_ref, kc_ref, vc_ref, bias_ref, o_ref):
    blk = ATTN_BLOCK
    nblk = pl.num_programs(1) * ATTN_SUB
    ctx_k = kc_ref[...].astype(BF16)
    ctx_v = vc_ref[...].astype(BF16)
    for sub in range(ATTN_SUB):
        i = pl.program_id(1) * ATTN_SUB + sub
        starts = [pl.multiple_of(jnp.maximum(i - 1, 0) * blk, blk),
                  pl.multiple_of(i * blk, blk),
                  pl.multiple_of(jnp.minimum(i + 1, nblk - 1) * blk, blk)]
        keys = jnp.concatenate([k_ref[pl.ds(st, blk), :].astype(BF16) for st in starts] + [ctx_k], axis=0)
        vals = jnp.concatenate([v_ref[pl.ds(st, blk), :].astype(BF16) for st in starts] + [ctx_v], axis=0)
        bias = bias_ref[jnp.where(i == 0, 0, jnp.where(i == nblk - 1, 2, 1))]
        rows = pl.ds(sub * blk, blk)
        _attend_heads(sink_ref, q_ref.at[rows], keys, vals, bias, o_ref.at[rows])


def _window_bias(n_ctx_keys):
    blk = ATTN_BLOCK
    r = jnp.arange(blk)[:, None]
    j = jnp.arange(3 * blk)[None, :]
    band = (j >= r) & (j <= r + 2 * WINDOW)
    has_prev = j >= blk
    has_next = j < 2 * blk
    variants = [band & has_prev, band, band & has_next]
    out = []
    for m in variants:
        full = jnp.concatenate([m, jnp.ones((blk, n_ctx_keys), bool)], axis=1)
        out.append(jnp.tile(jnp.where(full, 0.0, NEG_INF).astype(F32), (ATTN_STACK, 1)))
    return jnp.stack(out)


def _lat_attention(sink, q, k, v, kc, vc, batch, seq):
    rows = ATTN_BLOCK * ATTN_SUB
    steps = seq // rows
    assert seq % rows == 0 and seq // ATTN_BLOCK >= 2 and WINDOW == ATTN_BLOCK
    pc = kc.shape[0] // batch
    bias = _window_bias(pc)
    return pl.pallas_call(
        _lat_attn_kernel,
        out_shape=jax.ShapeDtypeStruct((batch * seq, Q_DIM), BF16),
        grid=(batch, steps),
        in_specs=[pl.BlockSpec(memory_space=pltpu.SMEM),
                  pl.BlockSpec((rows, Q_DIM), lambda i, j: (i * steps + j, 0)),
                  pl.BlockSpec((seq, KV_DIM), lambda i, j: (i, 0)),
                  pl.BlockSpec((seq, KV_DIM), lambda i, j: (i, 0)),
                  pl.BlockSpec((pc, KV_DIM), lambda i, j: (i, 0)),
                  pl.BlockSpec((pc, KV_DIM), lambda i, j: (i, 0)),
                  _full(bias.shape)],
        out_specs=pl.BlockSpec((rows, Q_DIM), lambda i, j: (i * steps + j, 0)),
        compiler_params=pltpu.CompilerParams(vmem_limit_bytes=VMEM_LIMIT),
        name="attn_lat",
    )(sink, q, k, v, kc, vc, bias)


def _fourier_kernel(cn_ref, sn_ref, uc_ref, us_ref, o_ref):
    o_ref[...] = (_dot(cn_ref[...], uc_ref[...]) - _dot(sn_ref[...], us_ref[...])).astype(o_ref.dtype)


def _fourier_positions(cn, sn, uc, us, nb):
    seq = cn.shape[0]
    tf = min(512, seq)
    jt = seq // tf
    return pl.pallas_call(
        _fourier_kernel,
        out_shape=jax.ShapeDtypeStruct((nb * seq, FOURIER_DIM), BF16),
        grid=(jt, nb),
        in_specs=[pl.BlockSpec((tf, seq), lambda j, b: (j, 0)),
                  pl.BlockSpec((tf, seq), lambda j, b: (j, 0)),
                  pl.BlockSpec((seq, FOURIER_DIM), lambda j, b: (b, 0)),
                  pl.BlockSpec((seq, FOURIER_DIM), lambda j, b: (b, 0))],
        out_specs=pl.BlockSpec((tf, FOURIER_DIM), lambda j, b: (b * jt + j, 0)),
        compiler_params=pltpu.CompilerParams(vmem_limit_bytes=VMEM_LIMIT),
        name="fourier",
    )(cn, sn, uc, us)


def _postmix_kernel(xc_ref, xl_ref, mod_ref, z_ref, zp_ref, zn_ref, attn_c_ref, attn_l_ref, four_c_ref, four_l_ref,
                    wg_ref, cw_ref, cb_ref, cg_ref,
                    cbeta_ref, wco_ref, wao_ref, wfo_ref, wo_ref, g1_ref, b1_ref, rwh_ref, rwl_ref, rb_ref,
                    sg_ref, su_ref, sd_ref,
                    r0_ref, xs_ref, pos_ref, wts_ref, tab_ref, cnt_ref, zs_ref, carry_ref, *, st):
    d = D_MODEL
    i = pl.program_id(0)
    tm = xc_ref.shape[0]
    is_lat = i >= st.nct
    attn = jnp.where(is_lat, attn_l_ref[...], attn_c_ref[...])
    four = jnp.where(is_lat, four_l_ref[...], four_c_ref[...])
    x = jnp.where(is_lat, xl_ref[...], xc_ref[...])
    mod = mod_ref[0]
    shift1, scale1, gate1, shift2, scale2, gate2 = [mod[:, j * d:(j + 1) * d] for j in range(6)]
    h = (_layer_norm(x) * (1.0 + scale1) + shift1).astype(BF16)
    gates = jax.nn.sigmoid(_dot(h, wg_ref[...]))

    tib = st.tile_in_seq(i)
    zs_ref[0:CONV_HALO, :] = zp_ref[...].astype(F32) * jnp.where(tib != 0, 1.0, 0.0)
    zs_ref[CONV_HALO + tm:, :] = zn_ref[...].astype(F32) * jnp.where(tib != st.tiles_per_seq(i) - 1, 1.0, 0.0)
    zs_ref[CONV_HALO:CONV_HALO + tm, :] = z_ref[...].astype(F32)
    acc = jnp.broadcast_to(cb_ref[...], (tm, CONV_DIM))
    first_off = CONV_HALO - CONV_PAD
    window = zs_ref[...]
    rows = window.shape[0]
    for r in range(SUBLANES):
        offs = [o for o in range(first_off, first_off + CONV_WIDTH) if o % SUBLANES == r]
        shifted = window if r == 0 else pltpu.roll(window, rows - r, 0)
        for o in offs:
            acc = acc + cw_ref[o - first_off:o - first_off + 1, :] * shifted[o - r:o - r + tm]
    conv = _silu(_layer_norm(acc) * cg_ref[...] + cbeta_ref[...]).astype(BF16)

    merged = (gates[:, 0:d] * _dot(conv, wco_ref[...])
              + gates[:, d:2 * d] * _dot(attn, wao_ref[...])
              + gates[:, 2 * d:3 * d] * _dot(four, wfo_ref[...]))
    mix = _dot(merged.astype(BF16), wo_ref[...])
    x1 = _layer_norm(DEEPNORM_ALPHA * x + gate1 * mix) * g1_ref[...] + b1_ref[...]
    h2 = _layer_norm(x1) * (1.0 + scale2) + shift2

    h2_hi = h2.astype(BF16)
    h2_lo = (h2 - h2_hi.astype(F32)).astype(BF16)
    nt = (((1,), (1,)), ((), ()))
    logits_t = (lax.dot_general(rwh_ref[...], h2_hi, nt, preferred_element_type=F32)
                + lax.dot_general(rwh_ref[...], h2_lo, nt, preferred_element_type=F32)
                + lax.dot_general(rwl_ref[...], h2_hi, nt, preferred_element_type=F32))
    scores_t = jax.nn.sigmoid(logits_t)
    sel_t = scores_t + rb_ref[...]
    expert_t = lax.broadcasted_iota(I32, (LANES, tm), 0)
    onehots_t, top_s = [], []
    for _ in range(TOP_K):
        mx = jnp.max(sel_t, axis=0, keepdims=True)
        idx = jnp.min(jnp.where(sel_t == mx, expert_t, LANES), axis=0, keepdims=True)
        oh = expert_t == idx
        onehots_t.append(oh)
        top_s.append(jnp.sum(jnp.where(oh, scores_t, 0.0), axis=0, keepdims=True))
        sel_t = jnp.where(oh, -jnp.inf, sel_t)
    total = top_s[0]
    for s_k in top_s[1:]:
        total = total + s_k

    @pl.when(i == 0)
    def _():
        carry_ref[...] = jnp.zeros_like(carry_ref)

    chosen_t = jnp.zeros((LANES, tm), F32)
    for oh in onehots_t:
        chosen_t = jnp.where(oh, 1.0, chosen_t)
    chosen_tb = chosen_t.astype(BF16)
    tok_r = lax.broadcasted_iota(I32, (tm, tm), 0)
    tok_c = lax.broadcasted_iota(I32, (tm, tm), 1)
    prefix_t = _dot(chosen_tb, jnp.where(tok_r < tok_c, 1.0, 0.0).astype(BF16))
    cpad_t = jnp.floor((jnp.sum(chosen_t, axis=1, keepdims=True) + (PIECE - 1)) * (1.0 / PIECE)) * PIECE
    exp_r = lax.broadcasted_iota(I32, (LANES, LANES), 0)
    exp_c = lax.broadcasted_iota(I32, (LANES, LANES), 1)
    cstart_t = _dot(jnp.where(exp_c < exp_r, 1.0, 0.0).astype(BF16),
                    jnp.broadcast_to(cpad_t, (LANES, tm)).astype(BF16))

    where_t = prefix_t + cstart_t
    krow = lax.broadcasted_iota(I32, (LANES, tm), 0)
    pos_t = jnp.zeros((LANES, tm), F32)
    wts_t = jnp.zeros((LANES, tm), F32)
    for k, oh in enumerate(onehots_t):
        pos_t = jnp.where(krow == k, jnp.sum(jnp.where(oh, where_t, 0.0), axis=0, keepdims=True), pos_t)
        wts_t = jnp.where(krow == k, ROUTED_SCALE * top_s[k] / total, wts_t)
    pos_ref[...] = pos_t.T.astype(I32)
    wts_ref[...] = wts_t.T

    chosen = chosen_t.T
    count = jnp.sum(chosen, axis=0, keepdims=True)
    cpad = jnp.floor((count + (PIECE - 1)) * (1.0 / PIECE)) * PIECE
    er = lax.broadcasted_iota(I32, (LANES, LANES), 0)
    ec = lax.broadcasted_iota(I32, (LANES, LANES), 1)
    cstart = _dot(jnp.broadcast_to(cpad, (SUBLANES, LANES)), jnp.where(er < ec, 1.0, 0.0))[0:1]
    gbase = carry_ref[...]
    carry_ref[...] = gbase + cpad
    cnt_ref[...] = gbase + cpad
    pieces = jnp.sum(cpad, axis=1, keepdims=True) * (1.0 / PIECE)
    trow = lax.broadcasted_iota(I32, (SUBLANES, LANES), 0)
    per_expert = jnp.where(trow == 0, cstart, jnp.where(trow == 1, cpad, jnp.where(trow == 2, gbase, pieces)))
    piece_row = (lax.broadcasted_iota(I32, (SORT_PIECES, LANES), 0) * PIECE).astype(F32)
    owns = jnp.where((piece_row >= cstart) & (piece_row < cstart + cpad), 1.0, 0.0).astype(BF16)
    expert_id = lax.broadcasted_iota(I32, (SUBLANES, LANES), 1).astype(BF16)
    piece_expert = lax.dot_general(expert_id, owns, (((1,), (1,)), ((), ())), preferred_element_type=F32)
    tab_ref[...] = jnp.concatenate([per_expert, piece_expert], axis=1).astype(I32)

    slot_e = lax.broadcasted_iota(I32, (SORT_ROWS, LANES), 0).astype(F32)
    slot_expert = jnp.where((slot_e >= cstart) & (slot_e < cstart + cpad), 1.0, 0.0).astype(BF16)
    rank_t = jnp.where(chosen_t > 0.0, prefix_t, -1.0)
    looked = _dot(slot_expert, jnp.concatenate([rank_t.astype(BF16), cstart_t.astype(BF16)], axis=1))
    slot_t = lax.broadcasted_iota(I32, (SORT_ROWS, tm), 0).astype(F32)
    perm = jnp.where(looked[:, :tm] + looked[:, tm:] == slot_t, 1.0, 0.0).astype(BF16)
    h2b = h2.astype(BF16)
    xs_ref[...] = _pack_rows(_dot(perm, h2b), rounded=True)

    shared = _dot((_silu(_dot(h2b, sg_ref[...])) * _dot(h2b, su_ref[...])).astype(BF16), sd_ref[...])
    r0_ref[...] = DEEPNORM_ALPHA * x1 + gate2 * shared


def _postmix(st, x_ctx, x_lat, mod_l, z, attn_c, attn_l, four_c, four_l, lw):
    n, d = st.n, x_ctx.shape[1]
    tm = st.tm
    hb = tm // CONV_HALO
    last_halo = n // CONV_HALO - 1
    tok = lambda w: pl.BlockSpec((tm, w), lambda i: (i, 0))
    ctx_tok = lambda w: pl.BlockSpec((tm, w), lambda i: (jnp.minimum(i, st.nct - 1), 0))
    lat_tok = lambda w: pl.BlockSpec((tm, w), lambda i: (jnp.maximum(i - st.nct, 0), 0))
    in_specs = [ctx_tok(d), lat_tok(d), pl.BlockSpec((1, 1, 6 * d), lambda i: (st.mod_row(i), 0, 0)), tok(CONV_DIM),
                pl.BlockSpec((CONV_HALO, CONV_DIM), lambda i: (jnp.maximum(i * hb - 1, 0), 0)),
                pl.BlockSpec((CONV_HALO, CONV_DIM), lambda i: (jnp.minimum((i + 1) * hb, last_halo), 0)),
                ctx_tok(Q_DIM), lat_tok(Q_DIM), ctx_tok(FOURIER_DIM), lat_tok(FOURIER_DIM)]
    args = [x_ctx, x_lat, mod_l, z, z, z, attn_c, attn_l, four_c, four_l]
    weights = [lw['w_gates'], lw['conv_w'], lw['conv_b'], lw['conv_ln_g'], lw['conv_ln_b'], lw['w_conv_out'],
               lw['w_attn_out'], lw['w_fourier_out'], lw['w_out'], lw['ln1_g'], lw['ln1_b'],
               lw['router_hi'], lw['router_lo'], lw['router_bias'],
               lw['shared_w_gate'], lw['shared_w_up'], lw['shared_w_down']]
    in_specs += [_full(w.shape) for w in weights]
    args += weights
    return pl.pallas_call(
        functools.partial(_postmix_kernel, st=st),
        out_shape=[jax.ShapeDtypeStruct((n, d), F32),
                   jax.ShapeDtypeStruct((st.tiles * SORT_ROWS, PACKED), U32),
                   jax.ShapeDtypeStruct((n, LANES), I32),
                   jax.ShapeDtypeStruct((n, LANES), F32),
                   jax.ShapeDtypeStruct((st.tiles * SUBLANES, TAB_COLS), I32),
                   jax.ShapeDtypeStruct((1, LANES), F32)],
        grid=(st.tiles,),
        in_specs=in_specs,
        out_specs=[tok(d), pl.BlockSpec((SORT_ROWS, PACKED), lambda i: (i, 0)), tok(LANES), tok(LANES),
                   pl.BlockSpec((SUBLANES, TAB_COLS), lambda i: (i, 0)), _full((1, LANES))],
        scratch_shapes=[pltpu.VMEM((tm + 2 * CONV_HALO, CONV_DIM), F32),
                        pltpu.VMEM((1, LANES), F32)],
        compiler_params=pltpu.CompilerParams(dimension_semantics=("arbitrary",), vmem_limit_bytes=VMEM_LIMIT),
        name="postmix",
    )(*args)


RING = 3


def _tab(tab_ref, r, c):
    return tab_ref[r * TAB_COLS + c]


def _for_each_piece(pstart_ref, tab_ref, fn):
    def one(p, prio):
        e = _tab(tab_ref, 0, LANES + p)
        local_row = p * PIECE
        sorted_row = pstart_ref[e] + _tab(tab_ref, 2, e) + (local_row - _tab(tab_ref, 0, e))
        fn(pl.multiple_of(local_row, PIECE), pl.multiple_of(sorted_row, PIECE), prio)

    n = _tab(tab_ref, 3, 0)
    groups = n // PIECE_UNROLL

    def group(g, c):
        for u in range(PIECE_UNROLL):
            one(g * PIECE_UNROLL + u, u % 2)
        return c

    lax.fori_loop(0, groups, group, 0)
    lax.fori_loop(groups * PIECE_UNROLL, n, lambda p, c: (one(p, 0), c)[1], 0)


def _wait_pieces(n_pieces, src_ref, dst_ref, sem):
    @pl.when(n_pieces > 0)
    def _():
        rows = n_pieces * PIECE
        pltpu.make_async_copy(src_ref.at[pl.ds(0, rows)], dst_ref.at[pl.ds(0, rows)], sem).wait()


def _experts_kernel(be_ref, nu_ref, src_ref, xs_ref, wg_ref, wu_ref, wd_ref, ys_ref, xbuf_ref, sem, wgb_ref, wub_ref,
                    wdb_ref):
    j = pl.program_id(0)
    n_used = nu_ref[0]
    used = j < n_used

    def gather(block):
        slot = block % RING
        first = (block * BLOCK_PIECES) % SRC_BLOCK

        def group(g, c):
            for u in range(PIECE_UNROLL):
                p = g * PIECE_UNROLL + u
                src_row = pl.multiple_of(src_ref[first + p], PIECE)
                pltpu.make_async_copy(xs_ref.at[pl.ds(src_row, PIECE)],
                                      xbuf_ref.at[slot, pl.ds(pl.multiple_of(p * PIECE, PIECE), PIECE)],
                                      sem.at[slot]).start(priority=u % 2)
            return c

        lax.fori_loop(0, BLOCK_PIECES // PIECE_UNROLL, group, 0)

    @pl.when(j == 0)
    def _():
        gather(0)

        @pl.when(n_used > 1)
        def _():
            gather(1)

    @pl.when(j + 2 < n_used)
    def _():
        gather(j + 2)

    @pl.when(jnp.logical_or(j == 0, be_ref[j] != be_ref[jnp.maximum(j - 1, 0)]))
    def _():
        wgb_ref[...] = wg_ref[0, 0].astype(BF16)
        wub_ref[...] = wu_ref[0, 0].astype(BF16)
        wdb_ref[...] = wd_ref[0, 0].astype(BF16)

    @pl.when(used)
    def _():
        slot = j % RING
        pltpu.make_async_copy(xs_ref.at[pl.ds(0, EXPERT_ROWS)], xbuf_ref.at[slot], sem.at[slot]).wait()
        x = _unpack_rows(xbuf_ref[slot])
        a = (_silu(_dot(x, wgb_ref[...])) * _dot(x, wub_ref[...])).astype(BF16)
        ys_ref[...] = _pack_rows(_dot(a, wdb_ref[...]))

    @pl.when(jnp.logical_not(used))
    def _():
        ys_ref[...] = jnp.zeros_like(ys_ref)


def _experts(block_e, n_used, src_rows, xs_local, n_blocks, w_gate, w_up, w_down, layer):
    d = w_gate.shape[2]
    r = EXPERT_ROWS
    n_rows = n_blocks * r
    last_table = src_rows.shape[0] // SRC_BLOCK - 1
    ahead = lambda j, be, nu: (jnp.minimum((j + 2) * BLOCK_PIECES // SRC_BLOCK, last_table),)
    return pl.pallas_call(
        _experts_kernel,
        out_shape=jax.ShapeDtypeStruct((n_rows, PACKED), U32),
        grid_spec=pltpu.PrefetchScalarGridSpec(
            num_scalar_prefetch=2,
            grid=(n_blocks,),
            in_specs=[pl.BlockSpec((SRC_BLOCK,), ahead, memory_space=pltpu.SMEM),
                      pl.BlockSpec(memory_space=pl.ANY),
                      pl.BlockSpec((1, 1, d, EXPERT_DIM), lambda j, be, nu: (layer, be[j], 0, 0)),
                      pl.BlockSpec((1, 1, d, EXPERT_DIM), lambda j, be, nu: (layer, be[j], 0, 0)),
                      pl.BlockSpec((1, 1, EXPERT_DIM, d), lambda j, be, nu: (layer, be[j], 0, 0))],
            out_specs=pl.BlockSpec((r, PACKED), lambda j, be, nu: (j, 0)),
            scratch_shapes=[pltpu.VMEM((RING, r, PACKED), U32), pltpu.SemaphoreType.DMA((RING,)),
                            pltpu.VMEM((d, EXPERT_DIM), BF16), pltpu.VMEM((d, EXPERT_DIM), BF16),
                            pltpu.VMEM((EXPERT_DIM, d), BF16)]),
        compiler_params=pltpu.CompilerParams(dimension_semantics=("arbitrary",), vmem_limit_bytes=VMEM_LIMIT),
        name="moe_experts",
    )(block_e, n_used, src_rows, xs_local, w_gate, w_up, w_down)


def _combine_kernel(pstart_ref, tab_ref, tab_next_ref, r0_ref, pos_ref, wts_ref, mod_ref, g2_ref, b2_ref, ys_ref,
                    oc_ref, ol_ref, buf_ref, sem, *, nct):
    tm, d = r0_ref.shape
    i = pl.program_id(0)
    slot = i % 2

    def piece_copy(slot_idx, local_row, sorted_row):
        return pltpu.make_async_copy(ys_ref.at[pl.ds(sorted_row, PIECE)],
                                     buf_ref.at[slot_idx, pl.ds(local_row, PIECE)], sem.at[slot_idx])

    @pl.when(i == 0)
    def _():
        buf_ref[...] = jnp.zeros_like(buf_ref)
        _for_each_piece(pstart_ref, tab_ref, lambda a, g, prio: piece_copy(0, a, g).start(priority=prio))

    @pl.when(i + 1 < pl.num_programs(0))
    def _():
        _for_each_piece(pstart_ref, tab_next_ref,
                        lambda a, g, prio: piece_copy(1 - slot, a, g).start(priority=prio))

    _wait_pieces(_tab(tab_ref, 3, 0), ys_ref, buf_ref.at[slot], sem.at[slot])

    pos = pos_ref[...]
    wts = wts_ref[...]
    col = lax.broadcasted_iota(I32, (tm, SORT_ROWS), 1)
    weight = jnp.zeros((tm, SORT_ROWS), F32)
    for k in range(TOP_K):
        weight = jnp.where(col == pos[:, k:k + 1], wts[:, k:k + 1], weight)
    routed = _dot(weight.astype(BF16), _unpack_rows(buf_ref[slot]))
    gate2 = mod_ref[0][:, 5 * d:6 * d]
    out = _layer_norm(r0_ref[...] + gate2 * routed) * g2_ref[...] + b2_ref[...]
    @pl.when(i < nct)
    def _():
        oc_ref[...] = out

    @pl.when(i >= nct)
    def _():
        ol_ref[...] = out


def _combine(st, pstart, tab, r0, pos, wts, mod_l, ln2_g, ln2_b, ys_sorted):
    n, d = r0.shape
    tm = st.tm
    tok = lambda w: pl.BlockSpec((tm, w), lambda i, *_: (i, 0))
    last = st.tiles - 1
    out_shape = [jax.ShapeDtypeStruct((st.n_ctx, d), F32), jax.ShapeDtypeStruct((n - st.n_ctx, d), F32)]
    out_specs = [pl.BlockSpec((tm, d), lambda i, *_: (jnp.minimum(i, st.nct - 1), 0)),
                 pl.BlockSpec((tm, d), lambda i, *_: (jnp.maximum(i - st.nct, 0), 0))]
    return pl.pallas_call(
        functools.partial(_combine_kernel, nct=st.nct),
        out_shape=out_shape,
        grid_spec=pltpu.PrefetchScalarGridSpec(
            num_scalar_prefetch=1,
            grid=(st.tiles,),
            in_specs=[pl.BlockSpec((TAB_SIZE,), lambda i, *_: (i,), memory_space=pltpu.SMEM),
                      pl.BlockSpec((TAB_SIZE,), lambda i, *_: (jnp.minimum(i + 1, last),), memory_space=pltpu.SMEM),
                      tok(d), tok(LANES), tok(LANES),
                      pl.BlockSpec((1, 1, 6 * d), lambda i, *_: (st.mod_row(i), 0, 0)),
                      _full((1, d)), _full((1, d)),
                      pl.BlockSpec(memory_space=pl.ANY)],
            out_specs=out_specs,
            scratch_shapes=[pltpu.VMEM((2, SORT_ROWS, PACKED), U32), pltpu.SemaphoreType.DMA((2,))]),
        compiler_params=pltpu.CompilerParams(dimension_semantics=("arbitrary",), vmem_limit_bytes=VMEM_LIMIT),
        name="moe_combine",
    )(pstart, tab, tab, r0, pos, wts, mod_l, ln2_g, ln2_b, ys_sorted)


def _moe_routed(st, xs_local, tab, counts, lw, layer):
    r = EXPERT_ROWS
    n_blocks = -(-(st.tiles * (st.tm * TOP_K + N_EXPERTS * (PIECE - 1))) // r) + N_EXPERTS
    cnt = counts[0, :N_EXPERTS].astype(I32)
    padded = (cnt + r - 1) // r * r
    pend = jnp.cumsum(padded)
    pstart = (pend - padded).astype(I32)
    n_used = (pend[-1] // r).astype(I32)
    blocks = jnp.arange(n_blocks, dtype=I32)
    block_i = jnp.minimum(blocks, n_used - 1)
    block_e = jnp.sum((pend[None, :] <= (block_i * r)[:, None]).astype(I32), axis=1)
    block_e = jnp.minimum(block_e, N_EXPERTS - 1).astype(I32)
    t3 = tab.reshape(st.tiles, SUBLANES, TAB_COLS)
    cstart_et = t3[:, 0, :N_EXPERTS].T.astype(F32)
    gbase_et = t3[:, 2, :N_EXPERTS].T.astype(F32)
    gend_et = gbase_et + t3[:, 1, :N_EXPERTS].T.astype(F32)
    n_entries = -(-(n_blocks * BLOCK_PIECES) // SRC_BLOCK) * SRC_BLOCK
    entry_is = (block_e[:, None] == jnp.arange(N_EXPERTS, dtype=I32)[None, :]).astype(F32)
    entry_is = jnp.broadcast_to(entry_is[:, None, :], (n_blocks, BLOCK_PIECES, N_EXPERTS)).reshape(-1, N_EXPERTS)
    entry_is = jnp.pad(entry_is, ((0, n_entries - n_blocks * BLOCK_PIECES), (0, 0)))
    exact = lax.Precision.HIGHEST
    row_in_expert = (jnp.arange(n_entries, dtype=I32) * PIECE).astype(F32) - jnp.dot(
        entry_is, pstart.astype(F32), precision=exact)
    ends = jnp.dot(entry_is, gend_et, precision=exact)
    tile = jnp.sum((ends <= row_in_expert[:, None]).astype(I32), axis=1)
    in_use = row_in_expert < ends[:, -1]
    shift = jnp.dot(entry_is, cstart_et - gbase_et, precision=exact)
    shift = jnp.sum(jnp.where(jnp.arange(st.tiles, dtype=I32)[None, :] == tile[:, None], shift, 0.0), axis=1)
    src_rows = jnp.where(in_use, tile * SORT_ROWS + (shift + row_in_expert).astype(I32), SORT_ROWS - PIECE)
    ys = _experts(block_e, n_used.reshape(1), src_rows, xs_local, n_blocks,
                  lw['exp_w_gate'], lw['exp_w_up'], lw['exp_w_down'], layer)
    return pstart, ys


def _channel_dft():
    j = jnp.arange(FOURIER_DIM)
    same = (j[:, None] // FOURIER_GROUP_DIM) == (j[None, :] // FOURIER_GROUP_DIM)
    prod = ((j[:, None] % FOURIER_GROUP_DIM) * (j[None, :] % FOURIER_GROUP_DIM)) % FOURIER_GROUP_DIM
    ang = (2.0 * jnp.pi / FOURIER_GROUP_DIM) * prod.astype(F32)
    scale = FOURIER_GROUP_DIM ** -0.5
    return (jnp.where(same, jnp.cos(ang) * scale, 0.0).astype(BF16),
            jnp.where(same, jnp.sin(ang) * scale, 0.0).astype(BF16))


def _position_dft(n):
    m = 1 << (n.bit_length() // 2)
    assert n % m == 0
    k = jnp.arange(n, dtype=I32)[None, :]
    ang_hi = (2.0 * jnp.pi / n) * ((jnp.arange(n // m, dtype=I32)[:, None] * m * k) % n).astype(F32)
    ang_lo = (2.0 * jnp.pi / n) * ((jnp.arange(m, dtype=I32)[:, None] * k) % n).astype(F32)
    scale = n ** -0.5
    ch, sh = (jnp.cos(ang_hi) * scale)[:, None, :], (jnp.sin(ang_hi) * scale)[:, None, :]
    cl, sl = jnp.cos(ang_lo)[None, :, :], jnp.sin(ang_lo)[None, :, :]
    return ((ch * cl - sh * sl).reshape(n, n).astype(BF16), (sh * cl + ch * sl).reshape(n, n).astype(BF16))


def _rope_tables(s_ctx, n_lat):
    t = jnp.arange(n_lat)
    inv = ROPE_BASE ** (-jnp.arange(ROPE_HALF, dtype=F32) / ROPE_HALF)
    row_ang = (t // GRID_W).astype(F32)[:, None] * inv[None, :]
    col_ang = (t % GRID_W).astype(F32)[:, None] * inv[None, :]
    cos = jnp.concatenate([jnp.cos(row_ang)] * 2 + [jnp.cos(col_ang)] * 2, axis=1)
    sin = jnp.concatenate([-jnp.sin(row_ang), jnp.sin(row_ang), -jnp.sin(col_ang), jnp.sin(col_ang)], axis=1)
    reps = LANES // HEAD_DIM
    cos = jnp.concatenate([jnp.ones((s_ctx, LANES), F32), jnp.tile(cos, (1, reps))], axis=0)
    sin = jnp.concatenate([jnp.zeros((s_ctx, LANES), F32), jnp.tile(sin, (1, reps))], axis=0)
    return cos, sin


def kernel(x_prompt, x_sample, cache_k, cache_v, c, c_ctx, w_ada, b_ada, w_in, conv_w, conv_b, conv_ln_g, conv_ln_b, w_conv_out, attn_sink, w_attn_out, w_fourier_out, w_out, post_ln_g, post_ln_b, router_w, router_bias, exp_w_gate, exp_w_up, exp_w_down, shared_w_gate, shared_w_up, shared_w_down):
    b_ctx, s_ctx, d = x_prompt.shape
    b_lat, s_lat, _ = x_sample.shape
    depth = w_in.shape[0]
    assert b_lat < MOD_ROWS and d == D_MODEL
    st = _Stream(b_ctx, s_ctx, b_lat, s_lat)
    nc = st.n_ctx

    cond = jnp.concatenate([c, c_ctx[None, :], jnp.zeros((MOD_ROWS - b_lat - 1, d), F32)], axis=0)
    mod = _modulation(cond, w_ada, b_ada)

    cc, sc = _channel_dft()
    cn_ctx, sn_ctx = _position_dft(s_ctx)
    cn_lat, sn_lat = _position_dft(s_lat)
    rope_cos, rope_sin = _rope_tables(s_ctx, s_lat)

    y_ctx = x_prompt.reshape(nc, d)
    y_lat = x_sample.reshape(b_lat * s_lat, d)
    new_k, new_v = [], []
    for l in range(depth):
        rw = jnp.pad(router_w[l].T, ((0, LANES - N_EXPERTS), (0, 0)))
        rw_hi = rw.astype(BF16)
        row = lambda a: a.reshape(1, -1)
        lw = dict(
            w_gates=w_in[l][:, PRE_COLS:].astype(BF16),
            conv_w=conv_w[l], conv_b=row(conv_b[l]), conv_ln_g=row(conv_ln_g[l]), conv_ln_b=row(conv_ln_b[l]),
            w_conv_out=w_conv_out[l].astype(BF16), w_attn_out=w_attn_out[l].astype(BF16),
            w_fourier_out=w_fourier_out[l].astype(BF16), w_out=w_out[l].astype(BF16),
            ln1_g=row(post_ln_g[l, 0]), ln1_b=row(post_ln_b[l, 0]),
            router_hi=rw_hi, router_lo=(rw - rw_hi.astype(F32)).astype(BF16),
            router_bias=jnp.broadcast_to(
                jnp.pad(router_bias[l], (0, LANES - N_EXPERTS), constant_values=-jnp.inf)[:, None],
                (LANES, TOKEN_TILE)),
            exp_w_gate=exp_w_gate, exp_w_up=exp_w_up, exp_w_down=exp_w_down,
            shared_w_gate=shared_w_gate[l].astype(BF16), shared_w_up=shared_w_up[l].astype(BF16),
            shared_w_down=shared_w_down[l].astype(BF16))
        mod_l = mod[l].reshape(MOD_ROWS, 1, 6 * d)
        sink = attn_sink[l]

        z, q_c, q_l, k_c, k_l, v_c, v_l, uc_c, uc_l, us_c, us_l = _premix(
            st, y_ctx, y_lat, mod_l, w_in[l][:, :PRE_COLS].astype(BF16), cc, sc, rope_cos, rope_sin)
        new_k.append(k_c.reshape(b_ctx, s_ctx, N_KV_HEADS, HEAD_DIM))
        new_v.append(v_c.reshape(b_ctx, s_ctx, N_KV_HEADS, HEAD_DIM))
        attn_c = _ctx_attention(sink, q_c, k_c, v_c, b_ctx, s_ctx)
        attn_l = _lat_attention(sink, q_l, k_l, v_l, cache_k[:, l].reshape(-1, KV_DIM),
                                cache_v[:, l].reshape(-1, KV_DIM), b_lat, s_lat)
        four_c = _fourier_positions(cn_ctx, sn_ctx, uc_c, us_c, b_ctx)
        four_l = _fourier_positions(cn_lat, sn_lat, uc_l, us_l, b_lat)
        r0, xs_local, pos, wts, tab, counts = _postmix(st, y_ctx, y_lat, mod_l, z, attn_c, attn_l, four_c, four_l, lw)
        tab = tab.reshape(-1)
        pstart, ys_sorted = _moe_routed(st, xs_local, tab, counts, lw, l)
        y_ctx, y_lat = _combine(st, pstart, tab, r0, pos, wts, mod_l, row(post_ln_g[l, 1]), row(post_ln_b[l, 1]),
                                ys_sorted)

    return (y_ctx.reshape(b_ctx, s_ctx, d), y_lat.reshape(b_lat, s_lat, d),
            jnp.stack(new_k, axis=1), jnp.stack(new_v, axis=1))
```
